```python
import jax, jax.numpy as jnp
from jax import lax
import numpy as np

D_MODEL = 1024
BATCH = 1
SEQ = 16384
DEPTH = 2
DEC_BATCH = 32
DEC_SEQ = 8
PAST_LEN = 16384
PAGE_SIZE = 128

HEAD_DIM = 64
N_HEADS_B = 6
C_B = N_HEADS_B * HEAD_DIM
N_GROUPS_A = 5
C_A = N_GROUPS_A * HEAD_DIM
N_GROUPS_C = 5
C_C = N_GROUPS_C * HEAD_DIM
MIX_WIDTH = C_A + C_B + C_C
PROJ_WIDTH = 3 * C_B + 2 * C_A + 2 * C_C
SPLITS = (C_B, 2 * C_B, 3 * C_B, 3 * C_B + C_A, 3 * C_B + 2 * C_A, 3 * C_B + 2 * C_A + C_C)
CONV_W = 31
DILATED_PATTERNS = ((128, 1), (512, 4), (2048, 16))
WIN_MAX = 2048
Q_BLOCK = 128
CHUNK = 128
N_EXPERTS = 32
TOP_K = 4
D_FF_EXPERT = 1024
SWIGLU_LIMIT = 7.0
SWIGLU_ALPHA = 1.702
MOE_BLOCK = 128
LN_EPS = 1e-5
DN_ALPHA = (2.0 * DEPTH) ** 0.25
DN_BETA = (8.0 * DEPTH) ** -0.25

kernel_name = 'hybrid_conv_dilated_gmlp_moe_step'


def layer_norm(x, g, b):
    xf = x.astype(jnp.float32)
    mu = jnp.mean(xf, axis=-1, keepdims=True)
    xc = xf - mu
    var = jnp.mean(xc * xc, axis=-1, keepdims=True)
    return (xc * lax.rsqrt(var + LN_EPS) * g.astype(jnp.float32) + b.astype(jnp.float32)).astype(x.dtype)


def conformer_conv_tail(a_full, w_dw, b_dw, g, b):
    y = lax.conv_general_dilated(a_full, w_dw[:, None, :].astype(a_full.dtype), window_strides=(1,),
                                 padding='VALID', dimension_numbers=('NWC', 'WIO', 'NWC'),
                                 feature_group_count=C_A)
    return jax.nn.silu(layer_norm(y + b_dw, g, b))


def dilated_branch(q, q_idx, k_all, v_all, slopes, window, dil):
    dist = jnp.arange(window // dil + 1) * dil
    idx = q_idx[:, None] - dist[None, :]
    valid = idx >= 0
    idx = jnp.maximum(idx, 0)
    kg = k_all[:, idx]
    vg = v_all[:, idx]
    s = jnp.einsum('bqhd,bqnhd->bqhn', q, kg).astype(jnp.float32) * (HEAD_DIM ** -0.5)
    s = s - slopes[:, None] * dist.astype(jnp.float32)[None, :]
    s = jnp.where(valid[None, :, None, :], s, -jnp.inf)
    m = jnp.max(s, axis=-1, keepdims=True)
    p = jnp.exp(s - m)
    l = jnp.sum(p, axis=-1, keepdims=True)
    o = jnp.einsum('bqhn,bqnhd->bqhd', (p / l).astype(vg.dtype), vg)
    return o, (m + jnp.log(l))[..., 0]


def dilated_attention(q, q_idx, k_all, v_all):
    slopes = jnp.exp2(-8.0 * jnp.arange(1, N_HEADS_B + 1, dtype=jnp.float32) / N_HEADS_B)

    def attend(qb, ib):
        outs, lses = [], []
        for window, dil in DILATED_PATTERNS:
            o, lse = dilated_branch(qb, ib, k_all, v_all, slopes, window, dil)
            outs.append(o)
            lses.append(lse)
        wts = jax.nn.softmax(jnp.stack(lses), axis=0)
        out = jnp.einsum('rbqh,rbqhd->bqhd', wts, jnp.stack(outs).astype(jnp.float32))
        return out.astype(qb.dtype)

    bsz, t = q.shape[:2]
    if t > Q_BLOCK and t % Q_BLOCK == 0:
        nb = t // Q_BLOCK
        qb = jnp.moveaxis(q.reshape(bsz, nb, Q_BLOCK, N_HEADS_B, HEAD_DIM), 1, 0)
        ib = q_idx.reshape(nb, Q_BLOCK)
        out = lax.map(lambda args: attend(args[0], args[1]), (qb, ib))
        return jnp.moveaxis(out, 0, 1).reshape(bsz, t, N_HEADS_B, HEAD_DIM)
    return attend(q, q_idx)


def spatial_gate(vn, w_sp, b_sp):
    bsz, t, _ = vn.shape
    nc = -(-t // CHUNK)
    vp = jnp.pad(vn, ((0, 0), (0, nc * CHUNK - t), (0, 0))).reshape(bsz, nc, CHUNK, N_GROUPS_C, HEAD_DIM)
    mask = jnp.tril(jnp.ones((CHUNK, CHUNK), dtype=bool))
    w = jnp.where(mask[None], w_sp, 0.0)
    out = jnp.einsum('gts,bnsgc->bntgc', w, vp) + jnp.swapaxes(b_sp, 0, 1)[None, None, :, :, None]
    return out.reshape(bsz, nc * CHUNK, C_C)[:, :t]


def moe(h, w_router, b_router, w1, b1, w2, b2):
    shp = h.shape
    x = h.reshape(-1, D_MODEL)
    t = x.shape[0]
    logits = (x @ w_router + b_router).astype(jnp.float32)
    top_v, top_e = lax.top_k(logits, TOP_K)
    gates = jax.nn.softmax(top_v, axis=-1)
    n = t * TOP_K
    e_flat = top_e.reshape(-1)
    tok_flat = jnp.repeat(jnp.arange(t, dtype=jnp.int32), TOP_K)
    g_flat = gates.reshape(-1)
    order = jnp.argsort(e_flat)
    e_s, tok_s, g_s = e_flat[order], tok_flat[order], g_flat[order]
    counts = jnp.bincount(e_flat, length=N_EXPERTS)
    starts = jnp.cumsum(counts) - counts
    padded = (counts + MOE_BLOCK - 1) // MOE_BLOCK * MOE_BLOCK
    pends = jnp.cumsum(padded)
    pstarts = pends - padded
    dest = pstarts[e_s] + (jnp.arange(n) - starts[e_s])
    n_blocks = -(-n // MOE_BLOCK) + N_EXPERTS
    n_slots = n_blocks * MOE_BLOCK
    slot_tok = jnp.zeros((n_slots,), jnp.int32).at[dest].set(tok_s)
    slot_g = jnp.zeros((n_slots,), x.dtype).at[dest].set(g_s.astype(x.dtype))
    blk_e = jnp.minimum(jnp.searchsorted(pends, jnp.arange(n_blocks) * MOE_BLOCK, side='right'), N_EXPERTS - 1)

    def expert_block(args):
        tok, g, e = args
        xb = x[tok]
        gu = xb @ w1[e] + b1[e]
        gate = jnp.minimum(gu[:, :D_FF_EXPERT], SWIGLU_LIMIT)
        up = jnp.clip(gu[:, D_FF_EXPERT:], -SWIGLU_LIMIT, SWIGLU_LIMIT)
        hh = (up + 1.0) * (gate * jax.nn.sigmoid(SWIGLU_ALPHA * gate))
        return (hh @ w2[e] + b2[e]) * g[:, None]

    y = lax.map(expert_block, (slot_tok.reshape(n_blocks, MOE_BLOCK), slot_g.reshape(n_blocks, MOE_BLOCK), blk_e))
    out = jnp.zeros_like(x).at[slot_tok].add(y.reshape(n_slots, D_MODEL))
    return out.reshape(shp)


def hybrid_layer(x, a_hist, k_hist, v_hist, w_in, b_in, w_dw, b_dw, ln_a_g, ln_a_b, ln_c_g, ln_c_b,
                 w_sp, b_sp, w_out, b_out, ln1_g, ln1_b, w_router, b_router, w1, b1, w2, b2, ln2_g, ln2_b):
    bsz, t, _ = x.shape
    p = x @ w_in + b_in
    q, k, v, a_val, a_gate, c_u, c_v = jnp.split(p, SPLITS, axis=-1)
    a = a_val * jax.nn.sigmoid(a_gate)
    a_full = jnp.concatenate([a_hist.astype(a.dtype), a], axis=1)
    a_out = conformer_conv_tail(a_full, w_dw, b_dw, ln_a_g, ln_a_b)
    new_a = a_full[:, -(CONV_W - 1):]
    q = q.reshape(bsz, t, N_HEADS_B, HEAD_DIM)
    k = k.reshape(bsz, t, N_HEADS_B, HEAD_DIM)
    v = v.reshape(bsz, t, N_HEADS_B, HEAD_DIM)
    k_all = jnp.concatenate([k_hist.astype(k.dtype), k], axis=1)
    v_all = jnp.concatenate([v_hist.astype(v.dtype), v], axis=1)
    q_idx = k_hist.shape[1] + jnp.arange(t, dtype=jnp.int32)
    o_b = dilated_attention(q, q_idx, k_all, v_all).reshape(bsz, t, C_B)
    vn = layer_norm(c_v, ln_c_g, ln_c_b)
    c_out = c_u * spatial_gate(vn, w_sp, b_sp)
    mix = jnp.concatenate([a_out, o_b, c_out], axis=-1) @ w_out + b_out
    h = layer_norm(DN_ALPHA * x + mix, ln1_g, ln1_b)
    y = layer_norm(DN_ALPHA * h + moe(h, w_router, b_router, w1, b1, w2, b2), ln2_g, ln2_b)
    return y, new_a, k, v, vn


def setup_inputs(seed: int = 0) -> dict:
    key = jax.random.key(seed)
    ks = jax.random.split(key, 32)

    def nrm(k, shape, scale):
        return jax.random.normal(k, shape, jnp.float32) * scale

    w_buf = min(WIN_MAX, PAST_LEN)
    L = DEPTH
    return {
        'x_prompt': nrm(ks[0], (BATCH, SEQ, D_MODEL), 1.0),
        'x_sample': nrm(ks[1], (DEC_BATCH, DEC_SEQ, D_MODEL), 1.0),
        'state_a_conv': nrm(ks[2], (L, DEC_BATCH, CONV_W - 1, C_A), 0.5),
        'cache_b_k': nrm(ks[3], (L, DEC_BATCH, w_buf, N_HEADS_B, HEAD_DIM), 1.0),
        'cache_b_v': nrm(ks[4], (L, DEC_BATCH, w_buf, N_HEADS_B, HEAD_DIM), 1.0),
        'w_in': nrm(ks[5], (L, D_MODEL, PROJ_WIDTH), D_MODEL ** -0.5),
        'b_in': nrm(ks[6], (L, PROJ_WIDTH), 0.02),
        'w_dw': nrm(ks[7], (L, CONV_W, C_A), CONV_W ** -0.5),
        'b_dw': nrm(ks[8], (L, C_A), 0.02),
        'ln_a_g': 1.0 + nrm(ks[9], (L, C_A), 0.05),
        'ln_a_b': nrm(ks[10], (L, C_A), 0.02),
        'ln_c_g': 1.0 + nrm(ks[11], (L, C_C), 0.05),
        'ln_c_b': nrm(ks[12], (L, C_C), 0.02),
        'w_sp': nrm(ks[13], (L, N_GROUPS_C, CHUNK, CHUNK), CHUNK ** -0.5),
        'b_sp': 1.0 + nrm(ks[14], (L, N_GROUPS_C, CHUNK), 0.1),
        'w_out': nrm(ks[15], (L, MIX_WIDTH, D_MODEL), DN_BETA * MIX_WIDTH ** -0.5),
        'b_out': nrm(ks[16], (L, D_MODEL), 0.02),
        'ln1_g': 1.0 + nrm(ks[17], (L, D_MODEL), 0.05),
        'ln1_b': nrm(ks[18], (L, D_MODEL), 0.02),
        'w_router': nrm(ks[19], (L, D_MODEL, N_EXPERTS), D_MODEL ** -0.5),
        'b_router': nrm(ks[20], (L, N_EXPERTS), 0.01),
        'w1': nrm(ks[21], (L, N_EXPERTS, D_MODEL, 2 * D_FF_EXPERT), D_MODEL ** -0.5),
        'b1': nrm(ks[22], (L, N_EXPERTS, 2 * D_FF_EXPERT), 0.02),
        'w2': nrm(ks[23], (L, N_EXPERTS, D_FF_EXPERT, D_MODEL), DN_BETA * D_FF_EXPERT ** -0.5),
        'b2': nrm(ks[24], (L, N_EXPERTS, D_MODEL), 0.02),
        'ln2_g': 1.0 + nrm(ks[25], (L, D_MODEL), 0.05),
        'ln2_b': nrm(ks[26], (L, D_MODEL), 0.02),
    }


def reference(x_prompt, x_sample, state_a_conv, cache_b_k, cache_b_v, w_in, b_in, w_dw, b_dw,
              ln_a_g, ln_a_b, ln_c_g, ln_c_b, w_sp, b_sp, w_out, b_out, ln1_g, ln1_b,
              w_router, b_router, w1, b1, w2, b2, ln2_g, ln2_b):
    bp, sp = x_prompt.shape[:2]
    keep_p = min(WIN_MAX, sp)
    xp, xs = x_prompt, x_sample
    a_p_l, a_s_l, kp_l, vp_l, ks_l, vs_l, cs_l = [], [], [], [], [], [], []
    for l in range(DEPTH):
        params = (w_in[l], b_in[l], w_dw[l], b_dw[l], ln_a_g[l], ln_a_b[l], ln_c_g[l], ln_c_b[l],
                  w_sp[l], b_sp[l], w_out[l], b_out[l], ln1_g[l], ln1_b[l], w_router[l], b_router[l],
                  w1[l], b1[l], w2[l], b2[l], ln2_g[l], ln2_b[l])
        a_hist0 = jnp.zeros((bp, CONV_W - 1, C_A), xp.dtype)
        kv_hist0 = jnp.zeros((bp, 0, N_HEADS_B, HEAD_DIM), xp.dtype)
        xp, a_p, k_p, v_p, _ = hybrid_layer(xp, a_hist0, kv_hist0, kv_hist0, *params)
        a_p_l.append(a_p)
        kp_l.append(k_p[:, sp - keep_p:])
        vp_l.append(v_p[:, sp - keep_p:])
        xs, a_s, k_s, v_s, vn_s = hybrid_layer(xs, state_a_conv[l], cache_b_k[l], cache_b_v[l], *params)
        a_s_l.append(a_s)
        ks_l.append(k_s)
        vs_l.append(v_s)
        cs_l.append(vn_s)
    return (xp, xs, jnp.stack(a_p_l), jnp.stack(a_s_l), jnp.stack(kp_l), jnp.stack(vp_l),
            jnp.stack(ks_l), jnp.stack(vs_l), jnp.stack(cs_l))
```

```python
import functools
import math

import jax
import jax.numpy as jnp
from jax import lax
from jax.experimental import pallas as pl
from jax.experimental.pallas import tpu as pltpu

F32 = jnp.float32
BF16 = jnp.bfloat16

D_MODEL = 1024
HEAD_DIM = 64
N_HEADS_B = 6
C_B = N_HEADS_B * HEAD_DIM
C_A = 320
C_C = 320
N_GROUPS_C = 5
CONV_W = 31
DILATIONS = (1, 4, 16)
BAND = 128
WIN_MAX = 2048
CHUNK = 128
N_EXPERTS = 32
TOP_K = 4
D_FF = 1024
SWIGLU_LIMIT = 7.0
SWIGLU_ALPHA = 1.702
LN_EPS = 1e-5
DEPTH = 2
DN_ALPHA = (2.0 * DEPTH) ** 0.25
DEC_SEQ = 8

LANES = 128
SUBLANES = 8
SEG = 384
N_SEG = 7
ROW_TILES = D_MODEL // LANES
assert ROW_TILES == SUBLANES

TT = 256
TM = 256
TC = 128
NEG = -1e30
VMEM_LIMIT = 48 * 1024 * 1024


def _params(n_axes=1):
    return pltpu.CompilerParams(dimension_semantics=("arbitrary",) * n_axes,
                                vmem_limit_bytes=VMEM_LIMIT)


def _full(a):
    nd = a.ndim
    return pl.BlockSpec(a.shape, lambda *_: (0,) * nd)


def _ln_valid(x, g, b, n_valid):
    col = lax.broadcasted_iota(jnp.int32, x.shape, 1)
    ok = col < n_valid
    mu = jnp.sum(jnp.where(ok, x, 0.0), axis=-1, keepdims=True) / n_valid
    xc = jnp.where(ok, x - mu, 0.0)
    var = jnp.sum(xc * xc, axis=-1, keepdims=True) / n_valid
    return xc * lax.rsqrt(var + LN_EPS) * g + b


def _ln_full(x, g, b):
    mu = jnp.mean(x, axis=-1, keepdims=True)
    xc = x - mu
    var = jnp.mean(xc * xc, axis=-1, keepdims=True)
    return xc * lax.rsqrt(var + LN_EPS) * g + b


def _proj_kernel(x_ref, w_ref, b_ref, lcg_ref, lcb_ref,
                 q_ref, k_ref, v_ref, a_ref, cu_ref, vn_ref):
    xb = x_ref[...].astype(BF16)

    def seg(i):
        lo = i * SEG
        return (jnp.dot(xb, w_ref[:, lo:lo + SEG], preferred_element_type=F32)
                + b_ref[:, lo:lo + SEG])

    q_ref[...] = seg(0)
    k_ref[...] = seg(1)
    v_ref[...] = seg(2)
    a_ref[...] = seg(3) * jax.nn.sigmoid(seg(4))
    cu_ref[...] = seg(5)
    vn_ref[...] = _ln_valid(seg(6), lcg_ref[...], lcb_ref[...], C_C)


def _proj_call(x, w_in_p, b_in_p, lcg, lcb):
    n = x.shape[0]
    row = lambda w: pl.BlockSpec((TT, w), lambda i: (i, 0))
    out = jax.ShapeDtypeStruct((n, SEG), F32)
    return pl.pallas_call(
        _proj_kernel, grid=(n // TT,),
        in_specs=[row(D_MODEL), _full(w_in_p), _full(b_in_p), _full(lcg), _full(lcb)],
        out_specs=[row(SEG)] * 6, out_shape=[out] * 6,
        compiler_params=_params(), name="in_proj")(x, w_in_p, b_in_p, lcg, lcb)


HALO = 32
CONV_CHUNK = 32


def _conv_tail(acc, bdw_ref, g_ref, b_ref):
    return jax.nn.silu(_ln_valid(acc + bdw_ref[...], g_ref[...], b_ref[...], C_A))


def _conv_prompt_kernel(halo_ref, a_ref, w_ref, bdw_ref, g_ref, b_ref, o_ref, buf, *, n_prompt):
    i = pl.program_id(0)

    @pl.when(i < n_prompt)
    def _():
        buf[0:HALO, :] = jnp.where(i > 0, halo_ref[...], 0.0)
        buf[HALO:, :] = a_ref[...]
        base = HALO - (CONV_W - 1)
        for r in range(0, TT, CONV_CHUNK):
            acc = jnp.zeros((CONV_CHUNK, SEG), F32)
            for j in range(CONV_W):
                acc = acc + w_ref[j:j + 1, :] * buf[base + r + j: base + r + j + CONV_CHUNK, :]
            o_ref[r:r + CONV_CHUNK, :] = _conv_tail(acc, bdw_ref, g_ref, b_ref)

    @pl.when(i >= n_prompt)
    def _():
        o_ref[...] = jnp.zeros_like(o_ref)


def _conv_prompt_call(a, w_dw_p, bdw, g, b, tp):
    n = a.shape[0]
    per = TT // HALO
    halo = pl.BlockSpec((HALO, SEG), lambda i: (jnp.maximum(i * per - 1, 0), 0))
    row = pl.BlockSpec((TT, SEG), lambda i: (i, 0))
    return pl.pallas_call(
        functools.partial(_conv_prompt_kernel, n_prompt=tp // TT), grid=(n // TT,),
        in_specs=[halo, row, _full(w_dw_p), _full(bdw), _full(g), _full(b)],
        out_specs=row, out_shape=jax.ShapeDtypeStruct((n, SEG), F32),
        scratch_shapes=[pltpu.VMEM((HALO + TT, SEG), F32)],
        compiler_params=_params(), name="conv_prompt")(a, a, w_dw_p, bdw, g, b)


SB = 8


def _conv_sample_kernel(hist_ref, a_ref, w_ref, bdw_ref, g_ref, b_ref, big_ref,
                        o_ref, newa_ref, buf):
    del big_ref
    bs = hist_ref.shape[0]
    buf[:, 0:HALO, :] = hist_ref[...]
    buf[:, HALO:, :] = a_ref[...].reshape(bs, DEC_SEQ, SEG)
    base = HALO - (CONV_W - 1)
    for s in range(0, bs, SB):
        acc = jnp.zeros((SB, DEC_SEQ, SEG), F32)
        for j in range(CONV_W):
            acc = acc + w_ref[j:j + 1, :][None] * buf[s:s + SB, base + j: base + j + DEC_SEQ, :]
        y = _conv_tail(acc.reshape(SB * DEC_SEQ, SEG), bdw_ref, g_ref, b_ref)
        o_ref[s * DEC_SEQ:(s + SB) * DEC_SEQ, :] = y
    newa_ref[...] = buf[:, DEC_SEQ:, :]


def _conv_sample_call(hist_p, a, w_dw_p, bdw, g, b, a_out, tp):
    bs = hist_p.shape[0]
    rows = bs * DEC_SEQ
    blk = pl.BlockSpec((rows, SEG), lambda i: (tp // rows, 0))
    return pl.pallas_call(
        _conv_sample_kernel, grid=(1,),
        in_specs=[_full(hist_p), blk, _full(w_dw_p), _full(bdw), _full(g), _full(b),
                  pl.BlockSpec(memory_space=pl.ANY)],
        out_specs=[blk, pl.BlockSpec((bs, HALO, SEG), lambda i: (0, 0, 0))],
        out_shape=[jax.ShapeDtypeStruct(a_out.shape, F32),
                   jax.ShapeDtypeStruct((bs, HALO, SEG), F32)],
        scratch_shapes=[pltpu.VMEM((bs, HALO + DEC_SEQ, SEG), F32)],
        input_output_aliases={6: 0},
        compiler_params=_params(), name="conv_sample")(hist_p, a, w_dw_p, bdw, g, b, a_out)


QB = 128
assert QB == BAND


def _attn_prompt_kernel(*refs, merge, n_prompt):
    o_ref, l_ref = refs[-2:]
    is_prompt = pl.program_id(1) < n_prompt
    pl.when(is_prompt)(functools.partial(_attn_prompt_block, refs, merge))

    @pl.when(jnp.logical_not(is_prompt))
    def _():
        o_ref[...] = jnp.zeros_like(o_ref)
        l_ref[...] = jnp.zeros_like(l_ref)


def _attn_prompt_block(refs, merge):
    if merge:
        (q_ref, kp_ref, kc_ref, vp_ref, vc_ref, bias_ref, op_ref, lp_ref, o_ref, l_ref) = refs
    else:
        (q_ref, kp_ref, kc_ref, vp_ref, vc_ref, bias_ref, o_ref, l_ref) = refs
    first = pl.program_id(1) == 0
    kcol = lax.broadcasted_iota(jnp.int32, (QB, 2 * QB), 1)
    no_prev = jnp.logical_and(first, kcol < QB)
    lane = lax.broadcasted_iota(jnp.int32, (QB, LANES), 1)
    lo_half = lane < HEAD_DIM
    for pair in range(N_HEADS_B // 2):
        cs = slice(pair * LANES, (pair + 1) * LANES)
        qp = q_ref[:, cs] * (HEAD_DIM ** -0.5)
        kk = jnp.concatenate([kp_ref[:, cs], kc_ref[:, cs]], axis=0).astype(BF16)
        vv = jnp.concatenate([vp_ref[:, cs], vc_ref[:, cs]], axis=0).astype(BF16)
        outs, lses = [], []
        for hh in range(2):
            keep = lo_half if hh == 0 else jnp.logical_not(lo_half)
            qm = jnp.where(keep, qp, 0.0).astype(BF16)
            s = lax.dot_general(qm, kk, (((1,), (1,)), ((), ())), preferred_element_type=F32)
            s = s + bias_ref[pair * 2 + hh]
            s = jnp.where(no_prev, NEG, s)
            m = jnp.max(s, axis=-1, keepdims=True)
            p = jnp.exp(s - m)
            l = jnp.sum(p, axis=-1, keepdims=True)
            pv = jnp.dot(p.astype(BF16), vv, preferred_element_type=F32)
            outs.append(pv / l)
            lses.append(m + jnp.log(l))
        o_new = jnp.where(lo_half, outs[0], outs[1])
        l_new = jnp.where(lo_half, lses[0], lses[1])
        if merge:
            l_old = lp_ref[:, cs]
            mx = jnp.maximum(l_old, l_new)
            w_old = jnp.exp(l_old - mx)
            w_new = jnp.exp(l_new - mx)
            tot = w_old + w_new
            o_new = (w_old * op_ref[:, cs] + w_new * o_new) / tot
            l_new = mx + jnp.log(tot)
        o_ref[:, cs] = o_new
        l_ref[:, cs] = l_new


def _attn_prompt_call(q, k, v, bias, prev, dil, tp):
    n = q.shape[0]
    view = lambda a: a.reshape(n // dil, dil * SEG)
    cur = pl.BlockSpec((QB, SEG), lambda r, m: (m, r))
    prv = pl.BlockSpec((QB, SEG), lambda r, m: (jnp.maximum(m - 1, 0), r))
    args = [view(q), view(k), view(k), view(v), view(v), bias]
    specs = [cur, prv, cur, prv, cur, _full(bias)]
    if prev is not None:
        args += [view(prev[0]), view(prev[1])]
        specs += [cur, cur]
    shape = jax.ShapeDtypeStruct((n // dil, dil * SEG), F32)
    o, l = pl.pallas_call(
        functools.partial(_attn_prompt_kernel, merge=prev is not None, n_prompt=tp // dil // QB),
        grid=(dil, pl.cdiv(n // dil, QB)), in_specs=specs, out_specs=[cur, cur],
        out_shape=[shape, shape], compiler_params=_params(2),
        name=f"attn_prompt_d{dil}")(*args)
    return o.reshape(n, SEG), l.reshape(n, SEG)


def _alibi_slopes():
    return [2.0 ** (-8.0 * (h + 1) / N_HEADS_B) for h in range(N_HEADS_B)]


def _prompt_bias(dil):
    i = jnp.arange(QB)[:, None]
    j = jnp.arange(2 * QB)[None, :]
    rel = i + QB - j
    ok = jnp.logical_and(rel >= 0, rel <= BAND)
    slopes = jnp.asarray(_alibi_slopes(), F32)[:, None, None]
    pen = -slopes * (rel * dil).astype(F32)[None]
    return jnp.where(ok[None], pen, NEG).astype(F32)


def _branch_multiplicity(dist):
    cnt = jnp.zeros(dist.shape, jnp.int32)
    for dil in DILATIONS:
        cnt = cnt + jnp.logical_and(dist % dil == 0, dist <= BAND * dil).astype(jnp.int32)
    return cnt


def _sample_bias(n_cache):
    t = jnp.arange(DEC_SEQ)[:, None]
    dist_c = n_cache + t - jnp.arange(n_cache)[None, :]
    dist_n = t - jnp.arange(LANES)[None, :]
    ok_n = jnp.logical_and(dist_n >= 0, jnp.arange(LANES)[None, :] < DEC_SEQ)
    slopes = jnp.asarray(_alibi_slopes(), F32)[:, None, None]

    def bias(dist, ok):
        mult = _branch_multiplicity(jnp.maximum(dist, 0))
        ok = jnp.logical_and(ok, mult > 0)
        val = -slopes * dist.astype(F32)[None] + jnp.log(jnp.maximum(mult, 1).astype(F32))[None]
        return jnp.where(ok[None], val, NEG).astype(F32)

    bc = bias(dist_c, jnp.ones(dist_c.shape, bool)).reshape(N_HEADS_B * DEC_SEQ, n_cache)
    bn = bias(dist_n, ok_n).reshape(N_HEADS_B * DEC_SEQ, LANES)
    return bc, bn


def _attn_sample_kernel(q_ref, kn_ref, vn_ref, kc_ref, vc_ref, bc_ref, bn_ref, big_ref, o_ref):
    del big_ref
    rows = N_HEADS_B * DEC_SEQ
    q = q_ref[0] * (HEAD_DIM ** -0.5)
    qrep = jnp.concatenate([q] * N_HEADS_B, axis=0)
    rh = lax.shift_right_logical(lax.broadcasted_iota(jnp.int32, (rows, SEG), 0), 3)
    ch = lax.shift_right_logical(lax.broadcasted_iota(jnp.int32, (rows, SEG), 1), 6)
    own = rh == ch
    qm = jnp.where(own, qrep, 0.0).astype(BF16)
    pad = jnp.zeros((LANES - DEC_SEQ, SEG), F32)
    kn = jnp.concatenate([kn_ref[0], pad], axis=0).astype(BF16)
    vn = jnp.concatenate([vn_ref[0], pad], axis=0).astype(BF16)
    nt = (((1,), (1,)), ((), ()))
    s_c = lax.dot_general(qm, kc_ref[0].astype(BF16), nt, preferred_element_type=F32) + bc_ref[...]
    s_n = lax.dot_general(qm, kn, nt, preferred_element_type=F32) + bn_ref[...]
    m = jnp.maximum(jnp.max(s_c, axis=-1, keepdims=True), jnp.max(s_n, axis=-1, keepdims=True))
    p_c = jnp.exp(s_c - m)
    p_n = jnp.exp(s_n - m)
    l = jnp.sum(p_c, axis=-1, keepdims=True) + jnp.sum(p_n, axis=-1, keepdims=True)
    r = (jnp.dot(p_c.astype(BF16), vc_ref[0].astype(BF16), preferred_element_type=F32)
         + jnp.dot(p_n.astype(BF16), vn, preferred_element_type=F32)) / l
    r = jnp.where(own, r, 0.0)
    o = r[0:DEC_SEQ]
    for h in range(1, N_HEADS_B):
        o = o + r[h * DEC_SEQ:(h + 1) * DEC_SEQ]
    o_ref[...] = o


def _attn_sample_call(q, k, v, cache_k, cache_v, bc, bn, o_big, tp):
    n = q.shape[0]
    bs, n_cache = cache_k.shape[0], cache_k.shape[1]
    v3 = lambda a: a.reshape(n // DEC_SEQ, DEC_SEQ, SEG)
    new = pl.BlockSpec((1, DEC_SEQ, SEG), lambda b: (tp // DEC_SEQ + b, 0, 0))
    cache = pl.BlockSpec((1, n_cache, SEG), lambda b: (b, 0, 0))
    return pl.pallas_call(
        _attn_sample_kernel, grid=(bs,),
        in_specs=[new, new, new, cache, cache, _full(bc), _full(bn),
                  pl.BlockSpec(memory_space=pl.ANY)],
        out_specs=pl.BlockSpec((DEC_SEQ, SEG), lambda b: (tp // DEC_SEQ + b, 0)),
        out_shape=jax.ShapeDtypeStruct(o_big.shape, F32),
        input_output_aliases={7: 0},
        compiler_params=_params(), name="attn_sample")(
            v3(q), v3(k), v3(v), cache_k, cache_v, bc, bn, o_big)


def _gate_kernel(*refs, rows, sub, aliased):
    if aliased:
        vn_ref, cu_ref, w_ref, bf_ref, big_ref, o_ref = refs
        del big_ref
    else:
        vn_ref, cu_ref, w_ref, bf_ref, o_ref = refs
    r = lax.broadcasted_iota(jnp.int32, (rows, rows), 0)
    c = lax.broadcasted_iota(jnp.int32, (rows, rows), 1)
    causal = c <= r
    if sub is not None:
        causal = jnp.logical_and(causal, lax.shift_right_logical(r, sub) == lax.shift_right_logical(c, sub))
    lane = lax.broadcasted_iota(jnp.int32, (rows, LANES), 1)
    lo_half = lane < HEAD_DIM
    for pair in range(SEG // LANES):
        cs = slice(pair * LANES, (pair + 1) * LANES)
        vp = vn_ref[:, cs]
        acc = jnp.zeros((rows, LANES), F32)
        for hh in range(2):
            g = pair * 2 + hh
            if g >= N_GROUPS_C:
                continue
            keep = lo_half if hh == 0 else jnp.logical_not(lo_half)
            wm = jnp.where(causal, w_ref[g], 0.0).astype(BF16)
            vm = jnp.where(keep, vp, 0.0).astype(BF16)
            acc = acc + jnp.dot(wm, vm, preferred_element_type=F32)
        o_ref[:, cs] = cu_ref[:, cs] * (acc + bf_ref[:, cs])


def _gate_call(vn, cu, w, bfull, rows, first_block, n_blocks, sub=None, big=None):
    blk = pl.BlockSpec((rows, SEG), lambda i: (first_block + i, 0))
    args = [vn, cu, w, bfull]
    specs = [blk, blk, _full(w), _full(bfull)]
    kwargs = {}
    if big is not None:
        args.append(big)
        specs.append(pl.BlockSpec(memory_space=pl.ANY))
        kwargs["input_output_aliases"] = {4: 0}
    return pl.pallas_call(
        functools.partial(_gate_kernel, rows=rows, sub=sub, aliased=big is not None),
        grid=(n_blocks,), in_specs=specs, out_specs=blk,
        out_shape=jax.ShapeDtypeStruct(vn.shape, F32),
        compiler_params=_params(), name="spatial_gate", **kwargs)(*args)


def _mix_kernel(a_ref, o_ref, c_ref, x_ref, wa_ref, wb_ref, wc_ref, bo_ref, g1_ref, b1_ref,
                wr_ref, br_ref, h_ref, h3_ref, aux_ref, gate_ref, cnt_ref, carry):
    step = pl.program_id(0)

    @pl.when(step == 0)
    def _():
        carry[...] = jnp.zeros_like(carry)

    mix = (jnp.dot(a_ref[...].astype(BF16), wa_ref[...], preferred_element_type=F32)
           + jnp.dot(o_ref[...].astype(BF16), wb_ref[...], preferred_element_type=F32)
           + jnp.dot(c_ref[...].astype(BF16), wc_ref[...], preferred_element_type=F32)
           + bo_ref[...])
    h = _ln_full(DN_ALPHA * x_ref[...] + mix, g1_ref[...], b1_ref[...])
    h_ref[...] = h
    for s in range(ROW_TILES):
        h3_ref[pl.ds(s, TT, stride=SUBLANES), :] = h[:, s * LANES:(s + 1) * LANES]

    logits = jnp.dot(h, wr_ref[...], preferred_element_type=F32,
                     precision=lax.Precision.HIGHEST) + br_ref[...]
    lane = lax.broadcasted_iota(jnp.int32, (TT, LANES), 1)
    lane_f = lane.astype(F32)
    vals, idxs, sels = [], [], []
    cur = logits
    for _ in range(TOP_K):
        m = jnp.max(cur, axis=-1, keepdims=True)
        idx = jnp.min(jnp.where(cur == m, lane_f, float(LANES)), axis=-1, keepdims=True)
        sel = lane_f == idx
        vals.append(m)
        idxs.append(idx)
        sels.append(sel)
        cur = jnp.where(sel, -jnp.inf, cur)
    exps = [jnp.exp(v - vals[0]) for v in vals]
    den = exps[0] + exps[1] + exps[2] + exps[3]

    onehot = jnp.zeros((TT, LANES), F32)
    for sel in sels:
        onehot = onehot + sel.astype(F32)
    r = lax.broadcasted_iota(jnp.int32, (TT, TT), 0)
    c = lax.broadcasted_iota(jnp.int32, (TT, TT), 1)
    below = (c < r).astype(BF16)
    before = jnp.dot(below, onehot.astype(BF16), preferred_element_type=F32) + carry[...]
    aux = jnp.zeros((TT, LANES), F32)
    gates = jnp.zeros((TT, LANES), F32)
    for k in range(TOP_K):
        rank = jnp.sum(jnp.where(sels[k], before, 0.0), axis=-1, keepdims=True)
        aux = aux + jnp.where(lane == k, idxs[k], 0.0) + jnp.where(lane == TOP_K + k, rank, 0.0)
        gates = gates + jnp.where(lane == k, exps[k] / den, 0.0)
    aux_ref[...] = aux.astype(jnp.int32)
    gate_ref[...] = gates
    carry[...] = carry[...] + jnp.sum(onehot, axis=0, keepdims=True)
    cnt_ref[...] = carry[...].astype(jnp.int32)


def _mix_call(a_out, o_b, c_out, x, wa, wb, wc, bo, g1, b1, wr, br):
    n = x.shape[0]
    row = lambda w: pl.BlockSpec((TT, w), lambda i: (i, 0))
    h3 = pl.BlockSpec((TT * SUBLANES, LANES), lambda i: (i, 0))
    one = pl.BlockSpec((1, LANES), lambda i: (0, 0))
    return pl.pallas_call(
        _mix_kernel, grid=(n // TT,),
        in_specs=[row(SEG), row(SEG), row(SEG), row(D_MODEL), _full(wa), _full(wb), _full(wc),
                  _full(bo), _full(g1), _full(b1), _full(wr), _full(br)],
        out_specs=[row(D_MODEL), h3, row(LANES), row(LANES), one],
        out_shape=[jax.ShapeDtypeStruct((n, D_MODEL), F32),
                   jax.ShapeDtypeStruct((n * SUBLANES, LANES), F32),
                   jax.ShapeDtypeStruct((n, LANES), jnp.int32),
                   jax.ShapeDtypeStruct((n, LANES), F32),
                   jax.ShapeDtypeStruct((1, LANES), jnp.int32)],
        scratch_shapes=[pltpu.VMEM((1, LANES), F32)],
        compiler_params=_params(), name="mix_ln_router")(
            a_out, o_b, c_out, x, wa, wb, wc, bo, g1, b1, wr, br)


def _row_gather(idx_ref, n_rows, src_hbm, buf, slot, sem):
    base = slot * (n_rows * SUBLANES)

    def body(i, carry):
        t = idx_ref[0, 0, i]
        pltpu.make_async_copy(
            src_hbm.at[pl.ds(pl.multiple_of(t * SUBLANES, SUBLANES), SUBLANES)],
            buf.at[pl.ds(pl.multiple_of(base + i * SUBLANES, SUBLANES), SUBLANES)],
            sem.at[slot]).start()
        return carry

    lax.fori_loop(0, n_rows, body, 0, unroll=8)


def _row_gather_wait(n_rows, src_hbm, buf, slot, sem):
    base = pl.multiple_of(slot * (n_rows * SUBLANES), SUBLANES)
    pltpu.make_async_copy(src_hbm.at[pl.ds(0, n_rows * SUBLANES)],
                          buf.at[pl.ds(base, n_rows * SUBLANES)], sem.at[slot]).wait()


def _expert_kernel(blk_e_ref, n_used_ref, tok_ref, tok_next_ref, h3_hbm, w1_ref, b1_ref,
                   w2_ref, b2_ref, y3_ref, buf, xb, sem):
    del blk_e_ref
    b = pl.program_id(0)
    n_used = n_used_ref[0]
    slot = lax.rem(b, 2)

    @pl.when(b == 0)
    def _():
        _row_gather(tok_ref, TM, h3_hbm, buf, 0, sem)

    @pl.when(b < n_used)
    def _():
        _row_gather_wait(TM, h3_hbm, buf, slot, sem)

    @pl.when(b + 1 < n_used)
    def _():
        _row_gather(tok_next_ref, TM, h3_hbm, buf, 1 - slot, sem)

    @pl.when(b < n_used)
    def _():
        base = slot * (TM * SUBLANES)
        for s in range(ROW_TILES):
            xb[:, s * LANES:(s + 1) * LANES] = buf[pl.ds(base + s, TM, stride=SUBLANES), :].astype(BF16)
        gu = jnp.dot(xb[...], w1_ref[0], preferred_element_type=F32) + b1_ref[0]
        gate = jnp.minimum(gu[:, :D_FF], SWIGLU_LIMIT)
        up = jnp.clip(gu[:, D_FF:], -SWIGLU_LIMIT, SWIGLU_LIMIT)
        hh = (up + 1.0) * (gate * jax.nn.sigmoid(SWIGLU_ALPHA * gate))
        y = jnp.dot(hh.astype(BF16), w2_ref[0], preferred_element_type=F32) + b2_ref[0]
        for s in range(ROW_TILES):
            y3_ref[pl.ds(s, TM, stride=SUBLANES), :] = y[:, s * LANES:(s + 1) * LANES]

    @pl.when(b >= n_used)
    def _():
        y3_ref[...] = jnp.zeros_like(y3_ref)


def _expert_call(blk_e, n_used, slot_tok, h3, w1b, b1, w2b, b2):
    n_blocks = blk_e.shape[0]
    tok = slot_tok.reshape(n_blocks, 1, TM)
    smem = lambda f: pl.BlockSpec((1, 1, TM), f, memory_space=pltpu.SMEM)
    grid_spec = pltpu.PrefetchScalarGridSpec(
        num_scalar_prefetch=2, grid=(n_blocks,),
        in_specs=[smem(lambda b, e, u: (b, 0, 0)),
                  smem(lambda b, e, u: (jnp.minimum(b + 1, n_blocks - 1), 0, 0)),
                  pl.BlockSpec(memory_space=pl.ANY),
                  pl.BlockSpec((1, D_MODEL, 2 * D_FF), lambda b, e, u: (e[b], 0, 0)),
                  pl.BlockSpec((1, 1, 2 * D_FF), lambda b, e, u: (e[b], 0, 0)),
                  pl.BlockSpec((1, D_FF, D_MODEL), lambda b, e, u: (e[b], 0, 0)),
                  pl.BlockSpec((1, 1, D_MODEL), lambda b, e, u: (e[b], 0, 0))],
        out_specs=pl.BlockSpec((TM * SUBLANES, LANES), lambda b, e, u: (b, 0)),
        scratch_shapes=[pltpu.VMEM((2 * TM * SUBLANES, LANES), F32),
                        pltpu.VMEM((TM, D_MODEL), BF16),
                        pltpu.SemaphoreType.DMA((2,))])
    return pl.pallas_call(
        _expert_kernel, grid_spec=grid_spec,
        out_shape=jax.ShapeDtypeStruct((n_blocks * TM * SUBLANES, LANES), F32),
        compiler_params=_params(), name="expert_ffn")(
            blk_e, n_used, tok, tok, h3, w1b, b1, w2b, b2)


def _combine_kernel(pos_ref, pos_next_ref, y3_hbm, h_ref, gate_ref, g2_ref, b2_ref, o_ref, buf, sem):
    i = pl.program_id(0)
    n_steps = pl.num_programs(0)
    slot = lax.rem(i, 2)
    n_rows = TOP_K * TC

    @pl.when(i == 0)
    def _():
        _row_gather(pos_ref, n_rows, y3_hbm, buf, 0, sem)

    _row_gather_wait(n_rows, y3_hbm, buf, slot, sem)

    @pl.when(i + 1 < n_steps)
    def _():
        _row_gather(pos_next_ref, n_rows, y3_hbm, buf, 1 - slot, sem)

    base = slot * (n_rows * SUBLANES)
    g = gate_ref[...]
    cols = []
    for s in range(ROW_TILES):
        acc = jnp.zeros((TC, LANES), F32)
        for k in range(TOP_K):
            piece = buf[pl.ds(base + k * TC * SUBLANES + s, TC, stride=SUBLANES), :]
            acc = acc + g[:, k:k + 1] * piece
        cols.append(acc)
    moe = jnp.concatenate(cols, axis=1)
    o_ref[...] = _ln_full(DN_ALPHA * h_ref[...] + moe, g2_ref[...], b2_ref[...])


def _combine_call(pos_t, y3, h, gates, g2, b2):
    n = h.shape[0]
    n_steps = n // TC
    n_rows = TOP_K * TC
    smem = lambda f: pl.BlockSpec((1, 1, n_rows), f, memory_space=pltpu.SMEM)
    row = lambda w: pl.BlockSpec((TC, w), lambda i: (i, 0))
    return pl.pallas_call(
        _combine_kernel, grid=(n_steps,),
        in_specs=[smem(lambda i: (i, 0, 0)),
                  smem(lambda i: (jnp.minimum(i + 1, n_steps - 1), 0, 0)),
                  pl.BlockSpec(memory_space=pl.ANY),
                  row(D_MODEL), row(LANES), _full(g2), _full(b2)],
        out_specs=row(D_MODEL), out_shape=jax.ShapeDtypeStruct((n, D_MODEL), F32),
        scratch_shapes=[pltpu.VMEM((2 * n_rows * SUBLANES, LANES), F32),
                        pltpu.SemaphoreType.DMA((2,))],
        compiler_params=_params(), name="combine_ln")(pos_t, pos_t, y3, h, gates, g2, b2)


def _routing_tables(aux, counts, n):
    top_e = aux[:, :TOP_K]
    rank = aux[:, TOP_K:2 * TOP_K]
    counts = counts[0, :N_EXPERTS]
    padded = (counts + TM - 1) // TM * TM
    pends = jnp.cumsum(padded)
    pstarts = pends - padded
    pos = pstarts[top_e] + rank
    n_blocks = n * TOP_K // TM + N_EXPERTS
    tok_ids = jnp.broadcast_to(jnp.arange(n, dtype=jnp.int32)[:, None], (n, TOP_K))
    slot_tok = jnp.zeros((n_blocks * TM,), jnp.int32).at[pos.reshape(-1)].set(tok_ids.reshape(-1))
    blk_start = jnp.arange(n_blocks, dtype=jnp.int32) * TM
    blk_e = jnp.minimum(jnp.sum((pends[None, :] <= blk_start[:, None]).astype(jnp.int32), axis=1),
                        N_EXPERTS - 1)
    n_used = (pends[-1:] // TM).astype(jnp.int32)
    pos_t = pos.reshape(n // TC, TC, TOP_K).transpose(0, 2, 1).reshape(n // TC, 1, TOP_K * TC)
    return blk_e, n_used, slot_tok, pos_t.astype(jnp.int32)


def _pad_cols(a, width):
    return jnp.pad(a, [(0, 0)] * (a.ndim - 1) + [(0, width - a.shape[-1])])


def _layer_params(l, w_in, b_in, w_dw, b_dw, ln_a_g, ln_a_b, ln_c_g, ln_c_b, w_sp, b_sp,
                  w_out, b_out, ln1_g, ln1_b, w_router, b_router, w1, b1, w2, b2, ln2_g, ln2_b):
    bounds = [0, C_B, 2 * C_B, 3 * C_B, 3 * C_B + C_A, 3 * C_B + 2 * C_A, 3 * C_B + 2 * C_A + C_C,
              3 * C_B + 2 * C_A + 2 * C_C]
    w_segs = [_pad_cols(w_in[l][:, bounds[i]:bounds[i + 1]], SEG) for i in range(N_SEG)]
    b_segs = [_pad_cols(b_in[l][bounds[i]:bounds[i + 1]], SEG) for i in range(N_SEG)]
    row = lambda a: _pad_cols(a, SEG)[None, :]
    wo = w_out[l]
    pad_rows = lambda a: jnp.pad(a, ((0, SEG - a.shape[0]), (0, 0)))
    bsp = b_sp[l]
    bfull = _pad_cols(jnp.repeat(bsp.T, HEAD_DIM, axis=1), SEG)
    return dict(
        w_in=jnp.concatenate(w_segs, axis=1).astype(BF16),
        b_in=jnp.concatenate(b_segs)[None, :],
        w_dw=jnp.pad(w_dw[l], ((0, HALO - CONV_W), (0, SEG - C_A))),
        b_dw=row(b_dw[l]), ln_a_g=row(ln_a_g[l]), ln_a_b=row(ln_a_b[l]),
        ln_c_g=row(ln_c_g[l]), ln_c_b=row(ln_c_b[l]),
        w_sp=w_sp[l], b_full=bfull,
        w_sp_s=jnp.tile(w_sp[l][:, :DEC_SEQ, :DEC_SEQ], (1, TT // DEC_SEQ, TT // DEC_SEQ)),
        b_full_s=jnp.tile(bfull[:DEC_SEQ], (TT // DEC_SEQ, 1)),
        wa=pad_rows(wo[:C_A]).astype(BF16), wb=wo[C_A:C_A + C_B].astype(BF16),
        wc=pad_rows(wo[C_A + C_B:]).astype(BF16), b_out=b_out[l][None, :],
        ln1_g=ln1_g[l][None, :], ln1_b=ln1_b[l][None, :],
        w_router=_pad_cols(w_router[l], LANES),
        b_router=jnp.concatenate([b_router[l], jnp.full((LANES - N_EXPERTS,), NEG, F32)])[None, :],
        w1=w1[l].astype(BF16), b1=b1[l][:, None, :], w2=w2[l].astype(BF16), b2=b2[l][:, None, :],
        ln2_g=ln2_g[l][None, :], ln2_b=ln2_b[l][None, :])


def _layer(x, p, hist_p, cache_k, cache_v, biases, tp):
    n = x.shape[0]
    q, k, v, a, cu, vn = _proj_call(x, p["w_in"], p["b_in"], p["ln_c_g"], p["ln_c_b"])

    conv_args = (p["w_dw"], p["b_dw"], p["ln_a_g"], p["ln_a_b"])
    a_out = _conv_prompt_call(a, *conv_args, tp)
    a_out, new_a_s = _conv_sample_call(hist_p, a, *conv_args, a_out, tp)

    merged = None
    for dil in DILATIONS:
        merged = _attn_prompt_call(q, k, v, biases["prompt"][dil], merged, dil, tp)
    o_b = _attn_sample_call(q, k, v, cache_k, cache_v, *biases["sample"], merged[0], tp)

    c_out = _gate_call(vn, cu, p["w_sp"], p["b_full"], CHUNK, 0, n // CHUNK)
    c_out = _gate_call(vn, cu, p["w_sp_s"], p["b_full_s"], TT, tp // TT, 1, sub=3, big=c_out)

    h, h3, aux, gates, counts = _mix_call(a_out, o_b, c_out, x, p["wa"], p["wb"], p["wc"], p["b_out"],
                                          p["ln1_g"], p["ln1_b"], p["w_router"], p["b_router"])
    blk_e, n_used, slot_tok, pos_t = _routing_tables(aux, counts, n)
    y3 = _expert_call(blk_e, n_used, slot_tok, h3, p["w1"], p["b1"], p["w2"], p["b2"])
    y = _combine_call(pos_t, y3, h, gates, p["ln2_g"], p["ln2_b"])
    return y, a, k, v, vn, new_a_s


def kernel(x_prompt, x_sample, state_a_conv, cache_b_k, cache_b_v, w_in, b_in, w_dw, b_dw, ln_a_g, ln_a_b, ln_c_g, ln_c_b, w_sp, b_sp, w_out, b_out, ln1_g, ln1_b, w_router, b_router, w1, b1, w2, b2, ln2_g, ln2_b):
    bp, tp, _ = x_prompt.shape
    bs, ts, _ = x_sample.shape
    n_cache = cache_b_k.shape[2]
    assert bp == 1 and ts == DEC_SEQ and bs * ts == TT and tp % WIN_MAX == 0
    assert n_cache == WIN_MAX
    n = tp + bs * ts
    keep = min(WIN_MAX, tp)
    hist = CONV_W - 1

    weights = (w_in, b_in, w_dw, b_dw, ln_a_g, ln_a_b, ln_c_g, ln_c_b, w_sp, b_sp, w_out, b_out,
               ln1_g, ln1_b, w_router, b_router, w1, b1, w2, b2, ln2_g, ln2_b)
    biases = {"prompt": {d: _prompt_bias(d) for d in DILATIONS}, "sample": _sample_bias(n_cache)}

    x = jnp.concatenate([x_prompt[0], x_sample.reshape(bs * ts, D_MODEL)], axis=0)
    outs = {name: [] for name in ("a_p", "a_s", "k_p", "v_p", "k_s", "v_s", "c_s")}
    for l in range(DEPTH):
        p = _layer_params(l, *weights)
        hist_p = jnp.pad(state_a_conv[l], ((0, 0), (HALO - hist, 0), (0, SEG - C_A)))
        ck = cache_b_k[l].reshape(bs, n_cache, C_B)
        cv = cache_b_v[l].reshape(bs, n_cache, C_B)
        x, a, k, v, vn, new_a_s = _layer(x, p, hist_p, ck, cv, biases, tp)
        outs["a_p"].append(a[tp - hist:tp, :C_A][None])
        outs["a_s"].append(new_a_s[:, HALO - hist:, :C_A])
        outs["k_p"].append(k[tp - keep:tp].reshape(1, keep, N_HEADS_B, HEAD_DIM))
        outs["v_p"].append(v[tp - keep:tp].reshape(1, keep, N_HEADS_B, HEAD_DIM))
        outs["k_s"].append(k[tp:].reshape(bs, ts, N_HEADS_B, HEAD_DIM))
        outs["v_s"].append(v[tp:].reshape(bs, ts, N_HEADS_B, HEAD_DIM))
        outs["c_s"].append(vn[tp:, :C_C].reshape(bs, ts, C_C))
    stack = lambda name: jnp.stack(outs[name])
    return (x[:tp][None], x[tp:].reshape(bs, ts, D_MODEL), stack("a_p"), stack("a_s"),
            stack("k_p"), stack("v_p"), stack("k_s"), stack("v_s"), stack("c_s"))
```

```python
import functools
import math

import jax
import jax.numpy as jnp
from jax import lax
from jax.experimental import pallas as pl
from jax.experimental.pallas import tpu as pltpu

F32 = jnp.float32
BF16 = jnp.bfloat16

D_MODEL = 1024
HEAD_DIM = 64
N_HEADS_B = 6
C_B = N_HEADS_B * HEAD_DIM
C_A = 320
C_C = 320
N_GROUPS_C = 5
CONV_W = 31
DILATIONS = (1, 4, 16)
BAND = 128
WIN_MAX = 2048
CHUNK = 128
N_EXPERTS = 32
TOP_K = 4
D_FF = 1024
SWIGLU_LIMIT = 7.0
SWIGLU_ALPHA = 1.702
LN_EPS = 1e-5
DEPTH = 2
DN_ALPHA = (2.0 * DEPTH) ** 0.25
DEC_SEQ = 8

LANES = 128
SUBLANES = 8
SEG = 384
N_SEG = 7
ROW_TILES = D_MODEL // LANES
assert ROW_TILES == SUBLANES

TT = 256
TM = 256
NEG = -1e30
VMEM_LIMIT = 48 * 1024 * 1024


def _params(n_axes=1):
    return pltpu.CompilerParams(dimension_semantics=("arbitrary",) * n_axes,
                                vmem_limit_bytes=VMEM_LIMIT)


def _full(a):
    nd = a.ndim
    return pl.BlockSpec(a.shape, lambda *_: (0,) * nd)


def _ln_valid(x, g, b, n_valid):
    col = lax.broadcasted_iota(jnp.int32, x.shape, 1)
    ok = col < n_valid
    mu = jnp.sum(jnp.where(ok, x, 0.0), axis=-1, keepdims=True) / n_valid
    xc = jnp.where(ok, x - mu, 0.0)
    var = jnp.sum(xc * xc, axis=-1, keepdims=True) / n_valid
    return xc * lax.rsqrt(var + LN_EPS) * g + b


def _ln_full(x, g, b):
    mu = jnp.mean(x, axis=-1, keepdims=True)
    xc = x - mu
    var = jnp.mean(xc * xc, axis=-1, keepdims=True)
    return xc * lax.rsqrt(var + LN_EPS) * g + b


def _proj_kernel(x_ref, w_ref, b_ref, lcg_ref, lcb_ref,
                 q_ref, k_ref, v_ref, a_ref, cu_ref, vn_ref):
    xb = x_ref[...].astype(BF16)

    def seg(i):
        lo = i * SEG
        return (jnp.dot(xb, w_ref[:, lo:lo + SEG], preferred_element_type=F32)
                + b_ref[:, lo:lo + SEG])

    q_ref[...] = seg(0)
    k_ref[...] = seg(1)
    v_ref[...] = seg(2)
    a_ref[...] = seg(3) * jax.nn.sigmoid(seg(4))
    cu_ref[...] = seg(5)
    vn_ref[...] = _ln_valid(seg(6), lcg_ref[...], lcb_ref[...], C_C)


def _proj_call(x, w_in_p, b_in_p, lcg, lcb):
    n = x.shape[0]
    row = lambda w: pl.BlockSpec((TT, w), lambda i: (i, 0))
    out = jax.ShapeDtypeStruct((n, SEG), F32)
    return pl.pallas_call(
        _proj_kernel, grid=(n // TT,),
        in_specs=[row(D_MODEL), _full(w_in_p), _full(b_in_p), _full(lcg), _full(lcb)],
        out_specs=[row(SEG)] * 6, out_shape=[out] * 6,
        compiler_params=_params(), name="in_proj")(x, w_in_p, b_in_p, lcg, lcb)


HALO = 32
CONV_CHUNK = 32


def _conv_tail(acc, bdw_ref, g_ref, b_ref):
    return jax.nn.silu(_ln_valid(acc + bdw_ref[...], g_ref[...], b_ref[...], C_A))


def _conv_prompt_kernel(halo_ref, a_ref, w_ref, bdw_ref, g_ref, b_ref, o_ref, buf, *, n_prompt):
    i = pl.program_id(0)

    @pl.when(i < n_prompt)
    def _():
        buf[0:HALO, :] = jnp.where(i > 0, halo_ref[...], 0.0)
        buf[HALO:, :] = a_ref[...]
        base = HALO - (CONV_W - 1)
        for r in range(0, TT, CONV_CHUNK):
            acc = jnp.zeros((CONV_CHUNK, SEG), F32)
            for j in range(CONV_W):
                acc = acc + w_ref[j:j + 1, :] * buf[base + r + j: base + r + j + CONV_CHUNK, :]
            o_ref[r:r + CONV_CHUNK, :] = _conv_tail(acc, bdw_ref, g_ref, b_ref)

    @pl.when(i >= n_prompt)
    def _():
        o_ref[...] = jnp.zeros_like(o_ref)


def _conv_prompt_call(a, w_dw_p, bdw, g, b, tp):
    n = a.shape[0]
    per = TT // HALO
    halo = pl.BlockSpec((HALO, SEG), lambda i: (jnp.maximum(i * per - 1, 0), 0))
    row = pl.BlockSpec((TT, SEG), lambda i: (i, 0))
    return pl.pallas_call(
        functools.partial(_conv_prompt_kernel, n_prompt=tp // TT), grid=(n // TT,),
        in_specs=[halo, row, _full(w_dw_p), _full(bdw), _full(g), _full(b)],
        out_specs=row, out_shape=jax.ShapeDtypeStruct((n, SEG), F32),
        scratch_shapes=[pltpu.VMEM((HALO + TT, SEG), F32)],
        compiler_params=_params(), name="conv_prompt")(a, a, w_dw_p, bdw, g, b)


SB = 8


def _conv_sample_kernel(hist_ref, a_ref, w_ref, bdw_ref, g_ref, b_ref, big_ref,
                        o_ref, newa_ref, buf):
    del big_ref
    bs = hist_ref.shape[0]
    buf[:, 0:HALO, :] = hist_ref[...]
    buf[:, HALO:, :] = a_ref[...].reshape(bs, DEC_SEQ, SEG)
    base = HALO - (CONV_W - 1)
    for s in range(0, bs, SB):
        acc = jnp.zeros((SB, DEC_SEQ, SEG), F32)
        for j in range(CONV_W):
            acc = acc + w_ref[j:j + 1, :][None] * buf[s:s + SB, base + j: base + j + DEC_SEQ, :]
        y = _conv_tail(acc.reshape(SB * DEC_SEQ, SEG), bdw_ref, g_ref, b_ref)
        o_ref[s * DEC_SEQ:(s + SB) * DEC_SEQ, :] = y
    newa_ref[...] = buf[:, DEC_SEQ:, :]


def _conv_sample_call(hist_p, a, w_dw_p, bdw, g, b, a_out, tp):
    bs = hist_p.shape[0]
    rows = bs * DEC_SEQ
    blk = pl.BlockSpec((rows, SEG), lambda i: (tp // rows, 0))
    return pl.pallas_call(
        _conv_sample_kernel, grid=(1,),
        in_specs=[_full(hist_p), blk, _full(w_dw_p), _full(bdw), _full(g), _full(b),
                  pl.BlockSpec(memory_space=pl.ANY)],
        out_specs=[blk, pl.BlockSpec((bs, HALO, SEG), lambda i: (0, 0, 0))],
        out_shape=[jax.ShapeDtypeStruct(a_out.shape, F32),
                   jax.ShapeDtypeStruct((bs, HALO, SEG), F32)],
        scratch_shapes=[pltpu.VMEM((bs, HALO + DEC_SEQ, SEG), F32)],
        input_output_aliases={6: 0},
        compiler_params=_params(), name="conv_sample")(hist_p, a, w_dw_p, bdw, g, b, a_out)


QB = 128
assert QB == BAND


def _attn_prompt_kernel(*refs, merge, n_prompt):
    o_ref, l_ref = refs[-2:]
    is_prompt = pl.program_id(1) < n_prompt
    pl.when(is_prompt)(functools.partial(_attn_prompt_block, refs, merge))

    @pl.when(jnp.logical_not(is_prompt))
    def _():
        o_ref[...] = jnp.zeros_like(o_ref)
        l_ref[...] = jnp.zeros_like(l_ref)


def _attn_prompt_block(refs, merge):
    if merge:
        (q_ref, kp_ref, kc_ref, vp_ref, vc_ref, bias_ref, op_ref, lp_ref, o_ref, l_ref) = refs
    else:
        (q_ref, kp_ref, kc_ref, vp_ref, vc_ref, bias_ref, o_ref, l_ref) = refs
    first = pl.program_id(1) == 0
    kcol = lax.broadcasted_iota(jnp.int32, (QB, 2 * QB), 1)
    no_prev = jnp.logical_and(first, kcol < QB)
    lane = lax.broadcasted_iota(jnp.int32, (QB, LANES), 1)
    lo_half = lane < HEAD_DIM
    for pair in range(N_HEADS_B // 2):
        cs = slice(pair * LANES, (pair + 1) * LANES)
        qp = q_ref[:, cs] * (HEAD_DIM ** -0.5)
        kk = jnp.concatenate([kp_ref[:, cs], kc_ref[:, cs]], axis=0).astype(BF16)
        vv = jnp.concatenate([vp_ref[:, cs], vc_ref[:, cs]], axis=0).astype(BF16)
        outs, lses = [], []
        for hh in range(2):
            keep = lo_half if hh == 0 else jnp.logical_not(lo_half)
            qm = jnp.where(keep, qp, 0.0).astype(BF16)
            s = lax.dot_general(qm, kk, (((1,), (1,)), ((), ())), preferred_element_type=F32)
            s = s + bias_ref[pair * 2 + hh]
            s = jnp.where(no_prev, NEG, s)
            m = jnp.max(s, axis=-1, keepdims=True)
            p = jnp.exp(s - m)
            l = jnp.sum(p, axis=-1, keepdims=True)
            pv = jnp.dot(p.astype(BF16), vv, preferred_element_type=F32)
            outs.append(pv / l)
            lses.append(m + jnp.log(l))
        o_new = jnp.where(lo_half, outs[0], outs[1])
        l_new = jnp.where(lo_half, lses[0], lses[1])
        if merge:
            l_old = lp_ref[:, cs]
            mx = jnp.maximum(l_old, l_new)
            w_old = jnp.exp(l_old - mx)
            w_new = jnp.exp(l_new - mx)
            tot = w_old + w_new
            o_new = (w_old * op_ref[:, cs] + w_new * o_new) / tot
            l_new = mx + jnp.log(tot)
        o_ref[:, cs] = o_new
        l_ref[:, cs] = l_new


def _attn_prompt_call(q, k, v, bias, prev, dil, tp):
    n = q.shape[0]
    view = lambda a: a.reshape(n // dil, dil * SEG)
    cur = pl.BlockSpec((QB, SEG), lambda r, m: (m, r))
    prv = pl.BlockSpec((QB, SEG), lambda r, m: (jnp.maximum(m - 1, 0), r))
    args = [view(q), view(k), view(k), view(v), view(v), bias]
    specs = [cur, prv, cur, prv, cur, _full(bias)]
    if prev is not None:
        args += [view(prev[0]), view(prev[1])]
        specs += [cur, cur]
    shape = jax.ShapeDtypeStruct((n // dil, dil * SEG), F32)
    o, l = pl.pallas_call(
        functools.partial(_attn_prompt_kernel, merge=prev is not None, n_prompt=tp // dil // QB),
        grid=(dil, pl.cdiv(n // dil, QB)), in_specs=specs, out_specs=[cur, cur],
        out_shape=[shape, shape], compiler_params=_params(2),
        name=f"attn_prompt_d{dil}")(*args)
    return o.reshape(n, SEG), l.reshape(n, SEG)


def _alibi_slopes():
    return [2.0 ** (-8.0 * (h + 1) / N_HEADS_B) for h in range(N_HEADS_B)]


def _prompt_bias(dil):
    i = jnp.arange(QB)[:, None]
    j = jnp.arange(2 * QB)[None, :]
    rel = i + QB - j
    ok = jnp.logical_and(rel >= 0, rel <= BAND)
    slopes = jnp.asarray(_alibi_slopes(), F32)[:, None, None]
    pen = -slopes * (rel * dil).astype(F32)[None]
    return jnp.where(ok[None], pen, NEG).astype(F32)


def _branch_multiplicity(dist):
    cnt = jnp.zeros(dist.shape, jnp.int32)
    for dil in DILATIONS:
        cnt = cnt + jnp.logical_and(dist % dil == 0, dist <= BAND * dil).astype(jnp.int32)
    return cnt


def _sample_bias(n_cache):
    t = jnp.arange(DEC_SEQ)[:, None]
    dist_c = n_cache + t - jnp.arange(n_cache)[None, :]
    dist_n = t - jnp.arange(LANES)[None, :]
    ok_n = jnp.logical_and(dist_n >= 0, jnp.arange(LANES)[None, :] < DEC_SEQ)
    slopes = jnp.asarray(_alibi_slopes(), F32)[:, None, None]

    def bias(dist, ok):
        mult = _branch_multiplicity(jnp.maximum(dist, 0))
        ok = jnp.logical_and(ok, mult > 0)
        val = -slopes * dist.astype(F32)[None] + jnp.log(jnp.maximum(mult, 1).astype(F32))[None]
        return jnp.where(ok[None], val, NEG).astype(F32)

    bc = bias(dist_c, jnp.ones(dist_c.shape, bool)).reshape(N_HEADS_B * DEC_SEQ, n_cache)
    bn = bias(dist_n, ok_n).reshape(N_HEADS_B * DEC_SEQ, LANES)
    return bc, bn


def _attn_sample_kernel(q_ref, kn_ref, vn_ref, kc_ref, vc_ref, bc_ref, bn_ref, big_ref, o_ref):
    del big_ref
    rows = N_HEADS_B * DEC_SEQ
    q = q_ref[0] * (HEAD_DIM ** -0.5)
    qrep = jnp.concatenate([q] * N_HEADS_B, axis=0)
    rh = lax.shift_right_logical(lax.broadcasted_iota(jnp.int32, (rows, SEG), 0), 3)
    ch = lax.shift_right_logical(lax.broadcasted_iota(jnp.int32, (rows, SEG), 1), 6)
    own = rh == ch
    qm = jnp.where(own, qrep, 0.0).astype(BF16)
    pad = jnp.zeros((LANES - DEC_SEQ, SEG), F32)
    kn = jnp.concatenate([kn_ref[0], pad], axis=0).astype(BF16)
    vn = jnp.concatenate([vn_ref[0], pad], axis=0).astype(BF16)
    nt = (((1,), (1,)), ((), ()))
    s_c = lax.dot_general(qm, kc_ref[0].astype(BF16), nt, preferred_element_type=F32) + bc_ref[...]
    s_n = lax.dot_general(qm, kn, nt, preferred_element_type=F32) + bn_ref[...]
    m = jnp.maximum(jnp.max(s_c, axis=-1, keepdims=True), jnp.max(s_n, axis=-1, keepdims=True))
    p_c = jnp.exp(s_c - m)
    p_n = jnp.exp(s_n - m)
    l = jnp.sum(p_c, axis=-1, keepdims=True) + jnp.sum(p_n, axis=-1, keepdims=True)
    r = (jnp.dot(p_c.astype(BF16), vc_ref[0].astype(BF16), preferred_element_type=F32)
         + jnp.dot(p_n.astype(BF16), vn, preferred_element_type=F32)) / l
    r = jnp.where(own, r, 0.0)
    o = r[0:DEC_SEQ]
    for h in range(1, N_HEADS_B):
        o = o + r[h * DEC_SEQ:(h + 1) * DEC_SEQ]
    o_ref[...] = o


def _attn_sample_call(q, k, v, cache_k, cache_v, bc, bn, o_big, tp):
    n = q.shape[0]
    bs, n_cache = cache_k.shape[0], cache_k.shape[1]
    v3 = lambda a: a.reshape(n // DEC_SEQ, DEC_SEQ, SEG)
    new = pl.BlockSpec((1, DEC_SEQ, SEG), lambda b: (tp // DEC_SEQ + b, 0, 0))
    cache = pl.BlockSpec((1, n_cache, SEG), lambda b: (b, 0, 0))
    return pl.pallas_call(
        _attn_sample_kernel, grid=(bs,),
        in_specs=[new, new, new, cache, cache, _full(bc), _full(bn),
                  pl.BlockSpec(memory_space=pl.ANY)],
        out_specs=pl.BlockSpec((DEC_SEQ, SEG), lambda b: (tp // DEC_SEQ + b, 0)),
        out_shape=jax.ShapeDtypeStruct(o_big.shape, F32),
        input_output_aliases={7: 0},
        compiler_params=_params(), name="attn_sample")(
            v3(q), v3(k), v3(v), cache_k, cache_v, bc, bn, o_big)


def _gate_kernel(*refs, rows, sub, aliased):
    if aliased:
        vn_ref, cu_ref, w_ref, bf_ref, big_ref, o_ref = refs
        del big_ref
    else:
        vn_ref, cu_ref, w_ref, bf_ref, o_ref = refs
    r = lax.broadcasted_iota(jnp.int32, (rows, rows), 0)
    c = lax.broadcasted_iota(jnp.int32, (rows, rows), 1)
    causal = c <= r
    if sub is not None:
        causal = jnp.logical_and(causal, lax.shift_right_logical(r, sub) == lax.shift_right_logical(c, sub))
    lane = lax.broadcasted_iota(jnp.int32, (rows, LANES), 1)
    lo_half = lane < HEAD_DIM
    for pair in range(SEG // LANES):
        cs = slice(pair * LANES, (pair + 1) * LANES)
        vp = vn_ref[:, cs]
        acc = jnp.zeros((rows, LANES), F32)
        for hh in range(2):
            g = pair * 2 + hh
            if g >= N_GROUPS_C:
                continue
            keep = lo_half if hh == 0 else jnp.logical_not(lo_half)
            wm = jnp.where(causal, w_ref[g], 0.0).astype(BF16)
            vm = jnp.where(keep, vp, 0.0).astype(BF16)
            acc = acc + jnp.dot(wm, vm, preferred_element_type=F32)
        o_ref[:, cs] = cu_ref[:, cs] * (acc + bf_ref[:, cs])


def _gate_call(vn, cu, w, bfull, rows, first_block, n_blocks, sub=None, big=None):
    blk = pl.BlockSpec((rows, SEG), lambda i: (first_block + i, 0))
    args = [vn, cu, w, bfull]
    specs = [blk, blk, _full(w), _full(bfull)]
    kwargs = {}
    if big is not None:
        args.append(big)
        specs.append(pl.BlockSpec(memory_space=pl.ANY))
        kwargs["input_output_aliases"] = {4: 0}
    return pl.pallas_call(
        functools.partial(_gate_kernel, rows=rows, sub=sub, aliased=big is not None),
        grid=(n_blocks,), in_specs=specs, out_specs=blk,
        out_shape=jax.ShapeDtypeStruct(vn.shape, F32),
        compiler_params=_params(), name="spatial_gate", **kwargs)(*args)


def _mix_kernel(a_ref, o_ref, c_ref, x_ref, wa_ref, wb_ref, wc_ref, bo_ref, g1_ref, b1_ref,
                wrh_ref, wrl_ref, br_ref, h_ref, xs3_ref, aux_ref, cnt_ref):
    mix = (jnp.dot(a_ref[...].astype(BF16), wa_ref[...], preferred_element_type=F32)
           + jnp.dot(o_ref[...].astype(BF16), wb_ref[...], preferred_element_type=F32)
           + jnp.dot(c_ref[...].astype(BF16), wc_ref[...], preferred_element_type=F32)
           + bo_ref[...])
    h = _ln_full(DN_ALPHA * x_ref[...] + mix, g1_ref[...], b1_ref[...])
    h_ref[...] = h

    hb = h.astype(BF16)
    h_lo = (h - hb.astype(F32)).astype(BF16)
    logits = (jnp.dot(hb, wrh_ref[...], preferred_element_type=F32)
              + jnp.dot(h_lo, wrh_ref[...], preferred_element_type=F32)
              + jnp.dot(hb, wrl_ref[...], preferred_element_type=F32) + br_ref[...])
    lane = lax.broadcasted_iota(jnp.int32, (TT, LANES), 1)
    lane_f = lane.astype(F32)
    vals, idxs, sels = [], [], []
    cur = logits
    for _ in range(TOP_K):
        m = jnp.max(cur, axis=-1, keepdims=True)
        idx = jnp.min(jnp.where(cur == m, lane_f, float(LANES)), axis=-1, keepdims=True)
        sel = lane_f == idx
        vals.append(m)
        idxs.append(idx)
        sels.append(sel)
        cur = jnp.where(sel, -jnp.inf, cur)
    exps = [jnp.exp(v - vals[0]) for v in vals]
    den = exps[0] + exps[1] + exps[2] + exps[3]

    onehot = jnp.zeros((TT, LANES), F32)
    for sel in sels:
        onehot = onehot + sel.astype(F32)
    r = lax.broadcasted_iota(jnp.int32, (TT, TT), 0)
    c = lax.broadcasted_iota(jnp.int32, (TT, TT), 1)
    below = (c < r).astype(BF16)
    earlier = jnp.dot(below, onehot.astype(BF16), preferred_element_type=F32)
    cnt = jnp.broadcast_to(jnp.sum(onehot, axis=0, keepdims=True), (SUBLANES, LANES))
    er = lax.broadcasted_iota(jnp.int32, (LANES, LANES), 0)
    ec = lax.broadcasted_iota(jnp.int32, (LANES, LANES), 1)
    off = jnp.dot(cnt.astype(BF16), (er < ec).astype(BF16), preferred_element_type=F32)[0:1]
    place = earlier + off
    aux = jnp.zeros((TT, LANES), F32)
    rows = []
    for k in range(TOP_K):
        row = jnp.sum(jnp.where(sels[k], place, 0.0), axis=-1, keepdims=True)
        rows.append(row)
        aux = aux + jnp.where(lane == k, row, 0.0) + jnp.where(lane == TOP_K + k, exps[k] / den, 0.0)
    aux_ref[...] = aux
    cnt_ref[0] = cnt.astype(jnp.int32)

    dest = lax.broadcasted_iota(jnp.int32, (TT, TOP_K * TT), 1).astype(F32)
    disp = jnp.zeros((TT, TOP_K * TT), F32)
    for k in range(TOP_K):
        disp = disp + (dest == rows[k]).astype(F32)
    xs = lax.dot_general(disp.astype(BF16), hb, (((0,), (0,)), ((), ())), preferred_element_type=F32)
    for s in range(ROW_TILES):
        xs3_ref[pl.ds(s, TOP_K * TT, stride=SUBLANES), :] = xs[:, s * LANES:(s + 1) * LANES]


def _mix_call(a_out, o_b, c_out, x, wa, wb, wc, bo, g1, b1, wrh, wrl, br):
    n = x.shape[0]
    row = lambda w: pl.BlockSpec((TT, w), lambda i: (i, 0))
    xs3 = pl.BlockSpec((TOP_K * TT * SUBLANES, LANES), lambda i: (i, 0))
    cnt = pl.BlockSpec((1, SUBLANES, LANES), lambda i: (i, 0, 0))
    return pl.pallas_call(
        _mix_kernel, grid=(n // TT,),
        in_specs=[row(SEG), row(SEG), row(SEG), row(D_MODEL), _full(wa), _full(wb), _full(wc),
                  _full(bo), _full(g1), _full(b1), _full(wrh), _full(wrl), _full(br)],
        out_specs=[row(D_MODEL), xs3, row(LANES), cnt],
        out_shape=[jax.ShapeDtypeStruct((n, D_MODEL), F32),
                   jax.ShapeDtypeStruct((n * TOP_K * SUBLANES, LANES), F32),
                   jax.ShapeDtypeStruct((n, LANES), F32),
                   jax.ShapeDtypeStruct((n // TT, SUBLANES, LANES), jnp.int32)],
        compiler_params=_params(), name="mix_ln_router")(
            a_out, o_b, c_out, x, wa, wb, wc, bo, g1, b1, wrh, wrl, br)


def _rows(start_row, n_rows):
    return pl.ds(pl.multiple_of(start_row * SUBLANES, SUBLANES), n_rows * SUBLANES)


def _expert_gather(tabs, blk, slot, xs3_hbm, buf, sem):
    blk_e, blk_s0, j_lo, j_hi, cnt, off, cum, tot = tabs
    e = blk_e[blk]
    s0 = blk_s0[blk]
    base = slot * TM

    @pl.when(tot[e] - s0 < TM)
    def _():
        buf[_rows(base, TM), :] = jnp.zeros((TM * SUBLANES, LANES), F32)

    def body(j, carry):
        run = cum[j * N_EXPERTS + e]
        lo = jnp.maximum(run, s0)
        hi = jnp.minimum(run + cnt[j * N_EXPERTS + e], s0 + TM)

        @pl.when(hi > lo)
        def _():
            src = j * (TOP_K * TT) + off[j * N_EXPERTS + e] + (lo - run)
            pltpu.make_async_copy(xs3_hbm.at[_rows(src, hi - lo)],
                                  buf.at[_rows(base + lo - s0, hi - lo)], sem.at[slot]).start()
        return carry

    lax.fori_loop(j_lo[blk], j_hi[blk], body, 0)


def _expert_gather_wait(tabs, blk, slot, xs3_hbm, buf, sem):
    blk_e, blk_s0, _, _, _, _, _, tot = tabs
    valid = jnp.minimum(tot[blk_e[blk]] - blk_s0[blk], TM)
    pltpu.make_async_copy(xs3_hbm.at[_rows(0, valid)], buf.at[_rows(slot * TM, valid)],
                          sem.at[slot]).wait()


def _expert_kernel(blk_e, blk_s0, j_lo, j_hi, cnt, off, cum, tot, n_used_ref,
                   xs3_hbm, w1_ref, b1_ref, w2_ref, b2_ref, y3_ref, buf, xb, w1b, w2b, sem):
    tabs = (blk_e, blk_s0, j_lo, j_hi, cnt, off, cum, tot)
    b = pl.program_id(0)
    n_used = n_used_ref[0]
    slot = lax.rem(b, 2)
    used = b < n_used

    @pl.when(b == 0)
    def _():
        _expert_gather(tabs, 0, 0, xs3_hbm, buf, sem)

    @pl.when(used)
    def _():
        _expert_gather_wait(tabs, b, slot, xs3_hbm, buf, sem)

    @pl.when(b + 1 < n_used)
    def _():
        _expert_gather(tabs, b + 1, 1 - slot, xs3_hbm, buf, sem)

    new_expert = jnp.logical_or(b == 0, blk_e[b] != blk_e[jnp.maximum(b - 1, 0)])

    @pl.when(jnp.logical_and(used, new_expert))
    def _():
        w1b[...] = w1_ref[0].astype(BF16)
        w2b[...] = w2_ref[0].astype(BF16)

    @pl.when(used)
    def _():
        base = slot * (TM * SUBLANES)
        for s in range(ROW_TILES):
            xb[:, s * LANES:(s + 1) * LANES] = buf[pl.ds(base + s, TM, stride=SUBLANES), :].astype(BF16)
        gu = jnp.dot(xb[...], w1b[...], preferred_element_type=F32) + b1_ref[0]
        gate = jnp.minimum(gu[:, :D_FF], SWIGLU_LIMIT)
        up = jnp.clip(gu[:, D_FF:], -SWIGLU_LIMIT, SWIGLU_LIMIT)
        hh = (up + 1.0) * (gate * jax.nn.sigmoid(SWIGLU_ALPHA * gate))
        y = jnp.dot(hh.astype(BF16), w2b[...], preferred_element_type=F32) + b2_ref[0]
        for s in range(ROW_TILES):
            y3_ref[pl.ds(s, TM, stride=SUBLANES), :] = y[:, s * LANES:(s + 1) * LANES]

    @pl.when(jnp.logical_not(used))
    def _():
        y3_ref[...] = jnp.zeros_like(y3_ref)


def _expert_call(tabs, n_used, xs3, w1, b1, w2, b2):
    n_blocks = tabs[0].shape[0]
    by_expert = lambda shape: pl.BlockSpec(shape, lambda b, e, *_: (e[b], 0, 0))
    grid_spec = pltpu.PrefetchScalarGridSpec(
        num_scalar_prefetch=len(tabs) + 1, grid=(n_blocks,),
        in_specs=[pl.BlockSpec(memory_space=pl.ANY),
                  by_expert((1, D_MODEL, 2 * D_FF)), by_expert((1, 1, 2 * D_FF)),
                  by_expert((1, D_FF, D_MODEL)), by_expert((1, 1, D_MODEL))],
        out_specs=pl.BlockSpec((TM * SUBLANES, LANES), lambda b, *_: (b, 0)),
        scratch_shapes=[pltpu.VMEM((2 * TM * SUBLANES, LANES), F32),
                        pltpu.VMEM((TM, D_MODEL), BF16),
                        pltpu.VMEM((D_MODEL, 2 * D_FF), BF16),
                        pltpu.VMEM((D_FF, D_MODEL), BF16),
                        pltpu.SemaphoreType.DMA((2,))])
    return pl.pallas_call(
        _expert_kernel, grid_spec=grid_spec,
        out_shape=jax.ShapeDtypeStruct((n_blocks * TM * SUBLANES, LANES), F32),
        compiler_params=_params(), name="expert_ffn")(*tabs, n_used, xs3, w1, b1, w2, b2)


PAIRS = TOP_K * TT


def _combine_gather(cnt, off, src, tile, slot, y3_hbm, buf, sem):
    def body(e, carry):
        n = cnt[tile * N_EXPERTS + e]

        @pl.when(n > 0)
        def _():
            pltpu.make_async_copy(y3_hbm.at[_rows(src[tile * N_EXPERTS + e], n)],
                                  buf.at[_rows(slot * PAIRS + off[tile * N_EXPERTS + e], n)],
                                  sem.at[slot]).start()
        return carry

    lax.fori_loop(0, N_EXPERTS, body, 0)


def _combine_kernel(cnt, off, src, y3_hbm, h_ref, aux_ref, g2_ref, b2_ref, *rest, n_prompt):
    if n_prompt is None:
        o_ref, buf, ys, sem = rest
    else:
        op_ref, os_ref, buf, ys, sem = rest
    i = pl.program_id(0)
    n_steps = pl.num_programs(0)
    slot = lax.rem(i, 2)

    @pl.when(i == 0)
    def _():
        _combine_gather(cnt, off, src, 0, 0, y3_hbm, buf, sem)

    pltpu.make_async_copy(y3_hbm.at[_rows(0, PAIRS)], buf.at[_rows(slot * PAIRS, PAIRS)],
                          sem.at[slot]).wait()

    @pl.when(i + 1 < n_steps)
    def _():
        _combine_gather(cnt, off, src, i + 1, 1 - slot, y3_hbm, buf, sem)

    base = slot * (PAIRS * SUBLANES)
    for s in range(ROW_TILES):
        ys[:, s * LANES:(s + 1) * LANES] = buf[pl.ds(base + s, PAIRS, stride=SUBLANES), :].astype(BF16)
    aux = aux_ref[...]
    dest = lax.broadcasted_iota(jnp.int32, (TT, PAIRS), 1).astype(F32)
    weights = jnp.zeros((TT, PAIRS), F32)
    for k in range(TOP_K):
        weights = weights + jnp.where(dest == aux[:, k:k + 1], aux[:, TOP_K + k:TOP_K + k + 1], 0.0)
    moe = jnp.dot(weights.astype(BF16), ys[...], preferred_element_type=F32)
    y = _ln_full(DN_ALPHA * h_ref[...] + moe, g2_ref[...], b2_ref[...])
    if n_prompt is None:
        o_ref[...] = y
    else:
        @pl.when(i < n_prompt)
        def _():
            op_ref[...] = y

        @pl.when(i >= n_prompt)
        def _():
            os_ref[...] = y


def _combine_call(cnt, off, src, y3, h, aux, g2, b2, tp=None):
    n = h.shape[0]
    row = lambda w: pl.BlockSpec((TT, w), lambda i, *_: (i, 0))
    if tp is None:
        n_prompt = None
        out_specs = row(D_MODEL)
        out_shape = jax.ShapeDtypeStruct((n, D_MODEL), F32)
    else:
        n_prompt = tp // TT
        out_specs = [pl.BlockSpec((TT, D_MODEL), lambda i, *_: (jnp.minimum(i, n_prompt - 1), 0)),
                     pl.BlockSpec((TT, D_MODEL), lambda i, *_: (jnp.maximum(i - n_prompt, 0), 0))]
        out_shape = [jax.ShapeDtypeStruct((tp, D_MODEL), F32),
                     jax.ShapeDtypeStruct((n - tp, D_MODEL), F32)]
    grid_spec = pltpu.PrefetchScalarGridSpec(
        num_scalar_prefetch=3, grid=(n // TT,),
        in_specs=[pl.BlockSpec(memory_space=pl.ANY), row(D_MODEL), row(LANES),
                  pl.BlockSpec(g2.shape, lambda i, *_: (0, 0)), pl.BlockSpec(b2.shape, lambda i, *_: (0, 0))],
        out_specs=out_specs,
        scratch_shapes=[pltpu.VMEM((2 * PAIRS * SUBLANES, LANES), F32),
                        pltpu.VMEM((PAIRS, D_MODEL), BF16),
                        pltpu.SemaphoreType.DMA((2,))])
    return pl.pallas_call(
        functools.partial(_combine_kernel, n_prompt=n_prompt), grid_spec=grid_spec, out_shape=out_shape,
        compiler_params=_params(), name="combine_ln")(cnt, off, src, y3, h, aux, g2, b2)


def _routing_tables(cnt_tiles, n):
    cnt = cnt_tiles[:, 0, :N_EXPERTS]
    off = jnp.cumsum(cnt, axis=1) - cnt
    cum = jnp.cumsum(cnt, axis=0) - cnt
    tot = jnp.sum(cnt, axis=0)
    padded = (tot + TM - 1) // TM * TM
    pends = jnp.cumsum(padded)
    pstart = pends - padded
    n_blocks = n * TOP_K // TM + N_EXPERTS
    blk_start = jnp.arange(n_blocks, dtype=jnp.int32) * TM
    blk_e = jnp.minimum(jnp.sum((pends[None, :] <= blk_start[:, None]).astype(jnp.int32), axis=1),
                        N_EXPERTS - 1)
    blk_s0 = blk_start - pstart[blk_e]
    run_lo = cum[:, blk_e].T
    run_hi = run_lo + cnt[:, blk_e].T
    j_lo = jnp.sum((run_hi <= blk_s0[:, None]).astype(jnp.int32), axis=1)
    j_hi = jnp.sum((run_lo < blk_s0[:, None] + TM).astype(jnp.int32), axis=1)
    n_used = (pends[-1:] // TM).astype(jnp.int32)
    flat = lambda a: a.reshape(-1).astype(jnp.int32)
    expert_tabs = (flat(blk_e), flat(blk_s0), flat(j_lo), flat(j_hi), flat(cnt), flat(off), flat(cum), flat(tot))
    combine_tabs = (flat(cnt), flat(off), flat(pstart[None, :] + cum))
    return expert_tabs, n_used, combine_tabs


def _pad_cols(a, width):
    return jnp.pad(a, [(0, 0)] * (a.ndim - 1) + [(0, width - a.shape[-1])])


def _layer_params(l, w_in, b_in, w_dw, b_dw, ln_a_g, ln_a_b, ln_c_g, ln_c_b, w_sp, b_sp,
                  w_out, b_out, ln1_g, ln1_b, w_router, b_router, w1, b1, w2, b2, ln2_g, ln2_b):
    bounds = [0, C_B, 2 * C_B, 3 * C_B, 3 * C_B + C_A, 3 * C_B + 2 * C_A, 3 * C_B + 2 * C_A + C_C,
              3 * C_B + 2 * C_A + 2 * C_C]
    w_segs = [_pad_cols(w_in[l][:, bounds[i]:bounds[i + 1]], SEG) for i in range(N_SEG)]
    b_segs = [_pad_cols(b_in[l][bounds[i]:bounds[i + 1]], SEG) for i in range(N_SEG)]
    row = lambda a: _pad_cols(a, SEG)[None, :]
    wo = w_out[l]
    pad_rows = lambda a: jnp.pad(a, ((0, SEG - a.shape[0]), (0, 0)))
    bsp = b_sp[l]
    bfull = _pad_cols(jnp.repeat(bsp.T, HEAD_DIM, axis=1), SEG)
    wr = _pad_cols(w_router[l], LANES)
    wr_hi = wr.astype(BF16)
    return dict(
        w_in=jnp.concatenate(w_segs, axis=1).astype(BF16),
        b_in=jnp.concatenate(b_segs)[None, :],
        w_dw=jnp.pad(w_dw[l], ((0, HALO - CONV_W), (0, SEG - C_A))),
        b_dw=row(b_dw[l]), ln_a_g=row(ln_a_g[l]), ln_a_b=row(ln_a_b[l]),
        ln_c_g=row(ln_c_g[l]), ln_c_b=row(ln_c_b[l]),
        w_sp=w_sp[l], b_full=bfull,
        w_sp_s=jnp.tile(w_sp[l][:, :DEC_SEQ, :DEC_SEQ], (1, TT // DEC_SEQ, TT // DEC_SEQ)),
        b_full_s=jnp.tile(bfull[:DEC_SEQ], (TT // DEC_SEQ, 1)),
        wa=pad_rows(wo[:C_A]).astype(BF16), wb=wo[C_A:C_A + C_B].astype(BF16),
        wc=pad_rows(wo[C_A + C_B:]).astype(BF16), b_out=b_out[l][None, :],
        ln1_g=ln1_g[l][None, :], ln1_b=ln1_b[l][None, :],
        w_router_hi=wr_hi, w_router_lo=(wr - wr_hi.astype(F32)).astype(BF16),
        b_router=jnp.concatenate([b_router[l], jnp.full((LANES - N_EXPERTS,), NEG, F32)])[None, :],
        w1=w1[l], b1=b1[l][:, None, :], w2=w2[l], b2=b2[l][:, None, :],
        ln2_g=ln2_g[l][None, :], ln2_b=ln2_b[l][None, :])


def _layer(x, p, hist_p, cache_k, cache_v, biases, tp, last):
    n = x.shape[0]
    q, k, v, a, cu, vn = _proj_call(x, p["w_in"], p["b_in"], p["ln_c_g"], p["ln_c_b"])

    conv_args = (p["w_dw"], p["b_dw"], p["ln_a_g"], p["ln_a_b"])
    a_out = _conv_prompt_call(a, *conv_args, tp)
    a_out, new_a_s = _conv_sample_call(hist_p, a, *conv_args, a_out, tp)

    merged = None
    for dil in DILATIONS:
        merged = _attn_prompt_call(q, k, v, biases["prompt"][dil], merged, dil, tp)
    o_b = _attn_sample_call(q, k, v, cache_k, cache_v, *biases["sample"], merged[0], tp)

    c_out = _gate_call(vn, cu, p["w_sp"], p["b_full"], CHUNK, 0, n // CHUNK)
    c_out = _gate_call(vn, cu, p["w_sp_s"], p["b_full_s"], TT, tp // TT, 1, sub=3, big=c_out)

    h, xs3, aux, cnt_tiles = _mix_call(a_out, o_b, c_out, x, p["wa"], p["wb"], p["wc"], p["b_out"],
                                       p["ln1_g"], p["ln1_b"], p["w_router_hi"], p["w_router_lo"],
                                       p["b_router"])
    expert_tabs, n_used, combine_tabs = _routing_tables(cnt_tiles, n)
    y3 = _expert_call(expert_tabs, n_used, xs3, p["w1"], p["b1"], p["w2"], p["b2"])
    y = _combine_call(*combine_tabs, y3, h, aux, p["ln2_g"], p["ln2_b"], tp=tp if last else None)
    return y, a, k, v, vn, new_a_s


def kernel(x_prompt, x_sample, state_a_conv, cache_b_k, cache_b_v, w_in, b_in, w_dw, b_dw, ln_a_g, ln_a_b, ln_c_g, ln_c_b, w_sp, b_sp, w_out, b_out, ln1_g, ln1_b, w_router, b_router, w1, b1, w2, b2, ln2_g, ln2_b):
    bp, tp, _ = x_prompt.shape
    bs, ts, _ = x_sample.shape
    n_cache = cache_b_k.shape[2]
    assert bp == 1 and ts == DEC_SEQ and bs * ts == TT and tp % WIN_MAX == 0
    assert n_cache == WIN_MAX
    n = tp + bs * ts
    keep = min(WIN_MAX, tp)
    hist = CONV_W - 1

    weights = (w_in, b_in, w_dw, b_dw, ln_a_g, ln_a_b, ln_c_g, ln_c_b, w_sp, b_sp, w_out, b_out,
               ln1_g, ln1_b, w_router, b_router, w1, b1, w2, b2, ln2_g, ln2_b)
    biases = {"prompt": {d: _prompt_bias(d) for d in DILATIONS}, "sample": _sample_bias(n_cache)}

    x = jnp.concatenate([x_prompt[0], x_sample.reshape(bs * ts, D_MODEL)], axis=0)
    outs = {name: [] for name in ("a_p", "a_s", "k_p", "v_p", "k_s", "v_s", "c_s")}
    for l in range(DEPTH):
        p = _layer_params(l, *weights)
        hist_p = jnp.pad(state_a_conv[l], ((0, 0), (HALO - hist, 0), (0, SEG - C_A)))
        ck = cache_b_k[l].reshape(bs, n_cache, C_B)
        cv = cache_b_v[l].reshape(bs, n_cache, C_B)
        x, a, k, v, vn, new_a_s = _layer(x, p, hist_p, ck, cv, biases, tp, last=l == DEPTH - 1)
        outs["a_p"].append(a[tp - hist:tp, :C_A][None])
        outs["a_s"].append(new_a_s[:, HALO - hist:, :C_A])
        outs["k_p"].append(k[tp - keep:tp].reshape(1, keep, N_HEADS_B, HEAD_DIM))
        outs["v_p"].append(v[tp - keep:tp].reshape(1, keep, N_HEADS_B, HEAD_DIM))
        outs["k_s"].append(k[tp:].reshape(bs, ts, N_HEADS_B, HEAD_DIM))
        outs["v_s"].append(v[tp:].reshape(bs, ts, N_HEADS_B, HEAD_DIM))
        outs["c_s"].append(vn[tp:, :C_C].reshape(bs, ts, C_C))
    stack = lambda name: jnp.stack(outs[name])
    y_prompt, y_sample = x
    return (y_prompt[None], y_sample.reshape(bs, ts, D_MODEL), stack("a_p"), stack("a_s"),
            stack("k_p"), stack("v_p"), stack("k_s"), stack("v_s"), stack("c_s"))
```

```python
import functools
import math

import jax
import jax.numpy as jnp
from jax import lax
from jax.experimental import pallas as pl
from jax.experimental.pallas import tpu as pltpu

F32 = jnp.float32
BF16 = jnp.bfloat16

D_MODEL = 1024
HEAD_DIM = 64
N_HEADS_B = 6
C_B = N_HEADS_B * HEAD_DIM
C_A = 320
C_C = 320
N_GROUPS_C = 5
CONV_W = 31
DILATIONS = (1, 4, 16)
BAND = 128
WIN_MAX = 2048
CHUNK = 128
N_EXPERTS = 32
TOP_K = 4
D_FF = 1024
SWIGLU_LIMIT = 7.0
SWIGLU_ALPHA = 1.702
LN_EPS = 1e-5
DEPTH = 2
DN_ALPHA = (2.0 * DEPTH) ** 0.25
DEC_SEQ = 8

LANES = 128
SUBLANES = 8
SEG = 384
N_SEG = 7
N_PAIRS = C_B // LANES
ROW_TILES = D_MODEL // LANES
assert ROW_TILES == SUBLANES

TT = 256
TM = 256
NEG = -1e30
VMEM_LIMIT = 48 * 1024 * 1024


def _params(n_axes=1):
    return pltpu.CompilerParams(dimension_semantics=("arbitrary",) * n_axes,
                                vmem_limit_bytes=VMEM_LIMIT)


def _full(a):
    nd = a.ndim
    return pl.BlockSpec(a.shape, lambda *_: (0,) * nd)


def _ln_valid(x, g, b, n_valid):
    col = lax.broadcasted_iota(jnp.int32, x.shape, 1)
    ok = col < n_valid
    mu = jnp.sum(jnp.where(ok, x, 0.0), axis=-1, keepdims=True) / n_valid
    xc = jnp.where(ok, x - mu, 0.0)
    var = jnp.sum(xc * xc, axis=-1, keepdims=True) / n_valid
    return xc * lax.rsqrt(var + LN_EPS) * g + b


def _ln_full(x, g, b):
    mu = jnp.mean(x, axis=-1, keepdims=True)
    xc = x - mu
    var = jnp.mean(xc * xc, axis=-1, keepdims=True)
    return xc * lax.rsqrt(var + LN_EPS) * g + b


def _proj_kernel(x_ref, w_ref, b_ref, lcg_ref, lcb_ref,
                 q_ref, k_ref, v_ref, a_ref, cu_ref, vn_ref):
    xb = x_ref[...].astype(BF16)

    def seg(i):
        lo = i * SEG
        return (jnp.dot(xb, w_ref[:, lo:lo + SEG], preferred_element_type=F32)
                + b_ref[:, lo:lo + SEG])

    for i, ref in enumerate((q_ref, k_ref, v_ref)):
        qkv = seg(i)
        for pair in range(N_PAIRS):
            ref[pair] = qkv[:, pair * LANES:(pair + 1) * LANES]
    a_ref[...] = seg(3) * jax.nn.sigmoid(seg(4))
    cu_ref[...] = seg(5)
    vn_ref[...] = _ln_valid(seg(6), lcg_ref[...], lcb_ref[...], C_C)


def _proj_call(x, w_in_p, b_in_p, lcg, lcb):
    n = x.shape[0]
    row = lambda w: pl.BlockSpec((TT, w), lambda i: (i, 0))
    heads = pl.BlockSpec((N_PAIRS, TT, LANES), lambda i: (0, i, 0))
    out = jax.ShapeDtypeStruct((n, SEG), F32)
    out3 = jax.ShapeDtypeStruct((N_PAIRS, n, LANES), F32)
    return pl.pallas_call(
        _proj_kernel, grid=(n // TT,),
        in_specs=[row(D_MODEL), _full(w_in_p), _full(b_in_p), _full(lcg), _full(lcb)],
        out_specs=[heads] * 3 + [row(SEG)] * 3, out_shape=[out3] * 3 + [out] * 3,
        compiler_params=_params(), name="in_proj")(x, w_in_p, b_in_p, lcg, lcb)


HALO = 32
CONV_CHUNK = 32


def _conv_tail(acc, bdw_ref, g_ref, b_ref):
    return jax.nn.silu(_ln_valid(acc + bdw_ref[...], g_ref[...], b_ref[...], C_A))


def _conv_prompt_kernel(halo_ref, a_ref, w_ref, bdw_ref, g_ref, b_ref, o_ref, buf, *, n_prompt):
    i = pl.program_id(0)

    @pl.when(i < n_prompt)
    def _():
        buf[0:HALO, :] = jnp.where(i > 0, halo_ref[...], 0.0)
        buf[HALO:, :] = a_ref[...]
        base = HALO - (CONV_W - 1)
        for r in range(0, TT, CONV_CHUNK):
            acc = jnp.zeros((CONV_CHUNK, SEG), F32)
            for j in range(CONV_W):
                acc = acc + w_ref[j:j + 1, :] * buf[base + r + j: base + r + j + CONV_CHUNK, :]
            o_ref[r:r + CONV_CHUNK, :] = _conv_tail(acc, bdw_ref, g_ref, b_ref)

    @pl.when(i >= n_prompt)
    def _():
        o_ref[...] = jnp.zeros_like(o_ref)


def _conv_prompt_call(a, w_dw_p, bdw, g, b, tp):
    n = a.shape[0]
    per = TT // HALO
    halo = pl.BlockSpec((HALO, SEG), lambda i: (jnp.maximum(i * per - 1, 0), 0))
    row = pl.BlockSpec((TT, SEG), lambda i: (i, 0))
    return pl.pallas_call(
        functools.partial(_conv_prompt_kernel, n_prompt=tp // TT), grid=(n // TT,),
        in_specs=[halo, row, _full(w_dw_p), _full(bdw), _full(g), _full(b)],
        out_specs=row, out_shape=jax.ShapeDtypeStruct((n, SEG), F32),
        scratch_shapes=[pltpu.VMEM((HALO + TT, SEG), F32)],
        compiler_params=_params(), name="conv_prompt")(a, a, w_dw_p, bdw, g, b)


SB = 8


def _conv_sample_kernel(hist_ref, a_ref, w_ref, bdw_ref, g_ref, b_ref, big_ref,
                        o_ref, newa_ref, buf):
    del big_ref
    bs = hist_ref.shape[0]
    buf[:, 0:HALO, :] = hist_ref[...]
    buf[:, HALO:, :] = a_ref[...].reshape(bs, DEC_SEQ, SEG)
    base = HALO - (CONV_W - 1)
    for s in range(0, bs, SB):
        acc = jnp.zeros((SB, DEC_SEQ, SEG), F32)
        for j in range(CONV_W):
            acc = acc + w_ref[j:j + 1, :][None] * buf[s:s + SB, base + j: base + j + DEC_SEQ, :]
        y = _conv_tail(acc.reshape(SB * DEC_SEQ, SEG), bdw_ref, g_ref, b_ref)
        o_ref[s * DEC_SEQ:(s + SB) * DEC_SEQ, :] = y
    newa_ref[...] = buf[:, DEC_SEQ:, :]


def _conv_sample_call(hist_p, a, w_dw_p, bdw, g, b, a_out, tp):
    bs = hist_p.shape[0]
    rows = bs * DEC_SEQ
    blk = pl.BlockSpec((rows, SEG), lambda i: (tp // rows, 0))
    return pl.pallas_call(
        _conv_sample_kernel, grid=(1,),
        in_specs=[_full(hist_p), blk, _full(w_dw_p), _full(bdw), _full(g), _full(b),
                  pl.BlockSpec(memory_space=pl.ANY)],
        out_specs=[blk, pl.BlockSpec((bs, HALO, SEG), lambda i: (0, 0, 0))],
        out_shape=[jax.ShapeDtypeStruct(a_out.shape, F32),
                   jax.ShapeDtypeStruct((bs, HALO, SEG), F32)],
        scratch_shapes=[pltpu.VMEM((bs, HALO + DEC_SEQ, SEG), F32)],
        input_output_aliases={6: 0},
        compiler_params=_params(), name="conv_sample")(hist_p, a, w_dw_p, bdw, g, b, a_out)


QB = 128
assert QB == BAND


SPAN = BAND * max(DILATIONS)
UNITS = SPAN // QB


def _attn_prompt_kernel(q_ref, kh_ref, kc_ref, vh_ref, vc_ref, b1_ref, b4_ref, b16_ref, o_ref,
                        kbuf, vbuf, oacc, lacc, *, n_prompt):
    is_prompt = pl.program_id(1) < n_prompt
    refs = (q_ref, kh_ref, kc_ref, vh_ref, vc_ref, (b1_ref, b4_ref, b16_ref), o_ref, kbuf, vbuf, oacc, lacc)
    pl.when(is_prompt)(functools.partial(_attn_prompt_span, refs))

    @pl.when(jnp.logical_not(is_prompt))
    def _():
        o_ref[...] = jnp.zeros_like(o_ref)


def _attn_prompt_span(refs):
    q_ref, kh_ref, kc_ref, vh_ref, vc_ref, bias_refs, o_ref, kbuf, vbuf, oacc, lacc = refs
    pair = pl.program_id(0)
    first_span = pl.program_id(1) == 0
    kbuf[0:SPAN, :] = kh_ref[0]
    kbuf[SPAN:, :] = kc_ref[0]
    vbuf[0:SPAN, :] = vh_ref[0]
    vbuf[SPAN:, :] = vc_ref[0]
    kcol = lax.broadcasted_iota(jnp.int32, (QB, 2 * QB), 1)
    lane = lax.broadcasted_iota(jnp.int32, (QB, LANES), 1)
    lo_half = lane < HEAD_DIM

    for dil, bias_ref in zip(DILATIONS, bias_refs):
        def unit(u, carry, dil=dil, bias_ref=bias_ref):
            sub = u // dil
            t0 = sub * (QB * dil) + (u - sub * dil)
            if dil == 1:
                rows_q = pl.ds(pl.multiple_of(t0, QB), QB)
                rows_k = pl.ds(pl.multiple_of(SPAN + t0 - QB, QB), 2 * QB)
            else:
                rows_q = pl.ds(t0, QB, stride=dil)
                rows_k = pl.ds(SPAN + t0 - QB * dil, 2 * QB, stride=dil)
            no_halo = jnp.logical_and(jnp.logical_and(first_span, sub == 0), kcol < QB)
            qp = q_ref[0, rows_q, :] * (HEAD_DIM ** -0.5)
            kk = kbuf[rows_k, :].astype(BF16)
            vv = vbuf[rows_k, :].astype(BF16)
            outs, lses = [], []
            for hh in range(2):
                keep = lo_half if hh == 0 else jnp.logical_not(lo_half)
                qm = jnp.where(keep, qp, 0.0).astype(BF16)
                s = lax.dot_general(qm, kk, (((1,), (1,)), ((), ())), preferred_element_type=F32)
                s = s + bias_ref[pair * 2 + hh]
                s = jnp.where(no_halo, NEG, s)
                m = jnp.max(s, axis=-1, keepdims=True)
                p = jnp.exp(s - m)
                l = jnp.sum(p, axis=-1, keepdims=True)
                pv = jnp.dot(p.astype(BF16), vv, preferred_element_type=F32)
                outs.append(pv / l)
                lses.append(m + jnp.log(l))
            o_new = jnp.where(lo_half, outs[0], outs[1])
            l_new = jnp.where(lo_half, lses[0], lses[1])
            if dil != 1:
                l_old = lacc[rows_q, :]
                mx = jnp.maximum(l_old, l_new)
                w_old = jnp.exp(l_old - mx)
                w_new = jnp.exp(l_new - mx)
                tot = w_old + w_new
                o_new = (w_old * oacc[rows_q, :] + w_new * o_new) / tot
                l_new = mx + jnp.log(tot)
            oacc[rows_q, :] = o_new
            lacc[rows_q, :] = l_new
            return carry

        lax.fori_loop(0, UNITS, unit, 0)
    o_ref[0] = oacc[...]


def _attn_prompt_call(q, k, v, biases, tp):
    n = q.shape[1]
    cur = pl.BlockSpec((1, SPAN, LANES), lambda p, m: (p, m, 0))
    halo = pl.BlockSpec((1, SPAN, LANES), lambda p, m: (p, jnp.maximum(m - 1, 0), 0))
    scratch = [pltpu.VMEM((2 * SPAN, LANES), F32), pltpu.VMEM((2 * SPAN, LANES), F32),
               pltpu.VMEM((SPAN, LANES), F32), pltpu.VMEM((SPAN, LANES), F32)]
    return pl.pallas_call(
        functools.partial(_attn_prompt_kernel, n_prompt=tp // SPAN),
        grid=(N_PAIRS, pl.cdiv(n, SPAN)),
        in_specs=[cur, halo, cur, halo, cur] + [_full(b) for b in biases],
        out_specs=cur, out_shape=jax.ShapeDtypeStruct((N_PAIRS, n, LANES), F32),
        scratch_shapes=scratch, compiler_params=_params(2), name="attn_prompt")(q, k, k, v, v, *biases)


def _alibi_slopes():
    return [2.0 ** (-8.0 * (h + 1) / N_HEADS_B) for h in range(N_HEADS_B)]


def _prompt_bias(dil):
    i = jnp.arange(QB)[:, None]
    j = jnp.arange(2 * QB)[None, :]
    rel = i + QB - j
    ok = jnp.logical_and(rel >= 0, rel <= BAND)
    slopes = jnp.asarray(_alibi_slopes(), F32)[:, None, None]
    pen = -slopes * (rel * dil).astype(F32)[None]
    return jnp.where(ok[None], pen, NEG).astype(F32)


def _branch_multiplicity(dist):
    cnt = jnp.zeros(dist.shape, jnp.int32)
    for dil in DILATIONS:
        cnt = cnt + jnp.logical_and(dist % dil == 0, dist <= BAND * dil).astype(jnp.int32)
    return cnt


def _sample_bias(n_cache):
    t = jnp.arange(DEC_SEQ)[:, None]
    dist_c = n_cache + t - jnp.arange(n_cache)[None, :]
    dist_n = t - jnp.arange(LANES)[None, :]
    ok_n = jnp.logical_and(dist_n >= 0, jnp.arange(LANES)[None, :] < DEC_SEQ)
    slopes = jnp.asarray(_alibi_slopes(), F32)[:, None, None]

    def bias(dist, ok):
        mult = _branch_multiplicity(jnp.maximum(dist, 0))
        ok = jnp.logical_and(ok, mult > 0)
        val = -slopes * dist.astype(F32)[None] + jnp.log(jnp.maximum(mult, 1).astype(F32))[None]
        return jnp.where(ok[None], val, NEG).astype(F32)

    bc = bias(dist_c, jnp.ones(dist_c.shape, bool)).reshape(N_HEADS_B * DEC_SEQ, n_cache)
    bn = bias(dist_n, ok_n).reshape(N_HEADS_B * DEC_SEQ, LANES)
    return bc, bn


def _attn_sample_kernel(q_ref, kn_ref, vn_ref, kc_ref, vc_ref, bc_ref, bn_ref, big_ref, o_ref):
    del big_ref
    rows = N_HEADS_B * DEC_SEQ
    wide = lambda ref: jnp.concatenate([ref[pair] for pair in range(N_PAIRS)], axis=1)
    q = wide(q_ref) * (HEAD_DIM ** -0.5)
    qrep = jnp.concatenate([q] * N_HEADS_B, axis=0)
    rh = lax.shift_right_logical(lax.broadcasted_iota(jnp.int32, (rows, SEG), 0), 3)
    ch = lax.shift_right_logical(lax.broadcasted_iota(jnp.int32, (rows, SEG), 1), 6)
    own = rh == ch
    qm = jnp.where(own, qrep, 0.0).astype(BF16)
    pad = jnp.zeros((LANES - DEC_SEQ, SEG), F32)
    kn = jnp.concatenate([wide(kn_ref), pad], axis=0).astype(BF16)
    vn = jnp.concatenate([wide(vn_ref), pad], axis=0).astype(BF16)
    nt = (((1,), (1,)), ((), ()))
    s_c = jnp.dot(qm, kc_ref[0, 0].astype(BF16), preferred_element_type=F32) + bc_ref[...]
    s_n = lax.dot_general(qm, kn, nt, preferred_element_type=F32) + bn_ref[...]
    m = jnp.maximum(jnp.max(s_c, axis=-1, keepdims=True), jnp.max(s_n, axis=-1, keepdims=True))
    p_c = jnp.exp(s_c - m)
    p_n = jnp.exp(s_n - m)
    l = jnp.sum(p_c, axis=-1, keepdims=True) + jnp.sum(p_n, axis=-1, keepdims=True)
    r = (lax.dot_general(p_c.astype(BF16), vc_ref[0, 0].astype(BF16), nt, preferred_element_type=F32)
         + jnp.dot(p_n.astype(BF16), vn, preferred_element_type=F32)) / l
    r = jnp.where(own, r, 0.0)
    o = r[0:DEC_SEQ]
    for h in range(1, N_HEADS_B):
        o = o + r[h * DEC_SEQ:(h + 1) * DEC_SEQ]
    for pair in range(N_PAIRS):
        o_ref[pair] = o[:, pair * LANES:(pair + 1) * LANES]


def _attn_sample_call(q, k, v, cache_kt, cache_vt, layer, bc, bn, o_big, tp):
    bs, n_cache = cache_kt.shape[1], cache_kt.shape[3]
    new = pl.BlockSpec((N_PAIRS, DEC_SEQ, LANES), lambda b: (0, tp // DEC_SEQ + b, 0))
    cache = pl.BlockSpec((1, 1, C_B, n_cache), lambda b: (layer, b, 0, 0))
    return pl.pallas_call(
        _attn_sample_kernel, grid=(bs,),
        in_specs=[new, new, new, cache, cache, _full(bc), _full(bn),
                  pl.BlockSpec(memory_space=pl.ANY)],
        out_specs=new, out_shape=jax.ShapeDtypeStruct(o_big.shape, F32),
        input_output_aliases={7: 0},
        compiler_params=_params(), name="attn_sample")(q, k, v, cache_kt, cache_vt, bc, bn, o_big)


def _gate_kernel(*refs, rows, sub, aliased):
    if aliased:
        vn_ref, cu_ref, w_ref, bf_ref, big_ref, o_ref = refs
        del big_ref
    else:
        vn_ref, cu_ref, w_ref, bf_ref, o_ref = refs
    r = lax.broadcasted_iota(jnp.int32, (rows, rows), 0)
    c = lax.broadcasted_iota(jnp.int32, (rows, rows), 1)
    causal = c <= r
    if sub is not None:
        causal = jnp.logical_and(causal, lax.shift_right_logical(r, sub) == lax.shift_right_logical(c, sub))
    lane = lax.broadcasted_iota(jnp.int32, (rows, LANES), 1)
    lo_half = lane < HEAD_DIM
    for pair in range(SEG // LANES):
        cs = slice(pair * LANES, (pair + 1) * LANES)
        vp = vn_ref[:, cs]
        acc = jnp.zeros((rows, LANES), F32)
        for hh in range(2):
            g = pair * 2 + hh
            if g >= N_GROUPS_C:
                continue
            keep = lo_half if hh == 0 else jnp.logical_not(lo_half)
            wm = jnp.where(causal, w_ref[g], 0.0).astype(BF16)
            vm = jnp.where(keep, vp, 0.0).astype(BF16)
            acc = acc + jnp.dot(wm, vm, preferred_element_type=F32)
        o_ref[:, cs] = cu_ref[:, cs] * (acc + bf_ref[:, cs])


def _gate_call(vn, cu, w, bfull, rows, first_block, n_blocks, sub=None, big=None):
    blk = pl.BlockSpec((rows, SEG), lambda i: (first_block + i, 0))
    args = [vn, cu, w, bfull]
    specs = [blk, blk, _full(w), _full(bfull)]
    kwargs = {}
    if big is not None:
        args.append(big)
        specs.append(pl.BlockSpec(memory_space=pl.ANY))
        kwargs["input_output_aliases"] = {4: 0}
    return pl.pallas_call(
        functools.partial(_gate_kernel, rows=rows, sub=sub, aliased=big is not None),
        grid=(n_blocks,), in_specs=specs, out_specs=blk,
        out_shape=jax.ShapeDtypeStruct(vn.shape, F32),
        compiler_params=_params(), name="spatial_gate", **kwargs)(*args)


def _mix_kernel(a_ref, o_ref, c_ref, x_ref, wa_ref, wb_ref, wc_ref, bo_ref, g1_ref, b1_ref,
                wrh_ref, wrl_ref, br_ref, h_ref, xs3_ref, aux_ref, cnt_ref):
    o_b = jnp.concatenate([o_ref[pair] for pair in range(N_PAIRS)], axis=1)
    mix = (jnp.dot(a_ref[...].astype(BF16), wa_ref[...], preferred_element_type=F32)
           + jnp.dot(o_b.astype(BF16), wb_ref[...], preferred_element_type=F32)
           + jnp.dot(c_ref[...].astype(BF16), wc_ref[...], preferred_element_type=F32)
           + bo_ref[...])
    h = _ln_full(DN_ALPHA * x_ref[...] + mix, g1_ref[...], b1_ref[...])
    h_ref[...] = h

    hb = h.astype(BF16)
    h_lo = (h - hb.astype(F32)).astype(BF16)
    logits = (jnp.dot(hb, wrh_ref[...], preferred_element_type=F32)
              + jnp.dot(h_lo, wrh_ref[...], preferred_element_type=F32)
              + jnp.dot(hb, wrl_ref[...], preferred_element_type=F32) + br_ref[...])
    lane = lax.broadcasted_iota(jnp.int32, (TT, LANES), 1)
    lane_f = lane.astype(F32)
    vals, idxs, sels = [], [], []
    cur = logits
    for _ in range(TOP_K):
        m = jnp.max(cur, axis=-1, keepdims=True)
        idx = jnp.min(jnp.where(cur == m, lane_f, float(LANES)), axis=-1, keepdims=True)
        sel = lane_f == idx
        vals.append(m)
        idxs.append(idx)
        sels.append(sel)
        cur = jnp.where(sel, -jnp.inf, cur)
    exps = [jnp.exp(v - vals[0]) for v in vals]
    den = exps[0] + exps[1] + exps[2] + exps[3]

    onehot = jnp.zeros((TT, LANES), F32)
    for sel in sels:
        onehot = onehot + sel.astype(F32)
    r = lax.broadcasted_iota(jnp.int32, (TT, TT), 0)
    c = lax.broadcasted_iota(jnp.int32, (TT, TT), 1)
    below = (c < r).astype(BF16)
    earlier = jnp.dot(below, onehot.astype(BF16), preferred_element_type=F32)
    cnt = jnp.broadcast_to(jnp.sum(onehot, axis=0, keepdims=True), (SUBLANES, LANES))
    er = lax.broadcasted_iota(jnp.int32, (LANES, LANES), 0)
    ec = lax.broadcasted_iota(jnp.int32, (LANES, LANES), 1)
    off = jnp.dot(cnt.astype(BF16), (er < ec).astype(BF16), preferred_element_type=F32)[0:1]
    place = earlier + off
    aux = jnp.zeros((TT, LANES), F32)
    rows = []
    for k in range(TOP_K):
        row = jnp.sum(jnp.where(sels[k], place, 0.0), axis=-1, keepdims=True)
        rows.append(row)
        aux = aux + jnp.where(lane == k, row, 0.0) + jnp.where(lane == TOP_K + k, exps[k] / den, 0.0)
    aux_ref[...] = aux
    cnt_ref[0] = cnt.astype(jnp.int32)

    dest = lax.broadcasted_iota(jnp.int32, (TT, TOP_K * TT), 1).astype(F32)
    disp = jnp.zeros((TT, TOP_K * TT), F32)
    for k in range(TOP_K):
        disp = disp + (dest == rows[k]).astype(F32)
    xs = lax.dot_general(disp.astype(BF16), hb, (((0,), (0,)), ((), ())), preferred_element_type=F32)
    for s in range(ROW_TILES):
        xs3_ref[pl.ds(s, TOP_K * TT, stride=SUBLANES), :] = xs[:, s * LANES:(s + 1) * LANES]


def _mix_call(a_out, o_b, c_out, x, wa, wb, wc, bo, g1, b1, wrh, wrl, br):
    n = x.shape[0]
    row = lambda w: pl.BlockSpec((TT, w), lambda i: (i, 0))
    xs3 = pl.BlockSpec((TOP_K * TT * SUBLANES, LANES), lambda i: (i, 0))
    cnt = pl.BlockSpec((1, SUBLANES, LANES), lambda i: (i, 0, 0))
    return pl.pallas_call(
        _mix_kernel, grid=(n // TT,),
        in_specs=[row(SEG), pl.BlockSpec((N_PAIRS, TT, LANES), lambda i: (0, i, 0)), row(SEG), row(D_MODEL),
                  _full(wa), _full(wb), _full(wc),
                  _full(bo), _full(g1), _full(b1), _full(wrh), _full(wrl), _full(br)],
        out_specs=[row(D_MODEL), xs3, row(LANES), cnt],
        out_shape=[jax.ShapeDtypeStruct((n, D_MODEL), F32),
                   jax.ShapeDtypeStruct((n * TOP_K * SUBLANES, LANES), F32),
                   jax.ShapeDtypeStruct((n, LANES), F32),
                   jax.ShapeDtypeStruct((n // TT, SUBLANES, LANES), jnp.int32)],
        compiler_params=_params(), name="mix_ln_router")(
            a_out, o_b, c_out, x, wa, wb, wc, bo, g1, b1, wrh, wrl, br)


def _rows(start_row, n_rows):
    return pl.ds(pl.multiple_of(start_row * SUBLANES, SUBLANES), n_rows * SUBLANES)


def _expert_gather(tabs, blk, slot, xs3_hbm, buf, sem):
    blk_e, blk_s0, j_lo, j_hi, cnt, off, cum, tot = tabs
    e = blk_e[blk]
    s0 = blk_s0[blk]
    base = slot * TM

    @pl.when(tot[e] - s0 < TM)
    def _():
        buf[_rows(base, TM), :] = jnp.zeros((TM * SUBLANES, LANES), F32)

    def body(j, carry):
        run = cum[j * N_EXPERTS + e]
        lo = jnp.maximum(run, s0)
        hi = jnp.minimum(run + cnt[j * N_EXPERTS + e], s0 + TM)

        @pl.when(hi > lo)
        def _():
            src = j * (TOP_K * TT) + off[j * N_EXPERTS + e] + (lo - run)
            pltpu.make_async_copy(xs3_hbm.at[_rows(src, hi - lo)],
                                  buf.at[_rows(base + lo - s0, hi - lo)], sem.at[slot]).start()
        return carry

    lax.fori_loop(j_lo[blk], j_hi[blk], body, 0)


def _expert_gather_wait(tabs, blk, slot, xs3_hbm, buf, sem):
    blk_e, blk_s0, _, _, _, _, _, tot = tabs
    valid = jnp.minimum(tot[blk_e[blk]] - blk_s0[blk], TM)
    pltpu.make_async_copy(xs3_hbm.at[_rows(0, valid)], buf.at[_rows(slot * TM, valid)],
                          sem.at[slot]).wait()


def _expert_kernel(blk_e, blk_s0, j_lo, j_hi, cnt, off, cum, tot, n_used_ref,
                   xs3_hbm, w1_ref, b1_ref, w2_ref, b2_ref, y3_ref, buf, xb, w1b, w2b, sem):
    tabs = (blk_e, blk_s0, j_lo, j_hi, cnt, off, cum, tot)
    b = pl.program_id(0)
    n_used = n_used_ref[0]
    slot = lax.rem(b, 2)
    used = b < n_used

    @pl.when(b == 0)
    def _():
        _expert_gather(tabs, 0, 0, xs3_hbm, buf, sem)

    @pl.when(used)
    def _():
        _expert_gather_wait(tabs, b, slot, xs3_hbm, buf, sem)

    @pl.when(b + 1 < n_used)
    def _():
        _expert_gather(tabs, b + 1, 1 - slot, xs3_hbm, buf, sem)

    new_expert = jnp.logical_or(b == 0, blk_e[b] != blk_e[jnp.maximum(b - 1, 0)])

    @pl.when(jnp.logical_and(used, new_expert))
    def _():
        w1b[...] = w1_ref[0].astype(BF16)
        w2b[...] = w2_ref[0].astype(BF16)

    @pl.when(used)
    def _():
        base = slot * (TM * SUBLANES)
        for s in range(ROW_TILES):
            xb[:, s * LANES:(s + 1) * LANES] = buf[pl.ds(base + s, TM, stride=SUBLANES), :].astype(BF16)
        gu = jnp.dot(xb[...], w1b[...], preferred_element_type=F32) + b1_ref[0]
        gate = jnp.minimum(gu[:, :D_FF], SWIGLU_LIMIT)
        up = jnp.clip(gu[:, D_FF:], -SWIGLU_LIMIT, SWIGLU_LIMIT)
        hh = (up + 1.0) * (gate * jax.nn.sigmoid(SWIGLU_ALPHA * gate))
        y = jnp.dot(hh.astype(BF16), w2b[...], preferred_element_type=F32) + b2_ref[0]
        for s in range(ROW_TILES):
            y3_ref[pl.ds(s, TM, stride=SUBLANES), :] = y[:, s * LANES:(s + 1) * LANES]

    @pl.when(jnp.logical_not(used))
    def _():
        y3_ref[...] = jnp.zeros_like(y3_ref)


def _expert_call(tabs, n_used, xs3, w1, b1, w2, b2, layer):
    n_blocks = tabs[0].shape[0]
    by_expert = lambda shape: pl.BlockSpec(shape, lambda b, e, *_: (layer * N_EXPERTS + e[b], 0, 0))
    grid_spec = pltpu.PrefetchScalarGridSpec(
        num_scalar_prefetch=len(tabs) + 1, grid=(n_blocks,),
        in_specs=[pl.BlockSpec(memory_space=pl.ANY),
                  by_expert((1, D_MODEL, 2 * D_FF)), by_expert((1, 1, 2 * D_FF)),
                  by_expert((1, D_FF, D_MODEL)), by_expert((1, 1, D_MODEL))],
        out_specs=pl.BlockSpec((TM * SUBLANES, LANES), lambda b, *_: (b, 0)),
        scratch_shapes=[pltpu.VMEM((2 * TM * SUBLANES, LANES), F32),
                        pltpu.VMEM((TM, D_MODEL), BF16),
                        pltpu.VMEM((D_MODEL, 2 * D_FF), BF16),
                        pltpu.VMEM((D_FF, D_MODEL), BF16),
                        pltpu.SemaphoreType.DMA((2,))])
    return pl.pallas_call(
        _expert_kernel, grid_spec=grid_spec,
        out_shape=jax.ShapeDtypeStruct((n_blocks * TM * SUBLANES, LANES), F32),
        compiler_params=_params(), name="expert_ffn")(*tabs, n_used, xs3, w1, b1, w2, b2)


PAIRS = TOP_K * TT


def _combine_gather(cnt, off, src, tile, slot, y3_hbm, buf, sem):
    def body(e, carry):
        n = cnt[tile * N_EXPERTS + e]

        @pl.when(n > 0)
        def _():
            pltpu.make_async_copy(y3_hbm.at[_rows(src[tile * N_EXPERTS + e], n)],
                                  buf.at[_rows(slot * PAIRS + off[tile * N_EXPERTS + e], n)],
                                  sem.at[slot]).start()
        return carry

    lax.fori_loop(0, N_EXPERTS, body, 0)


def _combine_kernel(cnt, off, src, y3_hbm, h_ref, aux_ref, g2_ref, b2_ref, *rest, n_prompt):
    if n_prompt is None:
        o_ref, buf, ys, sem = rest
    else:
        op_ref, os_ref, buf, ys, sem = rest
    i = pl.program_id(0)
    n_steps = pl.num_programs(0)
    slot = lax.rem(i, 2)

    @pl.when(i == 0)
    def _():
        _combine_gather(cnt, off, src, 0, 0, y3_hbm, buf, sem)

    pltpu.make_async_copy(y3_hbm.at[_rows(0, PAIRS)], buf.at[_rows(slot * PAIRS, PAIRS)],
                          sem.at[slot]).wait()

    @pl.when(i + 1 < n_steps)
    def _():
        _combine_gather(cnt, off, src, i + 1, 1 - slot, y3_hbm, buf, sem)

    base = slot * (PAIRS * SUBLANES)
    for s in range(ROW_TILES):
        ys[:, s * LANES:(s + 1) * LANES] = buf[pl.ds(base + s, PAIRS, stride=SUBLANES), :].astype(BF16)
    aux = aux_ref[...]
    dest = lax.broadcasted_iota(jnp.int32, (TT, PAIRS), 1).astype(F32)
    weights = jnp.zeros((TT, PAIRS), F32)
    for k in range(TOP_K):
        weights = weights + jnp.where(dest == aux[:, k:k + 1], aux[:, TOP_K + k:TOP_K + k + 1], 0.0)
    moe = jnp.dot(weights.astype(BF16), ys[...], preferred_element_type=F32)
    y = _ln_full(DN_ALPHA * h_ref[...] + moe, g2_ref[...], b2_ref[...])
    if n_prompt is None:
        o_ref[...] = y
    else:
        @pl.when(i < n_prompt)
        def _():
            op_ref[...] = y

        @pl.when(i >= n_prompt)
        def _():
            os_ref[...] = y


def _combine_call(cnt, off, src, y3, h, aux, g2, b2, tp=None):
    n = h.shape[0]
    row = lambda w: pl.BlockSpec((TT, w), lambda i, *_: (i, 0))
    if tp is None:
        n_prompt = None
        out_specs = row(D_MODEL)
        out_shape = jax.ShapeDtypeStruct((n, D_MODEL), F32)
    else:
        n_prompt = tp // TT
        out_specs = [pl.BlockSpec((TT, D_MODEL), lambda i, *_: (jnp.minimum(i, n_prompt - 1), 0)),
                     pl.BlockSpec((TT, D_MODEL), lambda i, *_: (jnp.maximum(i - n_prompt, 0), 0))]
        out_shape = [jax.ShapeDtypeStruct((tp, D_MODEL), F32),
                     jax.ShapeDtypeStruct((n - tp, D_MODEL), F32)]
    grid_spec = pltpu.PrefetchScalarGridSpec(
        num_scalar_prefetch=3, grid=(n // TT,),
        in_specs=[pl.BlockSpec(memory_space=pl.ANY), row(D_MODEL), row(LANES),
                  pl.BlockSpec(g2.shape, lambda i, *_: (0, 0)), pl.BlockSpec(b2.shape, lambda i, *_: (0, 0))],
        out_specs=out_specs,
        scratch_shapes=[pltpu.VMEM((2 * PAIRS * SUBLANES, LANES), F32),
                        pltpu.VMEM((PAIRS, D_MODEL), BF16),
                        pltpu.SemaphoreType.DMA((2,))])
    return pl.pallas_call(
        functools.partial(_combine_kernel, n_prompt=n_prompt), grid_spec=grid_spec, out_shape=out_shape,
        compiler_params=_params(), name="combine_ln")(cnt, off, src, y3, h, aux, g2, b2)


def _routing_tables(cnt_tiles, n):
    cnt = cnt_tiles[:, 0, :N_EXPERTS]
    off = jnp.cumsum(cnt, axis=1) - cnt
    cum = jnp.cumsum(cnt, axis=0) - cnt
    tot = jnp.sum(cnt, axis=0)
    padded = (tot + TM - 1) // TM * TM
    pends = jnp.cumsum(padded)
    pstart = pends - padded
    n_blocks = n * TOP_K // TM + N_EXPERTS
    blk_start = jnp.arange(n_blocks, dtype=jnp.int32) * TM
    blk_e = jnp.minimum(jnp.sum((pends[None, :] <= blk_start[:, None]).astype(jnp.int32), axis=1),
                        N_EXPERTS - 1)
    blk_s0 = blk_start - pstart[blk_e]
    run_lo = cum[:, blk_e].T
    run_hi = run_lo + cnt[:, blk_e].T
    j_lo = jnp.sum((run_hi <= blk_s0[:, None]).astype(jnp.int32), axis=1)
    j_hi = jnp.sum((run_lo < blk_s0[:, None] + TM).astype(jnp.int32), axis=1)
    n_used = (pends[-1:] // TM).astype(jnp.int32)
    flat = lambda a: a.reshape(-1).astype(jnp.int32)
    expert_tabs = (flat(blk_e), flat(blk_s0), flat(j_lo), flat(j_hi), flat(cnt), flat(off), flat(cum), flat(tot))
    combine_tabs = (flat(cnt), flat(off), flat(pstart[None, :] + cum))
    return expert_tabs, n_used, combine_tabs


def _pad_cols(a, width):
    return jnp.pad(a, [(0, 0)] * (a.ndim - 1) + [(0, width - a.shape[-1])])


def _layer_params(l, w_in, b_in, w_dw, b_dw, ln_a_g, ln_a_b, ln_c_g, ln_c_b, w_sp, b_sp,
                  w_out, b_out, ln1_g, ln1_b, w_router, b_router, ln2_g, ln2_b):
    bounds = [0, C_B, 2 * C_B, 3 * C_B, 3 * C_B + C_A, 3 * C_B + 2 * C_A, 3 * C_B + 2 * C_A + C_C,
              3 * C_B + 2 * C_A + 2 * C_C]
    w_segs = [_pad_cols(w_in[l][:, bounds[i]:bounds[i + 1]], SEG) for i in range(N_SEG)]
    b_segs = [_pad_cols(b_in[l][bounds[i]:bounds[i + 1]], SEG) for i in range(N_SEG)]
    row = lambda a: _pad_cols(a, SEG)[None, :]
    wo = w_out[l]
    pad_rows = lambda a: jnp.pad(a, ((0, SEG - a.shape[0]), (0, 0)))
    bsp = b_sp[l]
    bfull = _pad_cols(jnp.repeat(bsp.T, HEAD_DIM, axis=1), SEG)
    wr = _pad_cols(w_router[l], LANES)
    wr_hi = wr.astype(BF16)
    return dict(
        w_in=jnp.concatenate(w_segs, axis=1).astype(BF16),
        b_in=jnp.concatenate(b_segs)[None, :],
        w_dw=jnp.pad(w_dw[l], ((0, HALO - CONV_W), (0, SEG - C_A))),
        b_dw=row(b_dw[l]), ln_a_g=row(ln_a_g[l]), ln_a_b=row(ln_a_b[l]),
        ln_c_g=row(ln_c_g[l]), ln_c_b=row(ln_c_b[l]),
        w_sp=w_sp[l], b_full=bfull,
        w_sp_s=jnp.tile(w_sp[l][:, :DEC_SEQ, :DEC_SEQ], (1, TT // DEC_SEQ, TT // DEC_SEQ)),
        b_full_s=jnp.tile(bfull[:DEC_SEQ], (TT // DEC_SEQ, 1)),
        wa=pad_rows(wo[:C_A]).astype(BF16), wb=wo[C_A:C_A + C_B].astype(BF16),
        wc=pad_rows(wo[C_A + C_B:]).astype(BF16), b_out=b_out[l][None, :],
        ln1_g=ln1_g[l][None, :], ln1_b=ln1_b[l][None, :],
        w_router_hi=wr_hi, w_router_lo=(wr - wr_hi.astype(F32)).astype(BF16),
        b_router=jnp.concatenate([b_router[l], jnp.full((LANES - N_EXPERTS,), NEG, F32)])[None, :],
        ln2_g=ln2_g[l][None, :], ln2_b=ln2_b[l][None, :])


def _layer(x, p, hist_p, caches, experts, biases, tp, layer):
    n = x.shape[0]
    q, k, v, a, cu, vn = _proj_call(x, p["w_in"], p["b_in"], p["ln_c_g"], p["ln_c_b"])

    conv_args = (p["w_dw"], p["b_dw"], p["ln_a_g"], p["ln_a_b"])
    a_out = _conv_prompt_call(a, *conv_args, tp)
    a_out, new_a_s = _conv_sample_call(hist_p, a, *conv_args, a_out, tp)

    o_b = _attn_prompt_call(q, k, v, biases["prompt"], tp)
    o_b = _attn_sample_call(q, k, v, *caches, layer, *biases["sample"], o_b, tp)

    c_out = _gate_call(vn, cu, p["w_sp"], p["b_full"], CHUNK, 0, n // CHUNK)
    c_out = _gate_call(vn, cu, p["w_sp_s"], p["b_full_s"], TT, tp // TT, 1, sub=3, big=c_out)

    h, xs3, aux, cnt_tiles = _mix_call(a_out, o_b, c_out, x, p["wa"], p["wb"], p["wc"], p["b_out"],
                                       p["ln1_g"], p["ln1_b"], p["w_router_hi"], p["w_router_lo"],
                                       p["b_router"])
    expert_tabs, n_used, combine_tabs = _routing_tables(cnt_tiles, n)
    y3 = _expert_call(expert_tabs, n_used, xs3, *experts, layer)
    last = layer == DEPTH - 1
    y = _combine_call(*combine_tabs, y3, h, aux, p["ln2_g"], p["ln2_b"], tp=tp if last else None)
    return y, a, k, v, vn, new_a_s


def kernel(x_prompt, x_sample, state_a_conv, cache_b_k, cache_b_v, w_in, b_in, w_dw, b_dw, ln_a_g, ln_a_b, ln_c_g, ln_c_b, w_sp, b_sp, w_out, b_out, ln1_g, ln1_b, w_router, b_router, w1, b1, w2, b2, ln2_g, ln2_b):
    bp, tp, _ = x_prompt.shape
    bs, ts, _ = x_sample.shape
    n_cache = cache_b_k.shape[2]
    assert bp == 1 and ts == DEC_SEQ and bs * ts == TT and tp % WIN_MAX == 0
    assert n_cache == WIN_MAX
    n = tp + bs * ts
    keep = min(WIN_MAX, tp)
    hist = CONV_W - 1

    weights = (w_in, b_in, w_dw, b_dw, ln_a_g, ln_a_b, ln_c_g, ln_c_b, w_sp, b_sp, w_out, b_out,
               ln1_g, ln1_b, w_router, b_router, ln2_g, ln2_b)
    biases = {"prompt": [_prompt_bias(d) for d in DILATIONS], "sample": _sample_bias(n_cache)}
    experts = (w1.reshape(DEPTH * N_EXPERTS, D_MODEL, 2 * D_FF), b1.reshape(DEPTH * N_EXPERTS, 1, 2 * D_FF),
               w2.reshape(DEPTH * N_EXPERTS, D_FF, D_MODEL), b2.reshape(DEPTH * N_EXPERTS, 1, D_MODEL))
    to_feature_major = lambda c: jnp.transpose(c, (0, 1, 3, 4, 2)).reshape(DEPTH, bs, C_B, n_cache)
    caches = (to_feature_major(cache_b_k), to_feature_major(cache_b_v))
    heads = lambda a: jnp.transpose(a, (1, 0, 2)).reshape(a.shape[1], N_HEADS_B, HEAD_DIM)

    x = jnp.concatenate([x_prompt[0], x_sample.reshape(bs * ts, D_MODEL)], axis=0)
    outs = {name: [] for name in ("a_p", "a_s", "k_p", "v_p", "k_s", "v_s", "c_s")}
    for l in range(DEPTH):
        p = _layer_params(l, *weights)
        hist_p = jnp.pad(state_a_conv[l], ((0, 0), (HALO - hist, 0), (0, SEG - C_A)))
        x, a, k, v, vn, new_a_s = _layer(x, p, hist_p, caches, experts, biases, tp, l)
        outs["a_p"].append(a[tp - hist:tp, :C_A][None])
        outs["a_s"].append(new_a_s[:, HALO - hist:, :C_A])
        outs["k_p"].append(heads(k[:, tp - keep:tp])[None])
        outs["v_p"].append(heads(v[:, tp - keep:tp])[None])
        outs["k_s"].append(heads(k[:, tp:]).reshape(bs, ts, N_HEADS_B, HEAD_DIM))
        outs["v_s"].append(heads(v[:, tp:]).reshape(bs, ts, N_HEADS_B, HEAD_DIM))
        outs["c_s"].append(vn[tp:, :C_C].reshape(bs, ts, C_C))
    stack = lambda name: jnp.stack(outs[name])
    y_prompt, y_sample = x
    return (y_prompt[None], y_sample.reshape(bs, ts, D_MODEL), stack("a_p"), stack("a_s"),
            stack("k_p"), stack("v_p"), stack("k_s"), stack("v_s"), stack("c_s"))
```

```python
import functools
import math

import jax
import jax.numpy as jnp
from jax import lax
from jax.experimental import pallas as pl
from jax.experimental.pallas import tpu as pltpu

F32 = jnp.float32
BF16 = jnp.bfloat16

D_MODEL = 1024
HEAD_DIM = 64
N_HEADS_B = 6
C_B = N_HEADS_B * HEAD_DIM
C_A = 320
C_C = 320
N_GROUPS_C = 5
CONV_W = 31
DILATIONS = (1, 4, 16)
BAND = 128
WIN_MAX = 2048
CHUNK = 128
N_EXPERTS = 32
TOP_K = 4
D_FF = 1024
SWIGLU_LIMIT = 7.0
SWIGLU_ALPHA = 1.702
LN_EPS = 1e-5
DEPTH = 2
DN_ALPHA = (2.0 * DEPTH) ** 0.25
DEC_SEQ = 8

LANES = 128
SUBLANES = 8
SEG = 384
N_SEG = 7
N_PAIRS = C_B // LANES
ROW_TILES = D_MODEL // LANES
assert ROW_TILES == SUBLANES

TT = 256
TM = 256
NEG = -1e30
VMEM_LIMIT = 48 * 1024 * 1024


def _params(n_axes=1):
    return pltpu.CompilerParams(dimension_semantics=("arbitrary",) * n_axes,
                                vmem_limit_bytes=VMEM_LIMIT)


def _full(a):
    nd = a.ndim
    return pl.BlockSpec(a.shape, lambda *_: (0,) * nd)


def _ln_valid(x, g, b, n_valid):
    col = lax.broadcasted_iota(jnp.int32, x.shape, 1)
    ok = col < n_valid
    mu = jnp.sum(jnp.where(ok, x, 0.0), axis=-1, keepdims=True) / n_valid
    xc = jnp.where(ok, x - mu, 0.0)
    var = jnp.sum(xc * xc, axis=-1, keepdims=True) / n_valid
    return xc * lax.rsqrt(var + LN_EPS) * g + b


def _ln_full(x, g, b):
    mu = jnp.mean(x, axis=-1, keepdims=True)
    xc = x - mu
    var = jnp.mean(xc * xc, axis=-1, keepdims=True)
    return xc * lax.rsqrt(var + LN_EPS) * g + b


def _proj_kernel(x_ref, w_ref, b_ref, lcg_ref, lcb_ref,
                 q_ref, k_ref, v_ref, a_ref, cu_ref, vn_ref):
    xb = x_ref[...].astype(BF16)

    def seg(i):
        lo = i * SEG
        return (jnp.dot(xb, w_ref[:, lo:lo + SEG], preferred_element_type=F32)
                + b_ref[:, lo:lo + SEG])

    for i, ref in enumerate((q_ref, k_ref, v_ref)):
        qkv = seg(i)
        for pair in range(N_PAIRS):
            ref[pair] = qkv[:, pair * LANES:(pair + 1) * LANES]
    a_ref[...] = seg(3) * jax.nn.sigmoid(seg(4))
    cu_ref[...] = seg(5)
    vn_ref[...] = _ln_valid(seg(6), lcg_ref[...], lcb_ref[...], C_C)


def _proj_call(x, w_in_p, b_in_p, lcg, lcb):
    n = x.shape[0]
    row = lambda w: pl.BlockSpec((TT, w), lambda i: (i, 0))
    heads = pl.BlockSpec((N_PAIRS, TT, LANES), lambda i: (0, i, 0))
    out = jax.ShapeDtypeStruct((n, SEG), F32)
    out3 = jax.ShapeDtypeStruct((N_PAIRS, n, LANES), F32)
    return pl.pallas_call(
        _proj_kernel, grid=(n // TT,),
        in_specs=[row(D_MODEL), _full(w_in_p), _full(b_in_p), _full(lcg), _full(lcb)],
        out_specs=[heads] * 3 + [row(SEG)] * 3, out_shape=[out3] * 3 + [out] * 3,
        compiler_params=_params(), name="in_proj")(x, w_in_p, b_in_p, lcg, lcb)


HALO = 32
CONV_CHUNK = 32


def _conv_tail(acc, bdw_ref, g_ref, b_ref):
    return jax.nn.silu(_ln_valid(acc + bdw_ref[...], g_ref[...], b_ref[...], C_A))


def _conv_prompt_kernel(halo_ref, a_ref, w_ref, bdw_ref, g_ref, b_ref, o_ref, buf, shifted, *, n_prompt):
    i = pl.program_id(0)

    @pl.when(i < n_prompt)
    def _():
        buf[0:HALO, :] = jnp.where(i > 0, halo_ref[...], 0.0)
        buf[HALO:, :] = a_ref[...]
        n_rows = HALO + TT - SUBLANES
        for sh in range(1, SUBLANES):
            shifted[sh - 1, 0:n_rows, :] = buf[sh:sh + n_rows, :]
        base = HALO - (CONV_W - 1)
        for r in range(0, TT, CONV_CHUNK):
            acc = jnp.zeros((CONV_CHUNK, SEG), F32)
            for j in range(CONV_W):
                tiles, sh = divmod(base + j, SUBLANES)
                src = buf if sh == 0 else shifted.at[sh - 1]
                lo = r + tiles * SUBLANES
                acc = acc + w_ref[j:j + 1, :] * src[lo:lo + CONV_CHUNK, :]
            o_ref[r:r + CONV_CHUNK, :] = _conv_tail(acc, bdw_ref, g_ref, b_ref)

    @pl.when(i >= n_prompt)
    def _():
        o_ref[...] = jnp.zeros_like(o_ref)


def _conv_prompt_call(a, w_dw_p, bdw, g, b, tp):
    n = a.shape[0]
    per = TT // HALO
    halo = pl.BlockSpec((HALO, SEG), lambda i: (jnp.maximum(i * per - 1, 0), 0))
    row = pl.BlockSpec((TT, SEG), lambda i: (i, 0))
    return pl.pallas_call(
        functools.partial(_conv_prompt_kernel, n_prompt=tp // TT), grid=(n // TT,),
        in_specs=[halo, row, _full(w_dw_p), _full(bdw), _full(g), _full(b)],
        out_specs=row, out_shape=jax.ShapeDtypeStruct((n, SEG), F32),
        scratch_shapes=[pltpu.VMEM((HALO + TT, SEG), F32),
                        pltpu.VMEM((SUBLANES - 1, HALO + TT, SEG), F32)],
        compiler_params=_params(), name="conv_prompt")(a, a, w_dw_p, bdw, g, b)


SB = 8


def _conv_sample_kernel(hist_ref, a_ref, w_ref, bdw_ref, g_ref, b_ref, big_ref,
                        o_ref, newa_ref, buf):
    del big_ref
    bs = hist_ref.shape[0]
    buf[:, 0:HALO, :] = hist_ref[...]
    buf[:, HALO:, :] = a_ref[...].reshape(bs, DEC_SEQ, SEG)
    base = HALO - (CONV_W - 1)
    for s in range(0, bs, SB):
        acc = jnp.zeros((SB, DEC_SEQ, SEG), F32)
        for j in range(CONV_W):
            acc = acc + w_ref[j:j + 1, :][None] * buf[s:s + SB, base + j: base + j + DEC_SEQ, :]
        y = _conv_tail(acc.reshape(SB * DEC_SEQ, SEG), bdw_ref, g_ref, b_ref)
        o_ref[s * DEC_SEQ:(s + SB) * DEC_SEQ, :] = y
    newa_ref[...] = buf[:, DEC_SEQ:, :]


def _conv_sample_call(hist_p, a, w_dw_p, bdw, g, b, a_out, tp):
    bs = hist_p.shape[0]
    rows = bs * DEC_SEQ
    blk = pl.BlockSpec((rows, SEG), lambda i: (tp // rows, 0))
    return pl.pallas_call(
        _conv_sample_kernel, grid=(1,),
        in_specs=[_full(hist_p), blk, _full(w_dw_p), _full(bdw), _full(g), _full(b),
                  pl.BlockSpec(memory_space=pl.ANY)],
        out_specs=[blk, pl.BlockSpec((bs, HALO, SEG), lambda i: (0, 0, 0))],
        out_shape=[jax.ShapeDtypeStruct(a_out.shape, F32),
                   jax.ShapeDtypeStruct((bs, HALO, SEG), F32)],
        scratch_shapes=[pltpu.VMEM((bs, HALO + DEC_SEQ, SEG), F32)],
        input_output_aliases={6: 0},
        compiler_params=_params(), name="conv_sample")(hist_p, a, w_dw_p, bdw, g, b, a_out)


QB = 128
assert QB == BAND


SPAN = BAND * max(DILATIONS)
UNITS = SPAN // QB
UNIT_UNROLL = 8


def _attn_prompt_kernel(q_ref, kh_ref, kc_ref, vh_ref, vc_ref, b1_ref, b4_ref, b16_ref, o_ref,
                        kbuf, vbuf, oacc, lacc, *, n_prompt):
    is_prompt = pl.program_id(1) < n_prompt
    refs = (q_ref, kh_ref, kc_ref, vh_ref, vc_ref, (b1_ref, b4_ref, b16_ref), o_ref, kbuf, vbuf, oacc, lacc)
    pl.when(is_prompt)(functools.partial(_attn_prompt_span, refs))

    @pl.when(jnp.logical_not(is_prompt))
    def _():
        o_ref[...] = jnp.zeros_like(o_ref)


def _attn_prompt_span(refs):
    q_ref, kh_ref, kc_ref, vh_ref, vc_ref, bias_refs, o_ref, kbuf, vbuf, oacc, lacc = refs
    pair = pl.program_id(0)
    first_span = pl.program_id(1) == 0
    kbuf[0:SPAN, :] = kh_ref[0]
    kbuf[SPAN:, :] = kc_ref[0]
    vbuf[0:SPAN, :] = vh_ref[0]
    vbuf[SPAN:, :] = vc_ref[0]
    kcol = lax.broadcasted_iota(jnp.int32, (QB, 2 * QB), 1)
    lane = lax.broadcasted_iota(jnp.int32, (QB, LANES), 1)
    lo_half = lane < HEAD_DIM

    for dil, bias_ref in zip(DILATIONS, bias_refs):
        def unit(u, carry, dil=dil, bias_ref=bias_ref):
            sub = u // dil
            t0 = sub * (QB * dil) + (u - sub * dil)
            if dil == 1:
                rows_q = pl.ds(pl.multiple_of(t0, QB), QB)
                rows_k = pl.ds(pl.multiple_of(SPAN + t0 - QB, QB), 2 * QB)
            else:
                rows_q = pl.ds(t0, QB, stride=dil)
                rows_k = pl.ds(SPAN + t0 - QB * dil, 2 * QB, stride=dil)
            no_halo = jnp.logical_and(jnp.logical_and(first_span, sub == 0), kcol < QB)
            qp = q_ref[0, rows_q, :] * (HEAD_DIM ** -0.5)
            kk = kbuf[rows_k, :].astype(BF16)
            vv = vbuf[rows_k, :].astype(BF16)
            outs, lses = [], []
            for hh in range(2):
                keep = lo_half if hh == 0 else jnp.logical_not(lo_half)
                qm = jnp.where(keep, qp, 0.0).astype(BF16)
                s = lax.dot_general(qm, kk, (((1,), (1,)), ((), ())), preferred_element_type=F32)
                s = s + bias_ref[pair * 2 + hh]
                s = jnp.where(no_halo, NEG, s)
                m = jnp.max(s, axis=-1, keepdims=True)
                p = jnp.exp(s - m)
                l = jnp.sum(p, axis=-1, keepdims=True)
                pv = jnp.dot(p.astype(BF16), vv, preferred_element_type=F32)
                outs.append(pv / l)
                lses.append(m + jnp.log(l))
            o_new = jnp.where(lo_half, outs[0], outs[1])
            l_new = jnp.where(lo_half, lses[0], lses[1])
            if dil != 1:
                l_old = lacc[rows_q, :]
                mx = jnp.maximum(l_old, l_new)
                w_old = jnp.exp(l_old - mx)
                w_new = jnp.exp(l_new - mx)
                tot = w_old + w_new
                o_new = (w_old * oacc[rows_q, :] + w_new * o_new) / tot
                l_new = mx + jnp.log(tot)
            oacc[rows_q, :] = o_new
            lacc[rows_q, :] = l_new
            return carry

        lax.fori_loop(0, UNITS, unit, 0, unroll=UNIT_UNROLL)
    o_ref[0] = oacc[...]


def _attn_prompt_call(q, k, v, biases, tp):
    n = q.shape[1]
    cur = pl.BlockSpec((1, SPAN, LANES), lambda p, m: (p, m, 0))
    halo = pl.BlockSpec((1, SPAN, LANES), lambda p, m: (p, jnp.maximum(m - 1, 0), 0))
    scratch = [pltpu.VMEM((2 * SPAN, LANES), F32), pltpu.VMEM((2 * SPAN, LANES), F32),
               pltpu.VMEM((SPAN, LANES), F32), pltpu.VMEM((SPAN, LANES), F32)]
    return pl.pallas_call(
        functools.partial(_attn_prompt_kernel, n_prompt=tp // SPAN),
        grid=(N_PAIRS, pl.cdiv(n, SPAN)),
        in_specs=[cur, halo, cur, halo, cur] + [_full(b) for b in biases],
        out_specs=cur, out_shape=jax.ShapeDtypeStruct((N_PAIRS, n, LANES), F32),
        scratch_shapes=scratch, compiler_params=_params(2), name="attn_prompt")(q, k, k, v, v, *biases)


def _alibi_slopes():
    return [2.0 ** (-8.0 * (h + 1) / N_HEADS_B) for h in range(N_HEADS_B)]


def _prompt_bias(dil):
    i = jnp.arange(QB)[:, None]
    j = jnp.arange(2 * QB)[None, :]
    rel = i + QB - j
    ok = jnp.logical_and(rel >= 0, rel <= BAND)
    slopes = jnp.asarray(_alibi_slopes(), F32)[:, None, None]
    pen = -slopes * (rel * dil).astype(F32)[None]
    return jnp.where(ok[None], pen, NEG).astype(F32)


def _branch_multiplicity(dist):
    cnt = jnp.zeros(dist.shape, jnp.int32)
    for dil in DILATIONS:
        cnt = cnt + jnp.logical_and(dist % dil == 0, dist <= BAND * dil).astype(jnp.int32)
    return cnt


def _sample_bias(n_cache):
    t = jnp.arange(DEC_SEQ)[:, None]
    dist_c = n_cache + t - jnp.arange(n_cache)[None, :]
    dist_n = t - jnp.arange(LANES)[None, :]
    ok_n = jnp.logical_and(dist_n >= 0, jnp.arange(LANES)[None, :] < DEC_SEQ)
    slopes = jnp.asarray(_alibi_slopes(), F32)[:, None, None]

    def bias(dist, ok):
        mult = _branch_multiplicity(jnp.maximum(dist, 0))
        ok = jnp.logical_and(ok, mult > 0)
        val = -slopes * dist.astype(F32)[None] + jnp.log(jnp.maximum(mult, 1).astype(F32))[None]
        return jnp.where(ok[None], val, NEG).astype(F32)

    bc = bias(dist_c, jnp.ones(dist_c.shape, bool)).reshape(N_HEADS_B * DEC_SEQ, n_cache)
    bn = bias(dist_n, ok_n).reshape(N_HEADS_B * DEC_SEQ, LANES)
    return bc, bn


def _attn_sample_kernel(q_ref, kn_ref, vn_ref, kc_ref, vc_ref, bc_ref, bn_ref, big_ref, o_ref):
    del big_ref
    rows = N_HEADS_B * DEC_SEQ
    wide = lambda ref: jnp.concatenate([ref[pair] for pair in range(N_PAIRS)], axis=1)
    q = wide(q_ref) * (HEAD_DIM ** -0.5)
    qrep = jnp.concatenate([q] * N_HEADS_B, axis=0)
    rh = lax.shift_right_logical(lax.broadcasted_iota(jnp.int32, (rows, SEG), 0), 3)
    ch = lax.shift_right_logical(lax.broadcasted_iota(jnp.int32, (rows, SEG), 1), 6)
    own = rh == ch
    qm = jnp.where(own, qrep, 0.0).astype(BF16)
    pad = jnp.zeros((LANES - DEC_SEQ, SEG), F32)
    kn = jnp.concatenate([wide(kn_ref), pad], axis=0).astype(BF16)
    vn = jnp.concatenate([wide(vn_ref), pad], axis=0).astype(BF16)
    nt = (((1,), (1,)), ((), ()))
    s_c = jnp.dot(qm, kc_ref[0, 0].astype(BF16), preferred_element_type=F32) + bc_ref[...]
    s_n = lax.dot_general(qm, kn, nt, preferred_element_type=F32) + bn_ref[...]
    m = jnp.maximum(jnp.max(s_c, axis=-1, keepdims=True), jnp.max(s_n, axis=-1, keepdims=True))
    p_c = jnp.exp(s_c - m)
    p_n = jnp.exp(s_n - m)
    l = jnp.sum(p_c, axis=-1, keepdims=True) + jnp.sum(p_n, axis=-1, keepdims=True)
    r = (lax.dot_general(p_c.astype(BF16), vc_ref[0, 0].astype(BF16), nt, preferred_element_type=F32)
         + jnp.dot(p_n.astype(BF16), vn, preferred_element_type=F32)) / l
    r = jnp.where(own, r, 0.0)
    o = r[0:DEC_SEQ]
    for h in range(1, N_HEADS_B):
        o = o + r[h * DEC_SEQ:(h + 1) * DEC_SEQ]
    for pair in range(N_PAIRS):
        o_ref[pair] = o[:, pair * LANES:(pair + 1) * LANES]


def _attn_sample_call(q, k, v, cache_kt, cache_vt, layer, bc, bn, o_big, tp):
    bs, n_cache = cache_kt.shape[1], cache_kt.shape[3]
    new = pl.BlockSpec((N_PAIRS, DEC_SEQ, LANES), lambda b: (0, tp // DEC_SEQ + b, 0))
    cache = pl.BlockSpec((1, 1, C_B, n_cache), lambda b: (layer, b, 0, 0))
    return pl.pallas_call(
        _attn_sample_kernel, grid=(bs,),
        in_specs=[new, new, new, cache, cache, _full(bc), _full(bn),
                  pl.BlockSpec(memory_space=pl.ANY)],
        out_specs=new, out_shape=jax.ShapeDtypeStruct(o_big.shape, F32),
        input_output_aliases={7: 0},
        compiler_params=_params(), name="attn_sample")(q, k, v, cache_kt, cache_vt, bc, bn, o_big)


def _gate_kernel(*refs, rows, sub, aliased):
    if aliased:
        vn_ref, cu_ref, w_ref, bf_ref, big_ref, o_ref = refs
        del big_ref
    else:
        vn_ref, cu_ref, w_ref, bf_ref, o_ref = refs
    r = lax.broadcasted_iota(jnp.int32, (rows, rows), 0)
    c = lax.broadcasted_iota(jnp.int32, (rows, rows), 1)
    causal = c <= r
    if sub is not None:
        causal = jnp.logical_and(causal, lax.shift_right_logical(r, sub) == lax.shift_right_logical(c, sub))
    lane = lax.broadcasted_iota(jnp.int32, (rows, LANES), 1)
    lo_half = lane < HEAD_DIM
    for pair in range(SEG // LANES):
        cs = slice(pair * LANES, (pair + 1) * LANES)
        vp = vn_ref[:, cs]
        acc = jnp.zeros((rows, LANES), F32)
        for hh in range(2):
            g = pair * 2 + hh
            if g >= N_GROUPS_C:
                continue
            keep = lo_half if hh == 0 else jnp.logical_not(lo_half)
            wm = jnp.where(causal, w_ref[g], 0.0).astype(BF16)
            vm = jnp.where(keep, vp, 0.0).astype(BF16)
            acc = acc + jnp.dot(wm, vm, preferred_element_type=F32)
        o_ref[:, cs] = cu_ref[:, cs] * (acc + bf_ref[:, cs])


def _gate_call(vn, cu, w, bfull, rows, first_block, n_blocks, sub=None, big=None):
    blk = pl.BlockSpec((rows, SEG), lambda i: (first_block + i, 0))
    args = [vn, cu, w, bfull]
    specs = [blk, blk, _full(w), _full(bfull)]
    kwargs = {}
    if big is not None:
        args.append(big)
        specs.append(pl.BlockSpec(memory_space=pl.ANY))
        kwargs["input_output_aliases"] = {4: 0}
    return pl.pallas_call(
        functools.partial(_gate_kernel, rows=rows, sub=sub, aliased=big is not None),
        grid=(n_blocks,), in_specs=specs, out_specs=blk,
        out_shape=jax.ShapeDtypeStruct(vn.shape, F32),
        compiler_params=_params(), name="spatial_gate", **kwargs)(*args)


def _mix_kernel(a_ref, o_ref, c_ref, x_ref, wo_ref, bo_ref, g1_ref, b1_ref,
                wr_ref, br_ref, h_ref, xs3_ref, aux_ref, cnt_ref):
    mixed = jnp.concatenate([a_ref[...]] + [o_ref[pair] for pair in range(N_PAIRS)] + [c_ref[...]], axis=1)
    mix = jnp.dot(mixed.astype(BF16), wo_ref[...], preferred_element_type=F32) + bo_ref[...]
    h = _ln_full(DN_ALPHA * x_ref[...] + mix, g1_ref[...], b1_ref[...])
    h_ref[...] = h

    hb = h.astype(BF16)
    h_lo = (h - hb.astype(F32)).astype(BF16)
    both = jnp.dot(hb, wr_ref[...], preferred_element_type=F32)
    logits = (both[:, :LANES] + both[:, LANES:]
              + jnp.dot(h_lo, wr_ref[:, :LANES], preferred_element_type=F32) + br_ref[...])
    lane = lax.broadcasted_iota(jnp.int32, (TT, LANES), 1)
    lane_f = lane.astype(F32)
    vals, idxs, sels = [], [], []
    cur = logits
    for _ in range(TOP_K):
        m = jnp.max(cur, axis=-1, keepdims=True)
        idx = jnp.min(jnp.where(cur == m, lane_f, float(LANES)), axis=-1, keepdims=True)
        sel = lane_f == idx
        vals.append(m)
        idxs.append(idx)
        sels.append(sel)
        cur = jnp.where(sel, -jnp.inf, cur)
    exps = [jnp.exp(v - vals[0]) for v in vals]
    den = exps[0] + exps[1] + exps[2] + exps[3]

    onehot = jnp.zeros((TT, LANES), F32)
    for sel in sels:
        onehot = onehot + sel.astype(F32)
    r = lax.broadcasted_iota(jnp.int32, (TT, TT), 0)
    c = lax.broadcasted_iota(jnp.int32, (TT, TT), 1)
    below = (c < r).astype(BF16)
    earlier = jnp.dot(below, onehot.astype(BF16), preferred_element_type=F32)
    cnt = jnp.broadcast_to(jnp.sum(onehot, axis=0, keepdims=True), (SUBLANES, LANES))
    er = lax.broadcasted_iota(jnp.int32, (LANES, LANES), 0)
    ec = lax.broadcasted_iota(jnp.int32, (LANES, LANES), 1)
    off = jnp.dot(cnt.astype(BF16), (er < ec).astype(BF16), preferred_element_type=F32)[0:1]
    place = earlier + off
    aux = jnp.zeros((TT, LANES), F32)
    rows = []
    for k in range(TOP_K):
        row = jnp.sum(jnp.where(sels[k], place, 0.0), axis=-1, keepdims=True)
        rows.append(row)
        aux = aux + jnp.where(lane == k, row, 0.0) + jnp.where(lane == TOP_K + k, exps[k] / den, 0.0)
    aux_ref[...] = aux
    cnt_ref[0] = cnt.astype(jnp.int32)

    dest = lax.broadcasted_iota(jnp.int32, (TT, TOP_K * TT), 1).astype(F32)
    disp = jnp.zeros((TT, TOP_K * TT), F32)
    for k in range(TOP_K):
        disp = disp + (dest == rows[k]).astype(F32)
    xs = lax.dot_general(disp.astype(BF16), hb, (((0,), (0,)), ((), ())), preferred_element_type=F32)
    for s in range(ROW_TILES):
        xs3_ref[pl.ds(s, TOP_K * TT, stride=SUBLANES), :] = xs[:, s * LANES:(s + 1) * LANES]


def _mix_call(a_out, o_b, c_out, x, wo, bo, g1, b1, wr, br):
    n = x.shape[0]
    row = lambda w: pl.BlockSpec((TT, w), lambda i: (i, 0))
    xs3 = pl.BlockSpec((TOP_K * TT * SUBLANES, LANES), lambda i: (i, 0))
    cnt = pl.BlockSpec((1, SUBLANES, LANES), lambda i: (i, 0, 0))
    return pl.pallas_call(
        _mix_kernel, grid=(n // TT,),
        in_specs=[row(SEG), pl.BlockSpec((N_PAIRS, TT, LANES), lambda i: (0, i, 0)), row(SEG), row(D_MODEL),
                  _full(wo), _full(bo), _full(g1), _full(b1), _full(wr), _full(br)],
        out_specs=[row(D_MODEL), xs3, row(LANES), cnt],
        out_shape=[jax.ShapeDtypeStruct((n, D_MODEL), F32),
                   jax.ShapeDtypeStruct((n * TOP_K * SUBLANES, LANES), F32),
                   jax.ShapeDtypeStruct((n, LANES), F32),
                   jax.ShapeDtypeStruct((n // TT, SUBLANES, LANES), jnp.int32)],
        compiler_params=_params(), name="mix_ln_router")(
            a_out, o_b, c_out, x, wo, bo, g1, b1, wr, br)


def _rows(start_row, n_rows):
    return pl.ds(pl.multiple_of(start_row * SUBLANES, SUBLANES), n_rows * SUBLANES)


def _expert_gather(tabs, blk, slot, xs3_hbm, buf, sem):
    blk_e, blk_s0, j_lo, j_hi, cnt, off, cum, tot = tabs
    e = blk_e[blk]
    s0 = blk_s0[blk]
    base = slot * TM

    @pl.when(tot[e] - s0 < TM)
    def _():
        buf[_rows(base, TM), :] = jnp.zeros((TM * SUBLANES, LANES), F32)

    def body(j, carry):
        run = cum[j * N_EXPERTS + e]
        lo = jnp.maximum(run, s0)
        hi = jnp.minimum(run + cnt[j * N_EXPERTS + e], s0 + TM)

        @pl.when(hi > lo)
        def _():
            src = j * (TOP_K * TT) + off[j * N_EXPERTS + e] + (lo - run)
            pltpu.make_async_copy(xs3_hbm.at[_rows(src, hi - lo)],
                                  buf.at[_rows(base + lo - s0, hi - lo)], sem.at[slot]).start()
        return carry

    lax.fori_loop(j_lo[blk], j_hi[blk], body, 0)


def _expert_gather_wait(tabs, blk, slot, xs3_hbm, buf, sem):
    blk_e, blk_s0, _, _, _, _, _, tot = tabs
    valid = jnp.minimum(tot[blk_e[blk]] - blk_s0[blk], TM)
    pltpu.make_async_copy(xs3_hbm.at[_rows(0, valid)], buf.at[_rows(slot * TM, valid)],
                          sem.at[slot]).wait()


def _expert_kernel(blk_e, blk_s0, j_lo, j_hi, cnt, off, cum, tot, n_used_ref,
                   xs3_hbm, w1_ref, b1_ref, w2_ref, b2_ref, y3_ref, buf, xb, w1b, w2b, sem):
    tabs = (blk_e, blk_s0, j_lo, j_hi, cnt, off, cum, tot)
    b = pl.program_id(0)
    n_used = n_used_ref[0]
    slot = lax.rem(b, 2)
    used = b < n_used

    @pl.when(b == 0)
    def _():
        _expert_gather(tabs, 0, 0, xs3_hbm, buf, sem)

    @pl.when(used)
    def _():
        _expert_gather_wait(tabs, b, slot, xs3_hbm, buf, sem)

    @pl.when(b + 1 < n_used)
    def _():
        _expert_gather(tabs, b + 1, 1 - slot, xs3_hbm, buf, sem)

    new_expert = jnp.logical_or(b == 0, blk_e[b] != blk_e[jnp.maximum(b - 1, 0)])

    @pl.when(jnp.logical_and(used, new_expert))
    def _():
        w1b[...] = w1_ref[0].astype(BF16)
        w2b[...] = w2_ref[0].astype(BF16)

    @pl.when(used)
    def _():
        base = slot * (TM * SUBLANES)
        for s in range(ROW_TILES):
            xb[:, s * LANES:(s + 1) * LANES] = buf[pl.ds(base + s, TM, stride=SUBLANES), :].astype(BF16)
        gu = jnp.dot(xb[...], w1b[...], preferred_element_type=F32) + b1_ref[0]
        gate = jnp.minimum(gu[:, :D_FF], SWIGLU_LIMIT)
        up = jnp.clip(gu[:, D_FF:], -SWIGLU_LIMIT, SWIGLU_LIMIT)
        hh = (up + 1.0) * (gate * jax.nn.sigmoid(SWIGLU_ALPHA * gate))
        y = jnp.dot(hh.astype(BF16), w2b[...], preferred_element_type=F32) + b2_ref[0]
        for s in range(ROW_TILES):
            y3_ref[pl.ds(s, TM, stride=SUBLANES), :] = y[:, s * LANES:(s + 1) * LANES]

    @pl.when(jnp.logical_not(used))
    def _():
        y3_ref[...] = jnp.zeros_like(y3_ref)


def _expert_call(tabs, n_used, xs3, w1, b1, w2, b2, layer):
    n_blocks = tabs[0].shape[0]
    by_expert = lambda shape: pl.BlockSpec(shape, lambda b, e, *_: (layer * N_EXPERTS + e[b], 0, 0))
    grid_spec = pltpu.PrefetchScalarGridSpec(
        num_scalar_prefetch=len(tabs) + 1, grid=(n_blocks,),
        in_specs=[pl.BlockSpec(memory_space=pl.ANY),
                  by_expert((1, D_MODEL, 2 * D_FF)), by_expert((1, 1, 2 * D_FF)),
                  by_expert((1, D_FF, D_MODEL)), by_expert((1, 1, D_MODEL))],
        out_specs=pl.BlockSpec((TM * SUBLANES, LANES), lambda b, *_: (b, 0)),
        scratch_shapes=[pltpu.VMEM((2 * TM * SUBLANES, LANES), F32),
                        pltpu.VMEM((TM, D_MODEL), BF16),
                        pltpu.VMEM((D_MODEL, 2 * D_FF), BF16),
                        pltpu.VMEM((D_FF, D_MODEL), BF16),
                        pltpu.SemaphoreType.DMA((2,))])
    return pl.pallas_call(
        _expert_kernel, grid_spec=grid_spec,
        out_shape=jax.ShapeDtypeStruct((n_blocks * TM * SUBLANES, LANES), F32),
        compiler_params=_params(), name="expert_ffn")(*tabs, n_used, xs3, w1, b1, w2, b2)


PAIRS = TOP_K * TT


def _combine_gather(cnt, off, src, tile, slot, y3_hbm, buf, sem):
    def body(e, carry):
        n = cnt[tile * N_EXPERTS + e]

        @pl.when(n > 0)
        def _():
            pltpu.make_async_copy(y3_hbm.at[_rows(src[tile * N_EXPERTS + e], n)],
                                  buf.at[_rows(slot * PAIRS + off[tile * N_EXPERTS + e], n)],
                                  sem.at[slot]).start()
        return carry

    lax.fori_loop(0, N_EXPERTS, body, 0)


def _combine_kernel(cnt, off, src, y3_hbm, h_ref, aux_ref, g2_ref, b2_ref, *rest, n_prompt):
    if n_prompt is None:
        o_ref, buf, ys, sem = rest
    else:
        op_ref, os_ref, buf, ys, sem = rest
    i = pl.program_id(0)
    n_steps = pl.num_programs(0)
    slot = lax.rem(i, 2)

    @pl.when(i == 0)
    def _():
        _combine_gather(cnt, off, src, 0, 0, y3_hbm, buf, sem)

    pltpu.make_async_copy(y3_hbm.at[_rows(0, PAIRS)], buf.at[_rows(slot * PAIRS, PAIRS)],
                          sem.at[slot]).wait()

    @pl.when(i + 1 < n_steps)
    def _():
        _combine_gather(cnt, off, src, i + 1, 1 - slot, y3_hbm, buf, sem)

    base = slot * (PAIRS * SUBLANES)
    for s in range(ROW_TILES):
        ys[:, s * LANES:(s + 1) * LANES] = buf[pl.ds(base + s, PAIRS, stride=SUBLANES), :].astype(BF16)
    aux = aux_ref[...]
    dest = lax.broadcasted_iota(jnp.int32, (TT, PAIRS), 1).astype(F32)
    weights = jnp.zeros((TT, PAIRS), F32)
    for k in range(TOP_K):
        weights = weights + jnp.where(dest == aux[:, k:k + 1], aux[:, TOP_K + k:TOP_K + k + 1], 0.0)
    moe = jnp.dot(weights.astype(BF16), ys[...], preferred_element_type=F32)
    y = _ln_full(DN_ALPHA * h_ref[...] + moe, g2_ref[...], b2_ref[...])
    if n_prompt is None:
        o_ref[...] = y
    else:
        @pl.when(i < n_prompt)
        def _():
            op_ref[...] = y

        @pl.when(i >= n_prompt)
        def _():
            os_ref[...] = y


def _combine_call(cnt, off, src, y3, h, aux, g2, b2, tp=None):
    n = h.shape[0]
    row = lambda w: pl.BlockSpec((TT, w), lambda i, *_: (i, 0))
    if tp is None:
        n_prompt = None
        out_specs = row(D_MODEL)
        out_shape = jax.ShapeDtypeStruct((n, D_MODEL), F32)
    else:
        n_prompt = tp // TT
        out_specs = [pl.BlockSpec((TT, D_MODEL), lambda i, *_: (jnp.minimum(i, n_prompt - 1), 0)),
                     pl.BlockSpec((TT, D_MODEL), lambda i, *_: (jnp.maximum(i - n_prompt, 0), 0))]
        out_shape = [jax.ShapeDtypeStruct((tp, D_MODEL), F32),
                     jax.ShapeDtypeStruct((n - tp, D_MODEL), F32)]
    grid_spec = pltpu.PrefetchScalarGridSpec(
        num_scalar_prefetch=3, grid=(n // TT,),
        in_specs=[pl.BlockSpec(memory_space=pl.ANY), row(D_MODEL), row(LANES),
                  pl.BlockSpec(g2.shape, lambda i, *_: (0, 0)), pl.BlockSpec(b2.shape, lambda i, *_: (0, 0))],
        out_specs=out_specs,
        scratch_shapes=[pltpu.VMEM((2 * PAIRS * SUBLANES, LANES), F32),
                        pltpu.VMEM((PAIRS, D_MODEL), BF16),
                        pltpu.SemaphoreType.DMA((2,))])
    return pl.pallas_call(
        functools.partial(_combine_kernel, n_prompt=n_prompt), grid_spec=grid_spec, out_shape=out_shape,
        compiler_params=_params(), name="combine_ln")(cnt, off, src, y3, h, aux, g2, b2)


def _routing_tables(cnt_tiles, n):
    cnt = cnt_tiles[:, 0, :N_EXPERTS]
    off = jnp.cumsum(cnt, axis=1) - cnt
    cum = jnp.cumsum(cnt, axis=0) - cnt
    tot = jnp.sum(cnt, axis=0)
    padded = (tot + TM - 1) // TM * TM
    pends = jnp.cumsum(padded)
    pstart = pends - padded
    n_blocks = n * TOP_K // TM + N_EXPERTS
    blk_start = jnp.arange(n_blocks, dtype=jnp.int32) * TM
    blk_e = jnp.minimum(jnp.sum((pends[None, :] <= blk_start[:, None]).astype(jnp.int32), axis=1),
                        N_EXPERTS - 1)
    blk_s0 = blk_start - pstart[blk_e]
    run_lo = cum[:, blk_e].T
    run_hi = run_lo + cnt[:, blk_e].T
    j_lo = jnp.sum((run_hi <= blk_s0[:, None]).astype(jnp.int32), axis=1)
    j_hi = jnp.sum((run_lo < blk_s0[:, None] + TM).astype(jnp.int32), axis=1)
    n_used = (pends[-1:] // TM).astype(jnp.int32)
    flat = lambda a: a.reshape(-1).astype(jnp.int32)
    expert_tabs = (flat(blk_e), flat(blk_s0), flat(j_lo), flat(j_hi), flat(cnt), flat(off), flat(cum), flat(tot))
    combine_tabs = (flat(cnt), flat(off), flat(pstart[None, :] + cum))
    return expert_tabs, n_used, combine_tabs


def _pad_cols(a, width):
    return jnp.pad(a, [(0, 0)] * (a.ndim - 1) + [(0, width - a.shape[-1])])


def _layer_params(l, w_in, b_in, w_dw, b_dw, ln_a_g, ln_a_b, ln_c_g, ln_c_b, w_sp, b_sp,
                  w_out, b_out, ln1_g, ln1_b, w_router, b_router, ln2_g, ln2_b):
    bounds = [0, C_B, 2 * C_B, 3 * C_B, 3 * C_B + C_A, 3 * C_B + 2 * C_A, 3 * C_B + 2 * C_A + C_C,
              3 * C_B + 2 * C_A + 2 * C_C]
    w_segs = [_pad_cols(w_in[l][:, bounds[i]:bounds[i + 1]], SEG) for i in range(N_SEG)]
    b_segs = [_pad_cols(b_in[l][bounds[i]:bounds[i + 1]], SEG) for i in range(N_SEG)]
    row = lambda a: _pad_cols(a, SEG)[None, :]
    wo = w_out[l]
    pad_rows = lambda a: jnp.pad(a, ((0, SEG - a.shape[0]), (0, 0)))
    bsp = b_sp[l]
    bfull = _pad_cols(jnp.repeat(bsp.T, HEAD_DIM, axis=1), SEG)
    wr = _pad_cols(w_router[l], LANES)
    wr_hi = wr.astype(BF16)
    return dict(
        w_in=jnp.concatenate(w_segs, axis=1).astype(BF16),
        b_in=jnp.concatenate(b_segs)[None, :],
        w_dw=jnp.pad(w_dw[l], ((0, HALO - CONV_W), (0, SEG - C_A))),
        b_dw=row(b_dw[l]), ln_a_g=row(ln_a_g[l]), ln_a_b=row(ln_a_b[l]),
        ln_c_g=row(ln_c_g[l]), ln_c_b=row(ln_c_b[l]),
        w_sp=w_sp[l], b_full=bfull,
        w_sp_s=jnp.tile(w_sp[l][:, :DEC_SEQ, :DEC_SEQ], (1, TT // DEC_SEQ, TT // DEC_SEQ)),
        b_full_s=jnp.tile(bfull[:DEC_SEQ], (TT // DEC_SEQ, 1)),
        w_out=jnp.concatenate([pad_rows(wo[:C_A]), wo[C_A:C_A + C_B], pad_rows(wo[C_A + C_B:])]).astype(BF16),
        b_out=b_out[l][None, :],
        ln1_g=ln1_g[l][None, :], ln1_b=ln1_b[l][None, :],
        w_router=jnp.concatenate([wr_hi, (wr - wr_hi.astype(F32)).astype(BF16)], axis=1),
        b_router=jnp.concatenate([b_router[l], jnp.full((LANES - N_EXPERTS,), NEG, F32)])[None, :],
        ln2_g=ln2_g[l][None, :], ln2_b=ln2_b[l][None, :])


def _layer(x, p, hist_p, caches, experts, biases, tp, layer):
    n = x.shape[0]
    q, k, v, a, cu, vn = _proj_call(x, p["w_in"], p["b_in"], p["ln_c_g"], p["ln_c_b"])

    conv_args = (p["w_dw"], p["b_dw"], p["ln_a_g"], p["ln_a_b"])
    a_out = _conv_prompt_call(a, *conv_args, tp)
    a_out, new_a_s = _conv_sample_call(hist_p, a, *conv_args, a_out, tp)

    o_b = _attn_prompt_call(q, k, v, biases["prompt"], tp)
    o_b = _attn_sample_call(q, k, v, *caches, layer, *biases["sample"], o_b, tp)

    c_out = _gate_call(vn, cu, p["w_sp"], p["b_full"], CHUNK, 0, n // CHUNK)
    c_out = _gate_call(vn, cu, p["w_sp_s"], p["b_full_s"], TT, tp // TT, 1, sub=3, big=c_out)

    h, xs3, aux, cnt_tiles = _mix_call(a_out, o_b, c_out, x, p["w_out"], p["b_out"],
                                       p["ln1_g"], p["ln1_b"], p["w_router"], p["b_router"])
    expert_tabs, n_used, combine_tabs = _routing_tables(cnt_tiles, n)
    y3 = _expert_call(expert_tabs, n_used, xs3, *experts, layer)
    last = layer == DEPTH - 1
    y = _combine_call(*combine_tabs, y3, h, aux, p["ln2_g"], p["ln2_b"], tp=tp if last else None)
    return y, a, k, v, vn, new_a_s


def kernel(x_prompt, x_sample, state_a_conv, cache_b_k, cache_b_v, w_in, b_in, w_dw, b_dw, ln_a_g, ln_a_b, ln_c_g, ln_c_b, w_sp, b_sp, w_out, b_out, ln1_g, ln1_b, w_router, b_router, w1, b1, w2, b2, ln2_g, ln2_b):
    bp, tp, _ = x_prompt.shape
    bs, ts, _ = x_sample.shape
    n_cache = cache_b_k.shape[2]
    assert bp == 1 and ts == DEC_SEQ and bs * ts == TT and tp % WIN_MAX == 0
    assert n_cache == WIN_MAX
    n = tp + bs * ts
    keep = min(WIN_MAX, tp)
    hist = CONV_W - 1

    weights = (w_in, b_in, w_dw, b_dw, ln_a_g, ln_a_b, ln_c_g, ln_c_b, w_sp, b_sp, w_out, b_out,
               ln1_g, ln1_b, w_router, b_router, ln2_g, ln2_b)
    biases = {"prompt": [_prompt_bias(d) for d in DILATIONS], "sample": _sample_bias(n_cache)}
    experts = (w1.reshape(DEPTH * N_EXPERTS, D_MODEL, 2 * D_FF), b1.reshape(DEPTH * N_EXPERTS, 1, 2 * D_FF),
               w2.reshape(DEPTH * N_EXPERTS, D_FF, D_MODEL), b2.reshape(DEPTH * N_EXPERTS, 1, D_MODEL))
    to_feature_major = lambda c: jnp.transpose(c, (0, 1, 3, 4, 2)).reshape(DEPTH, bs, C_B, n_cache)
    caches = (to_feature_major(cache_b_k), to_feature_major(cache_b_v))
    heads = lambda a: jnp.transpose(a, (1, 0, 2)).reshape(a.shape[1], N_HEADS_B, HEAD_DIM)

    x = jnp.concatenate([x_prompt[0], x_sample.reshape(bs * ts, D_MODEL)], axis=0)
    outs = {name: [] for name in ("a_p", "a_s", "k_p", "v_p", "k_s", "v_s", "c_s")}
    for l in range(DEPTH):
        p = _layer_params(l, *weights)
        hist_p = jnp.pad(state_a_conv[l], ((0, 0), (HALO - hist, 0), (0, SEG - C_A)))
        x, a, k, v, vn, new_a_s = _layer(x, p, hist_p, caches, experts, biases, tp, l)
        outs["a_p"].append(a[tp - hist:tp, :C_A][None])
        outs["a_s"].append(new_a_s[:, HALO - hist:, :C_A])
        outs["k_p"].append(heads(k[:, tp - keep:tp])[None])
        outs["v_p"].append(heads(v[:, tp - keep:tp])[None])
        outs["k_s"].append(heads(k[:, tp:]).reshape(bs, ts, N_HEADS_B, HEAD_DIM))
        outs["v_s"].append(heads(v[:, tp:]).reshape(bs, ts, N_HEADS_B, HEAD_DIM))
        outs["c_s"].append(vn[tp:, :C_C].reshape(bs, ts, C_C))
    stack = lambda name: jnp.stack(outs[name])
    y_prompt, y_sample = x
    return (y_prompt[None], y_sample.reshape(bs, ts, D_MODEL), stack("a_p"), stack("a_s"),
            stack("k_p"), stack("v_p"), stack("k_s"), stack("v_s"), stack("c_s"))
```

```python
import functools
import math

import jax
import jax.numpy as jnp
from jax import lax
from jax.experimental import pallas as pl
from jax.experimental.pallas import tpu as pltpu

F32 = jnp.float32
BF16 = jnp.bfloat16

D_MODEL = 1024
HEAD_DIM = 64
N_HEADS_B = 6
C_B = N_HEADS_B * HEAD_DIM
C_A = 320
C_C = 320
N_GROUPS_C = 5
CONV_W = 31
DILATIONS = (1, 4, 16)
BAND = 128
WIN_MAX = 2048
CHUNK = 128
N_EXPERTS = 32
TOP_K = 4
D_FF = 1024
SWIGLU_LIMIT = 7.0
SWIGLU_ALPHA = 1.702
LN_EPS = 1e-5
DEPTH = 2
DN_ALPHA = (2.0 * DEPTH) ** 0.25
DEC_SEQ = 8

LANES = 128
SUBLANES = 8
SEG = 384
N_SEG = 7
N_PAIRS = C_B // LANES
ROW_TILES = D_MODEL // LANES
assert ROW_TILES == SUBLANES

TT = 256
TM = 512
FF_CHUNK = 512
NEG = -1e30
MIB = 1024 * 1024
VMEM_LIMIT = 48 * MIB
EXPERT_VMEM_LIMIT = 56 * MIB


def _params(n_axes=1, vmem_limit=VMEM_LIMIT):
    return pltpu.CompilerParams(dimension_semantics=("arbitrary",) * n_axes,
                                vmem_limit_bytes=vmem_limit)


def _full(a):
    nd = a.ndim
    return pl.BlockSpec(a.shape, lambda *_: (0,) * nd)


def _ln_valid(x, g, b, n_valid):
    col = lax.broadcasted_iota(jnp.int32, x.shape, 1)
    ok = col < n_valid
    mu = jnp.sum(jnp.where(ok, x, 0.0), axis=-1, keepdims=True) / n_valid
    xc = jnp.where(ok, x - mu, 0.0)
    var = jnp.sum(xc * xc, axis=-1, keepdims=True) / n_valid
    return xc * lax.rsqrt(var + LN_EPS) * g + b


def _ln_full(x, g, b):
    mu = jnp.mean(x, axis=-1, keepdims=True)
    xc = x - mu
    var = jnp.mean(xc * xc, axis=-1, keepdims=True)
    return xc * lax.rsqrt(var + LN_EPS) * g + b


def _spatial_gate(vn, cu, w_ref, bfull, rows, sub):
    r = lax.broadcasted_iota(jnp.int32, (rows, rows), 0)
    c = lax.broadcasted_iota(jnp.int32, (rows, rows), 1)
    causal = c <= r
    if sub is not None:
        causal = jnp.logical_and(causal, lax.shift_right_logical(r, sub) == lax.shift_right_logical(c, sub))
    lo_half = lax.broadcasted_iota(jnp.int32, (rows, LANES), 1) < HEAD_DIM
    cols = []
    for pair in range(SEG // LANES):
        cs = slice(pair * LANES, (pair + 1) * LANES)
        acc = jnp.zeros((rows, LANES), F32)
        for hh in range(2):
            g = pair * 2 + hh
            if g >= N_GROUPS_C:
                continue
            keep = lo_half if hh == 0 else jnp.logical_not(lo_half)
            wm = jnp.where(causal, w_ref[g], 0.0).astype(BF16)
            vm = jnp.where(keep, vn[:, cs], 0.0).astype(BF16)
            acc = acc + jnp.dot(wm, vm, preferred_element_type=F32)
        cols.append(cu[:, cs] * (acc + bfull[:, cs]))
    return jnp.concatenate(cols, axis=1)


def _proj_kernel(x_ref, w_ref, b_ref, lcg_ref, lcb_ref, wsp_ref, bsp_ref, wsps_ref, bsps_ref,
                 q_ref, k_ref, v_ref, a_ref, c_ref, vns_ref, *, n_prompt):
    xb = x_ref[...].astype(BF16)

    def seg(i):
        lo = i * SEG
        return (jnp.dot(xb, w_ref[:, lo:lo + SEG], preferred_element_type=F32)
                + b_ref[:, lo:lo + SEG])

    for i, ref in enumerate((q_ref, k_ref, v_ref)):
        qkv = seg(i)
        for pair in range(N_PAIRS):
            ref[pair] = qkv[:, pair * LANES:(pair + 1) * LANES]
    a_ref[...] = seg(3) * jax.nn.sigmoid(seg(4))
    cu = seg(5)
    vn = _ln_valid(seg(6), lcg_ref[...], lcb_ref[...], C_C)
    is_prompt = pl.program_id(0) < n_prompt

    @pl.when(is_prompt)
    def _():
        for ch in range(0, TT, CHUNK):
            rows = slice(ch, ch + CHUNK)
            c_ref[rows, :] = _spatial_gate(vn[rows], cu[rows], wsp_ref, bsp_ref[...], CHUNK, None)

    @pl.when(jnp.logical_not(is_prompt))
    def _():
        c_ref[...] = _spatial_gate(vn, cu, wsps_ref, bsps_ref[...], TT, DEC_SEQ.bit_length() - 1)
        vns_ref[...] = vn


def _proj_call(x, w_in_p, b_in_p, lcg, lcb, w_sp, b_full, w_sp_s, b_full_s, tp):
    n = x.shape[0]
    row = lambda w: pl.BlockSpec((TT, w), lambda i: (i, 0))
    heads = pl.BlockSpec((N_PAIRS, TT, LANES), lambda i: (0, i, 0))
    out = jax.ShapeDtypeStruct((n, SEG), F32)
    out3 = jax.ShapeDtypeStruct((N_PAIRS, n, LANES), F32)
    consts = (w_in_p, b_in_p, lcg, lcb, w_sp, b_full, w_sp_s, b_full_s)
    return pl.pallas_call(
        functools.partial(_proj_kernel, n_prompt=tp // TT), grid=(n // TT,),
        in_specs=[row(D_MODEL)] + [_full(c) for c in consts],
        out_specs=[heads] * 3 + [row(SEG)] * 2 + [pl.BlockSpec((TT, SEG), lambda i: (0, 0))],
        out_shape=[out3] * 3 + [out] * 2 + [jax.ShapeDtypeStruct((n - tp, SEG), F32)],
        compiler_params=_params(), name="in_proj")(x, *consts)


HALO = 32
CONV_CHUNK = 32


def _conv_tail(acc, bdw_ref, g_ref, b_ref):
    return jax.nn.silu(_ln_valid(acc + bdw_ref[...], g_ref[...], b_ref[...], C_A))


def _conv_prompt_kernel(halo_ref, a_ref, w_ref, bdw_ref, g_ref, b_ref, o_ref, buf, shifted, *, n_prompt):
    i = pl.program_id(0)

    @pl.when(i < n_prompt)
    def _():
        buf[0:HALO, :] = jnp.where(i > 0, halo_ref[...], 0.0)
        buf[HALO:, :] = a_ref[...]
        n_rows = HALO + TT - SUBLANES
        for sh in range(1, SUBLANES):
            shifted[sh - 1, 0:n_rows, :] = buf[sh:sh + n_rows, :]
        base = HALO - (CONV_W - 1)
        for r in range(0, TT, CONV_CHUNK):
            acc = jnp.zeros((CONV_CHUNK, SEG), F32)
            for j in range(CONV_W):
                tiles, sh = divmod(base + j, SUBLANES)
                src = buf if sh == 0 else shifted.at[sh - 1]
                lo = r + tiles * SUBLANES
                acc = acc + w_ref[j:j + 1, :] * src[lo:lo + CONV_CHUNK, :]
            o_ref[r:r + CONV_CHUNK, :] = _conv_tail(acc, bdw_ref, g_ref, b_ref)

    @pl.when(i >= n_prompt)
    def _():
        o_ref[...] = jnp.zeros_like(o_ref)


def _conv_prompt_call(a, w_dw_p, bdw, g, b, tp):
    n = a.shape[0]
    per = TT // HALO
    halo = pl.BlockSpec((HALO, SEG), lambda i: (jnp.maximum(i * per - 1, 0), 0))
    row = pl.BlockSpec((TT, SEG), lambda i: (i, 0))
    return pl.pallas_call(
        functools.partial(_conv_prompt_kernel, n_prompt=tp // TT), grid=(n // TT,),
        in_specs=[halo, row, _full(w_dw_p), _full(bdw), _full(g), _full(b)],
        out_specs=row, out_shape=jax.ShapeDtypeStruct((n, SEG), F32),
        scratch_shapes=[pltpu.VMEM((HALO + TT, SEG), F32),
                        pltpu.VMEM((SUBLANES - 1, HALO + TT, SEG), F32)],
        compiler_params=_params(), name="conv_prompt")(a, a, w_dw_p, bdw, g, b)


SB = 8


def _conv_sample_kernel(hist_ref, a_ref, w_ref, bdw_ref, g_ref, b_ref, big_ref,
                        o_ref, newa_ref, buf):
    del big_ref
    bs = hist_ref.shape[0]
    buf[:, 0:HALO, :] = hist_ref[...]
    buf[:, HALO:, :] = a_ref[...].reshape(bs, DEC_SEQ, SEG)
    base = HALO - (CONV_W - 1)
    for s in range(0, bs, SB):
        acc = jnp.zeros((SB, DEC_SEQ, SEG), F32)
        for j in range(CONV_W):
            acc = acc + w_ref[j:j + 1, :][None] * buf[s:s + SB, base + j: base + j + DEC_SEQ, :]
        y = _conv_tail(acc.reshape(SB * DEC_SEQ, SEG), bdw_ref, g_ref, b_ref)
        o_ref[s * DEC_SEQ:(s + SB) * DEC_SEQ, :] = y
    newa_ref[...] = buf[:, DEC_SEQ:, :]


def _conv_sample_call(hist_p, a, w_dw_p, bdw, g, b, a_out, tp):
    bs = hist_p.shape[0]
    rows = bs * DEC_SEQ
    blk = pl.BlockSpec((rows, SEG), lambda i: (tp // rows, 0))
    return pl.pallas_call(
        _conv_sample_kernel, grid=(1,),
        in_specs=[_full(hist_p), blk, _full(w_dw_p), _full(bdw), _full(g), _full(b),
                  pl.BlockSpec(memory_space=pl.ANY)],
        out_specs=[blk, pl.BlockSpec((bs, HALO, SEG), lambda i: (0, 0, 0))],
        out_shape=[jax.ShapeDtypeStruct(a_out.shape, F32),
                   jax.ShapeDtypeStruct((bs, HALO, SEG), F32)],
        scratch_shapes=[pltpu.VMEM((bs, HALO + DEC_SEQ, SEG), F32)],
        input_output_aliases={6: 0},
        compiler_params=_params(), name="conv_sample")(hist_p, a, w_dw_p, bdw, g, b, a_out)


QB = 128
assert QB == BAND


SPAN = BAND * max(DILATIONS)
UNITS = SPAN // QB
UNIT_UNROLL = 8


def _attn_prompt_kernel(q_ref, kh_ref, kc_ref, vh_ref, vc_ref, b1_ref, b4_ref, b16_ref, o_ref,
                        kbuf, vbuf, oacc, lacc, *, n_prompt):
    is_prompt = pl.program_id(1) < n_prompt
    refs = (q_ref, kh_ref, kc_ref, vh_ref, vc_ref, (b1_ref, b4_ref, b16_ref), o_ref, kbuf, vbuf, oacc, lacc)
    pl.when(is_prompt)(functools.partial(_attn_prompt_span, refs))

    @pl.when(jnp.logical_not(is_prompt))
    def _():
        o_ref[...] = jnp.zeros_like(o_ref)


def _attn_prompt_span(refs):
    q_ref, kh_ref, kc_ref, vh_ref, vc_ref, bias_refs, o_ref, kbuf, vbuf, oacc, lacc = refs
    pair = pl.program_id(0)
    first_span = pl.program_id(1) == 0
    kbuf[0:SPAN, :] = kh_ref[0]
    kbuf[SPAN:, :] = kc_ref[0]
    vbuf[0:SPAN, :] = vh_ref[0]
    vbuf[SPAN:, :] = vc_ref[0]
    kcol = lax.broadcasted_iota(jnp.int32, (QB, 2 * QB), 1)
    lane = lax.broadcasted_iota(jnp.int32, (QB, LANES), 1)
    lo_half = lane < HEAD_DIM

    for dil, bias_ref in zip(DILATIONS, bias_refs):
        def unit(u, carry, dil=dil, bias_ref=bias_ref):
            sub = u // dil
            t0 = sub * (QB * dil) + (u - sub * dil)
            if dil == 1:
                rows_q = pl.ds(pl.multiple_of(t0, QB), QB)
                rows_k = pl.ds(pl.multiple_of(SPAN + t0 - QB, QB), 2 * QB)
            else:
                rows_q = pl.ds(t0, QB, stride=dil)
                rows_k = pl.ds(SPAN + t0 - QB * dil, 2 * QB, stride=dil)
            no_halo = jnp.logical_and(jnp.logical_and(first_span, sub == 0), kcol < QB)
            qp = q_ref[0, rows_q, :] * (HEAD_DIM ** -0.5)
            kk = kbuf[rows_k, :].astype(BF16)
            vv = vbuf[rows_k, :].astype(BF16)
            outs, lses = [], []
            for hh in range(2):
                keep = lo_half if hh == 0 else jnp.logical_not(lo_half)
                qm = jnp.where(keep, qp, 0.0).astype(BF16)
                s = lax.dot_general(qm, kk, (((1,), (1,)), ((), ())), preferred_element_type=F32)
                s = s + bias_ref[pair * 2 + hh]
                s = jnp.where(no_halo, NEG, s)
                m = jnp.max(s, axis=-1, keepdims=True)
                p = jnp.exp(s - m)
                l = jnp.sum(p, axis=-1, keepdims=True)
                pv = jnp.dot(p.astype(BF16), vv, preferred_element_type=F32)
                outs.append(pv / l)
                lses.append(m + jnp.log(l))
            o_new = jnp.where(lo_half, outs[0], outs[1])
            l_new = jnp.where(lo_half, lses[0], lses[1])
            if dil != 1:
                l_old = lacc[rows_q, :]
                mx = jnp.maximum(l_old, l_new)
                w_old = jnp.exp(l_old - mx)
                w_new = jnp.exp(l_new - mx)
                tot = w_old + w_new
                o_new = (w_old * oacc[rows_q, :] + w_new * o_new) / tot
                l_new = mx + jnp.log(tot)
            oacc[rows_q, :] = o_new
            lacc[rows_q, :] = l_new
            return carry

        lax.fori_loop(0, UNITS, unit, 0, unroll=UNIT_UNROLL)
    o_ref[0] = oacc[...]


def _attn_prompt_call(q, k, v, biases, tp):
    n = q.shape[1]
    cur = pl.BlockSpec((1, SPAN, LANES), lambda p, m: (p, m, 0))
    halo = pl.BlockSpec((1, SPAN, LANES), lambda p, m: (p, jnp.maximum(m - 1, 0), 0))
    scratch = [pltpu.VMEM((2 * SPAN, LANES), F32), pltpu.VMEM((2 * SPAN, LANES), F32),
               pltpu.VMEM((SPAN, LANES), F32), pltpu.VMEM((SPAN, LANES), F32)]
    return pl.pallas_call(
        functools.partial(_attn_prompt_kernel, n_prompt=tp // SPAN),
        grid=(N_PAIRS, pl.cdiv(n, SPAN)),
        in_specs=[cur, halo, cur, halo, cur] + [_full(b) for b in biases],
        out_specs=cur, out_shape=jax.ShapeDtypeStruct((N_PAIRS, n, LANES), F32),
        scratch_shapes=scratch, compiler_params=_params(2), name="attn_prompt")(q, k, k, v, v, *biases)


def _alibi_slopes():
    return [2.0 ** (-8.0 * (h + 1) / N_HEADS_B) for h in range(N_HEADS_B)]


def _prompt_bias(dil):
    i = jnp.arange(QB)[:, None]
    j = jnp.arange(2 * QB)[None, :]
    rel = i + QB - j
    ok = jnp.logical_and(rel >= 0, rel <= BAND)
    slopes = jnp.asarray(_alibi_slopes(), F32)[:, None, None]
    pen = -slopes * (rel * dil).astype(F32)[None]
    return jnp.where(ok[None], pen, NEG).astype(F32)


def _branch_multiplicity(dist):
    cnt = jnp.zeros(dist.shape, jnp.int32)
    for dil in DILATIONS:
        cnt = cnt + jnp.logical_and(dist % dil == 0, dist <= BAND * dil).astype(jnp.int32)
    return cnt


def _sample_bias(n_cache):
    t = jnp.arange(DEC_SEQ)[:, None]
    dist_c = n_cache + t - jnp.arange(n_cache)[None, :]
    dist_n = t - jnp.arange(LANES)[None, :]
    ok_n = jnp.logical_and(dist_n >= 0, jnp.arange(LANES)[None, :] < DEC_SEQ)
    slopes = jnp.asarray(_alibi_slopes(), F32)[:, None, None]

    def bias(dist, ok):
        mult = _branch_multiplicity(jnp.maximum(dist, 0))
        ok = jnp.logical_and(ok, mult > 0)
        val = -slopes * dist.astype(F32)[None] + jnp.log(jnp.maximum(mult, 1).astype(F32))[None]
        return jnp.where(ok[None], val, NEG).astype(F32)

    bc = bias(dist_c, jnp.ones(dist_c.shape, bool)).reshape(N_HEADS_B * DEC_SEQ, n_cache)
    bn = bias(dist_n, ok_n).reshape(N_HEADS_B * DEC_SEQ, LANES)
    return bc, bn


def _attn_sample_kernel(q_ref, kn_ref, vn_ref, kc_ref, vc_ref, bc_ref, bn_ref, big_ref, o_ref):
    del big_ref
    rows = N_HEADS_B * DEC_SEQ
    wide = lambda ref: jnp.concatenate([ref[pair] for pair in range(N_PAIRS)], axis=1)
    q = wide(q_ref) * (HEAD_DIM ** -0.5)
    qrep = jnp.concatenate([q] * N_HEADS_B, axis=0)
    rh = lax.shift_right_logical(lax.broadcasted_iota(jnp.int32, (rows, SEG), 0), 3)
    ch = lax.shift_right_logical(lax.broadcasted_iota(jnp.int32, (rows, SEG), 1), 6)
    own = rh == ch
    qm = jnp.where(own, qrep, 0.0).astype(BF16)
    pad = jnp.zeros((LANES - DEC_SEQ, SEG), F32)
    kn = jnp.concatenate([wide(kn_ref), pad], axis=0).astype(BF16)
    vn = jnp.concatenate([wide(vn_ref), pad], axis=0).astype(BF16)
    nt = (((1,), (1,)), ((), ()))
    s_c = jnp.dot(qm, kc_ref[0, 0].astype(BF16), preferred_element_type=F32) + bc_ref[...]
    s_n = lax.dot_general(qm, kn, nt, preferred_element_type=F32) + bn_ref[...]
    m = jnp.maximum(jnp.max(s_c, axis=-1, keepdims=True), jnp.max(s_n, axis=-1, keepdims=True))
    p_c = jnp.exp(s_c - m)
    p_n = jnp.exp(s_n - m)
    l = jnp.sum(p_c, axis=-1, keepdims=True) + jnp.sum(p_n, axis=-1, keepdims=True)
    r = (lax.dot_general(p_c.astype(BF16), vc_ref[0, 0].astype(BF16), nt, preferred_element_type=F32)
         + jnp.dot(p_n.astype(BF16), vn, preferred_element_type=F32)) / l
    r = jnp.where(own, r, 0.0)
    o = r[0:DEC_SEQ]
    for h in range(1, N_HEADS_B):
        o = o + r[h * DEC_SEQ:(h + 1) * DEC_SEQ]
    for pair in range(N_PAIRS):
        o_ref[pair] = o[:, pair * LANES:(pair + 1) * LANES]


def _attn_sample_call(q, k, v, cache_kt, cache_vt, layer, bc, bn, o_big, tp):
    bs, n_cache = cache_kt.shape[1], cache_kt.shape[3]
    new = pl.BlockSpec((N_PAIRS, DEC_SEQ, LANES), lambda b: (0, tp // DEC_SEQ + b, 0))
    cache = pl.BlockSpec((1, 1, C_B, n_cache), lambda b: (layer, b, 0, 0))
    return pl.pallas_call(
        _attn_sample_kernel, grid=(bs,),
        in_specs=[new, new, new, cache, cache, _full(bc), _full(bn),
                  pl.BlockSpec(memory_space=pl.ANY)],
        out_specs=new, out_shape=jax.ShapeDtypeStruct(o_big.shape, F32),
        input_output_aliases={7: 0},
        compiler_params=_params(), name="attn_sample")(q, k, v, cache_kt, cache_vt, bc, bn, o_big)


def _mix_kernel(a_ref, o_ref, c_ref, x_ref, wo_ref, bo_ref, g1_ref, b1_ref,
                wr_ref, br_ref, h_ref, xs3_ref, aux_ref, cnt_ref):
    mixed = jnp.concatenate([a_ref[...]] + [o_ref[pair] for pair in range(N_PAIRS)] + [c_ref[...]], axis=1)
    mix = jnp.dot(mixed.astype(BF16), wo_ref[...], preferred_element_type=F32) + bo_ref[...]
    h = _ln_full(DN_ALPHA * x_ref[...] + mix, g1_ref[...], b1_ref[...])
    h_ref[...] = h

    hb = h.astype(BF16)
    h_lo = (h - hb.astype(F32)).astype(BF16)
    both = jnp.dot(hb, wr_ref[...], preferred_element_type=F32)
    logits = (both[:, :LANES] + both[:, LANES:]
              + jnp.dot(h_lo, wr_ref[:, :LANES], preferred_element_type=F32) + br_ref[...])
    lane = lax.broadcasted_iota(jnp.int32, (TT, LANES), 1)
    lane_f = lane.astype(F32)
    vals, idxs, sels = [], [], []
    cur = logits
    for _ in range(TOP_K):
        m = jnp.max(cur, axis=-1, keepdims=True)
        idx = jnp.min(jnp.where(cur == m, lane_f, float(LANES)), axis=-1, keepdims=True)
        sel = lane_f == idx
        vals.append(m)
        idxs.append(idx)
        sels.append(sel)
        cur = jnp.where(sel, -jnp.inf, cur)
    exps = [jnp.exp(v - vals[0]) for v in vals]
    den = exps[0] + exps[1] + exps[2] + exps[3]

    onehot = jnp.zeros((TT, LANES), F32)
    for sel in sels:
        onehot = onehot + sel.astype(F32)
    r = lax.broadcasted_iota(jnp.int32, (TT, TT), 0)
    c = lax.broadcasted_iota(jnp.int32, (TT, TT), 1)
    below = (c < r).astype(BF16)
    earlier = jnp.dot(below, onehot.astype(BF16), preferred_element_type=F32)
    cnt = jnp.broadcast_to(jnp.sum(onehot, axis=0, keepdims=True), (SUBLANES, LANES))
    er = lax.broadcasted_iota(jnp.int32, (LANES, LANES), 0)
    ec = lax.broadcasted_iota(jnp.int32, (LANES, LANES), 1)
    off = jnp.dot(cnt.astype(BF16), (er < ec).astype(BF16), preferred_element_type=F32)[0:1]
    place = earlier + off
    aux = jnp.zeros((TT, LANES), F32)
    rows = []
    for k in range(TOP_K):
        row = jnp.sum(jnp.where(sels[k], place, 0.0), axis=-1, keepdims=True)
        rows.append(row)
        aux = aux + jnp.where(lane == k, row, 0.0) + jnp.where(lane == TOP_K + k, exps[k] / den, 0.0)
    aux_ref[...] = aux
    cnt_ref[0] = cnt.astype(jnp.int32)

    aux_t = jnp.transpose(aux)
    dest = lax.broadcasted_iota(jnp.int32, (TOP_K * TT, TT), 0).astype(F32)
    disp = jnp.zeros((TOP_K * TT, TT), F32)
    for k in range(TOP_K):
        disp = disp + (dest == aux_t[k:k + 1, :]).astype(F32)
    xs = jnp.dot(disp.astype(BF16), hb, preferred_element_type=F32)
    for s in range(ROW_TILES):
        xs3_ref[pl.ds(s, TOP_K * TT, stride=SUBLANES), :] = xs[:, s * LANES:(s + 1) * LANES]


def _mix_call(a_out, o_b, c_out, x, wo, bo, g1, b1, wr, br):
    n = x.shape[0]
    row = lambda w: pl.BlockSpec((TT, w), lambda i: (i, 0))
    xs3 = pl.BlockSpec((TOP_K * TT * SUBLANES, LANES), lambda i: (i, 0))
    cnt = pl.BlockSpec((1, SUBLANES, LANES), lambda i: (i, 0, 0))
    return pl.pallas_call(
        _mix_kernel, grid=(n // TT,),
        in_specs=[row(SEG), pl.BlockSpec((N_PAIRS, TT, LANES), lambda i: (0, i, 0)), row(SEG), row(D_MODEL),
                  _full(wo), _full(bo), _full(g1), _full(b1), _full(wr), _full(br)],
        out_specs=[row(D_MODEL), xs3, row(LANES), cnt],
        out_shape=[jax.ShapeDtypeStruct((n, D_MODEL), F32),
                   jax.ShapeDtypeStruct((n * TOP_K * SUBLANES, LANES), F32),
                   jax.ShapeDtypeStruct((n, LANES), F32),
                   jax.ShapeDtypeStruct((n // TT, SUBLANES, LANES), jnp.int32)],
        compiler_params=_params(), name="mix_ln_router")(
            a_out, o_b, c_out, x, wo, bo, g1, b1, wr, br)


def _rows(start_row, n_rows):
    return pl.ds(pl.multiple_of(start_row * SUBLANES, SUBLANES), n_rows * SUBLANES)


def _expert_gather(tabs, blk, slot, xs3_hbm, buf, sem):
    blk_e, blk_s0, j_lo, j_hi, cnt, off, cum, tot = tabs
    e = blk_e[blk]
    s0 = blk_s0[blk]
    base = slot * TM

    @pl.when(tot[e] - s0 < TM)
    def _():
        buf[_rows(base, TM), :] = jnp.zeros((TM * SUBLANES, LANES), F32)

    def body(j, carry):
        run = cum[j * N_EXPERTS + e]
        lo = jnp.maximum(run, s0)
        hi = jnp.minimum(run + cnt[j * N_EXPERTS + e], s0 + TM)

        @pl.when(hi > lo)
        def _():
            src = j * (TOP_K * TT) + off[j * N_EXPERTS + e] + (lo - run)
            pltpu.make_async_copy(xs3_hbm.at[_rows(src, hi - lo)],
                                  buf.at[_rows(base + lo - s0, hi - lo)], sem.at[slot]).start()
        return carry

    lax.fori_loop(j_lo[blk], j_hi[blk], body, 0)


def _expert_gather_wait(tabs, blk, slot, xs3_hbm, buf, sem):
    blk_e, blk_s0, _, _, _, _, _, tot = tabs
    valid = jnp.minimum(tot[blk_e[blk]] - blk_s0[blk], TM)
    pltpu.make_async_copy(xs3_hbm.at[_rows(0, valid)], buf.at[_rows(slot * TM, valid)],
                          sem.at[slot]).wait()


def _expert_kernel(blk_e, blk_s0, j_lo, j_hi, cnt, off, cum, tot, n_used_ref,
                   xs3_hbm, w1_ref, b1_ref, w2_ref, b2_ref, y3_ref, buf, xb, w1b, w2b, sem):
    tabs = (blk_e, blk_s0, j_lo, j_hi, cnt, off, cum, tot)
    b = pl.program_id(0)
    n_used = n_used_ref[0]
    slot = lax.rem(b, 2)
    used = b < n_used

    @pl.when(b == 0)
    def _():
        _expert_gather(tabs, 0, 0, xs3_hbm, buf, sem)

    @pl.when(used)
    def _():
        _expert_gather_wait(tabs, b, slot, xs3_hbm, buf, sem)

    @pl.when(b + 1 < n_used)
    def _():
        _expert_gather(tabs, b + 1, 1 - slot, xs3_hbm, buf, sem)

    new_expert = jnp.logical_or(b == 0, blk_e[b] != blk_e[jnp.maximum(b - 1, 0)])

    @pl.when(jnp.logical_and(used, new_expert))
    def _():
        w1b[...] = w1_ref[0].astype(BF16)
        w2b[...] = w2_ref[0].astype(BF16)

    @pl.when(used)
    def _():
        base = slot * (TM * SUBLANES)
        for s in range(ROW_TILES):
            xb[:, s * LANES:(s + 1) * LANES] = buf[pl.ds(base + s, TM, stride=SUBLANES), :].astype(BF16)
        x = xb[...]
        y = jnp.broadcast_to(b2_ref[0], (TM, D_MODEL))
        for c in range(0, D_FF, FF_CHUNK):
            cols = slice(c, c + FF_CHUNK)
            ucols = slice(D_FF + c, D_FF + c + FF_CHUNK)
            g = jnp.dot(x, w1b[:, cols], preferred_element_type=F32) + b1_ref[0, :, cols]
            u = jnp.dot(x, w1b[:, ucols], preferred_element_type=F32) + b1_ref[0, :, ucols]
            gate = jnp.minimum(g, SWIGLU_LIMIT)
            up = jnp.clip(u, -SWIGLU_LIMIT, SWIGLU_LIMIT)
            hh = (up + 1.0) * (gate * jax.nn.sigmoid(SWIGLU_ALPHA * gate))
            y = y + jnp.dot(hh.astype(BF16), w2b[cols, :], preferred_element_type=F32)
        for s in range(ROW_TILES):
            y3_ref[pl.ds(s, TM, stride=SUBLANES), :] = y[:, s * LANES:(s + 1) * LANES]

    @pl.when(jnp.logical_not(used))
    def _():
        y3_ref[...] = jnp.zeros_like(y3_ref)


def _expert_call(tabs, n_used, xs3, w1, b1, w2, b2, layer):
    n_blocks = tabs[0].shape[0]
    by_expert = lambda shape: pl.BlockSpec(shape, lambda b, e, *_: (layer * N_EXPERTS + e[b], 0, 0))
    grid_spec = pltpu.PrefetchScalarGridSpec(
        num_scalar_prefetch=len(tabs) + 1, grid=(n_blocks,),
        in_specs=[pl.BlockSpec(memory_space=pl.ANY),
                  by_expert((1, D_MODEL, 2 * D_FF)), by_expert((1, 1, 2 * D_FF)),
                  by_expert((1, D_FF, D_MODEL)), by_expert((1, 1, D_MODEL))],
        out_specs=pl.BlockSpec((TM * SUBLANES, LANES), lambda b, *_: (b, 0)),
        scratch_shapes=[pltpu.VMEM((2 * TM * SUBLANES, LANES), F32),
                        pltpu.VMEM((TM, D_MODEL), BF16),
                        pltpu.VMEM((D_MODEL, 2 * D_FF), BF16),
                        pltpu.VMEM((D_FF, D_MODEL), BF16),
                        pltpu.SemaphoreType.DMA((2,))])
    return pl.pallas_call(
        _expert_kernel, grid_spec=grid_spec,
        out_shape=jax.ShapeDtypeStruct((n_blocks * TM * SUBLANES, LANES), F32),
        compiler_params=_params(vmem_limit=EXPERT_VMEM_LIMIT), name="expert_ffn")(
            *tabs, n_used, xs3, w1, b1, w2, b2)


PAIRS = TOP_K * TT


def _combine_gather(cnt, off, src, tile, slot, y3_hbm, buf, sem):
    def body(e, carry):
        n = cnt[tile * N_EXPERTS + e]

        @pl.when(n > 0)
        def _():
            pltpu.make_async_copy(y3_hbm.at[_rows(src[tile * N_EXPERTS + e], n)],
                                  buf.at[_rows(slot * PAIRS + off[tile * N_EXPERTS + e], n)],
                                  sem.at[slot]).start()
        return carry

    lax.fori_loop(0, N_EXPERTS, body, 0)


def _combine_kernel(cnt, off, src, y3_hbm, h_ref, aux_ref, g2_ref, b2_ref, *rest, n_prompt):
    if n_prompt is None:
        o_ref, buf, ys, sem = rest
    else:
        op_ref, os_ref, buf, ys, sem = rest
    i = pl.program_id(0)
    n_steps = pl.num_programs(0)
    slot = lax.rem(i, 2)

    @pl.when(i == 0)
    def _():
        _combine_gather(cnt, off, src, 0, 0, y3_hbm, buf, sem)

    pltpu.make_async_copy(y3_hbm.at[_rows(0, PAIRS)], buf.at[_rows(slot * PAIRS, PAIRS)],
                          sem.at[slot]).wait()

    @pl.when(i + 1 < n_steps)
    def _():
        _combine_gather(cnt, off, src, i + 1, 1 - slot, y3_hbm, buf, sem)

    base = slot * (PAIRS * SUBLANES)
    for s in range(ROW_TILES):
        ys[:, s * LANES:(s + 1) * LANES] = buf[pl.ds(base + s, PAIRS, stride=SUBLANES), :].astype(BF16)
    aux = aux_ref[...]
    dest = lax.broadcasted_iota(jnp.int32, (TT, PAIRS), 1).astype(F32)
    weights = jnp.zeros((TT, PAIRS), F32)
    for k in range(TOP_K):
        weights = weights + jnp.where(dest == aux[:, k:k + 1], aux[:, TOP_K + k:TOP_K + k + 1], 0.0)
    moe = jnp.dot(weights.astype(BF16), ys[...], preferred_element_type=F32)
    y = _ln_full(DN_ALPHA * h_ref[...] + moe, g2_ref[...], b2_ref[...])
    if n_prompt is None:
        o_ref[...] = y
    else:
        @pl.when(i < n_prompt)
        def _():
            op_ref[...] = y

        @pl.when(i >= n_prompt)
        def _():
            os_ref[...] = y


def _combine_call(cnt, off, src, y3, h, aux, g2, b2, tp=None):
    n = h.shape[0]
    row = lambda w: pl.BlockSpec((TT, w), lambda i, *_: (i, 0))
    if tp is None:
        n_prompt = None
        out_specs = row(D_MODEL)
        out_shape = jax.ShapeDtypeStruct((n, D_MODEL), F32)
    else:
        n_prompt = tp // TT
        out_specs = [pl.BlockSpec((TT, D_MODEL), lambda i, *_: (jnp.minimum(i, n_prompt - 1), 0)),
                     pl.BlockSpec((TT, D_MODEL), lambda i, *_: (jnp.maximum(i - n_prompt, 0), 0))]
        out_shape = [jax.ShapeDtypeStruct((tp, D_MODEL), F32),
                     jax.ShapeDtypeStruct((n - tp, D_MODEL), F32)]
    grid_spec = pltpu.PrefetchScalarGridSpec(
        num_scalar_prefetch=3, grid=(n // TT,),
        in_specs=[pl.BlockSpec(memory_space=pl.ANY), row(D_MODEL), row(LANES),
                  pl.BlockSpec(g2.shape, lambda i, *_: (0, 0)), pl.BlockSpec(b2.shape, lambda i, *_: (0, 0))],
        out_specs=out_specs,
        scratch_shapes=[pltpu.VMEM((2 * PAIRS * SUBLANES, LANES), F32),
                        pltpu.VMEM((PAIRS, D_MODEL), BF16),
                        pltpu.SemaphoreType.DMA((2,))])
    return pl.pallas_call(
        functools.partial(_combine_kernel, n_prompt=n_prompt), grid_spec=grid_spec, out_shape=out_shape,
        compiler_params=_params(), name="combine_ln")(cnt, off, src, y3, h, aux, g2, b2)


def _routing_tables(cnt_tiles, n):
    cnt = cnt_tiles[:, 0, :N_EXPERTS]
    off = jnp.cumsum(cnt, axis=1) - cnt
    cum = jnp.cumsum(cnt, axis=0) - cnt
    tot = jnp.sum(cnt, axis=0)
    padded = (tot + TM - 1) // TM * TM
    pends = jnp.cumsum(padded)
    pstart = pends - padded
    n_blocks = n * TOP_K // TM + N_EXPERTS
    blk_start = jnp.arange(n_blocks, dtype=jnp.int32) * TM
    blk_e = jnp.minimum(jnp.sum((pends[None, :] <= blk_start[:, None]).astype(jnp.int32), axis=1),
                        N_EXPERTS - 1)
    blk_s0 = blk_start - pstart[blk_e]
    run_lo = cum[:, blk_e].T
    run_hi = run_lo + cnt[:, blk_e].T
    j_lo = jnp.sum((run_hi <= blk_s0[:, None]).astype(jnp.int32), axis=1)
    j_hi = jnp.sum((run_lo < blk_s0[:, None] + TM).astype(jnp.int32), axis=1)
    n_used = (pends[-1:] // TM).astype(jnp.int32)
    flat = lambda a: a.reshape(-1).astype(jnp.int32)
    expert_tabs = (flat(blk_e), flat(blk_s0), flat(j_lo), flat(j_hi), flat(cnt), flat(off), flat(cum), flat(tot))
    combine_tabs = (flat(cnt), flat(off), flat(pstart[None, :] + cum))
    return expert_tabs, n_used, combine_tabs


def _pad_cols(a, width):
    return jnp.pad(a, [(0, 0)] * (a.ndim - 1) + [(0, width - a.shape[-1])])


def _layer_params(l, w_in, b_in, w_dw, b_dw, ln_a_g, ln_a_b, ln_c_g, ln_c_b, w_sp, b_sp,
                  w_out, b_out, ln1_g, ln1_b, w_router, b_router, ln2_g, ln2_b):
    bounds = [0, C_B, 2 * C_B, 3 * C_B, 3 * C_B + C_A, 3 * C_B + 2 * C_A, 3 * C_B + 2 * C_A + C_C,
              3 * C_B + 2 * C_A + 2 * C_C]
    w_segs = [_pad_cols(w_in[l][:, bounds[i]:bounds[i + 1]], SEG) for i in range(N_SEG)]
    b_segs = [_pad_cols(b_in[l][bounds[i]:bounds[i + 1]], SEG) for i in range(N_SEG)]
    row = lambda a: _pad_cols(a, SEG)[None, :]
    wo = w_out[l]
    pad_rows = lambda a: jnp.pad(a, ((0, SEG - a.shape[0]), (0, 0)))
    bsp = b_sp[l]
    bfull = _pad_cols(jnp.repeat(bsp.T, HEAD_DIM, axis=1), SEG)
    wr = _pad_cols(w_router[l], LANES)
    wr_hi = wr.astype(BF16)
    return dict(
        w_in=jnp.concatenate(w_segs, axis=1).astype(BF16),
        b_in=jnp.concatenate(b_segs)[None, :],
        w_dw=jnp.pad(w_dw[l], ((0, HALO - CONV_W), (0, SEG - C_A))),
        b_dw=row(b_dw[l]), ln_a_g=row(ln_a_g[l]), ln_a_b=row(ln_a_b[l]),
        ln_c_g=row(ln_c_g[l]), ln_c_b=row(ln_c_b[l]),
        w_sp=w_sp[l], b_full=bfull,
        w_sp_s=jnp.tile(w_sp[l][:, :DEC_SEQ, :DEC_SEQ], (1, TT // DEC_SEQ, TT // DEC_SEQ)),
        b_full_s=jnp.tile(bfull[:DEC_SEQ], (TT // DEC_SEQ, 1)),
        w_out=jnp.concatenate([pad_rows(wo[:C_A]), wo[C_A:C_A + C_B], pad_rows(wo[C_A + C_B:])]).astype(BF16),
        b_out=b_out[l][None, :],
        ln1_g=ln1_g[l][None, :], ln1_b=ln1_b[l][None, :],
        w_router=jnp.concatenate([wr_hi, (wr - wr_hi.astype(F32)).astype(BF16)], axis=1),
        b_router=jnp.concatenate([b_router[l], jnp.full((LANES - N_EXPERTS,), NEG, F32)])[None, :],
        ln2_g=ln2_g[l][None, :], ln2_b=ln2_b[l][None, :])


def _layer(x, p, hist_p, caches, experts, biases, tp, layer):
    n = x.shape[0]
    q, k, v, a, c_out, vn_s = _proj_call(x, p["w_in"], p["b_in"], p["ln_c_g"], p["ln_c_b"],
                                         p["w_sp"], p["b_full"], p["w_sp_s"], p["b_full_s"], tp)

    conv_args = (p["w_dw"], p["b_dw"], p["ln_a_g"], p["ln_a_b"])
    a_out = _conv_prompt_call(a, *conv_args, tp)
    a_out, new_a_s = _conv_sample_call(hist_p, a, *conv_args, a_out, tp)

    o_b = _attn_prompt_call(q, k, v, biases["prompt"], tp)
    o_b = _attn_sample_call(q, k, v, *caches, layer, *biases["sample"], o_b, tp)

    h, xs3, aux, cnt_tiles = _mix_call(a_out, o_b, c_out, x, p["w_out"], p["b_out"],
                                       p["ln1_g"], p["ln1_b"], p["w_router"], p["b_router"])
    expert_tabs, n_used, combine_tabs = _routing_tables(cnt_tiles, n)
    y3 = _expert_call(expert_tabs, n_used, xs3, *experts, layer)
    last = layer == DEPTH - 1
    y = _combine_call(*combine_tabs, y3, h, aux, p["ln2_g"], p["ln2_b"], tp=tp if last else None)
    return y, a, k, v, vn_s, new_a_s


def kernel(x_prompt, x_sample, state_a_conv, cache_b_k, cache_b_v, w_in, b_in, w_dw, b_dw, ln_a_g, ln_a_b, ln_c_g, ln_c_b, w_sp, b_sp, w_out, b_out, ln1_g, ln1_b, w_router, b_router, w1, b1, w2, b2, ln2_g, ln2_b):
    bp, tp, _ = x_prompt.shape
    bs, ts, _ = x_sample.shape
    n_cache = cache_b_k.shape[2]
    assert bp == 1 and ts == DEC_SEQ and bs * ts == TT and tp % WIN_MAX == 0
    assert n_cache == WIN_MAX
    n = tp + bs * ts
    keep = min(WIN_MAX, tp)
    hist = CONV_W - 1

    weights = (w_in, b_in, w_dw, b_dw, ln_a_g, ln_a_b, ln_c_g, ln_c_b, w_sp, b_sp, w_out, b_out,
               ln1_g, ln1_b, w_router, b_router, ln2_g, ln2_b)
    biases = {"prompt": [_prompt_bias(d) for d in DILATIONS], "sample": _sample_bias(n_cache)}
    experts = (w1.reshape(DEPTH * N_EXPERTS, D_MODEL, 2 * D_FF), b1.reshape(DEPTH * N_EXPERTS, 1, 2 * D_FF),
               w2.reshape(DEPTH * N_EXPERTS, D_FF, D_MODEL), b2.reshape(DEPTH * N_EXPERTS, 1, D_MODEL))
    to_feature_major = lambda c: jnp.transpose(c, (0, 1, 3, 4, 2)).reshape(DEPTH, bs, C_B, n_cache)
    caches = (to_feature_major(cache_b_k), to_feature_major(cache_b_v))
    heads = lambda a: jnp.transpose(a, (1, 0, 2)).reshape(a.shape[1], N_HEADS_B, HEAD_DIM)

    x = jnp.concatenate([x_prompt[0], x_sample.reshape(bs * ts, D_MODEL)], axis=0)
    outs = {name: [] for name in ("a_p", "a_s", "k_p", "v_p", "k_s", "v_s", "c_s")}
    for l in range(DEPTH):
        p = _layer_params(l, *weights)
        hist_p = jnp.pad(state_a_conv[l], ((0, 0), (HALO - hist, 0), (0, SEG - C_A)))
        x, a, k, v, vn_s, new_a_s = _layer(x, p, hist_p, caches, experts, biases, tp, l)
        outs["a_p"].append(a[tp - hist:tp, :C_A][None])
        outs["a_s"].append(new_a_s[:, HALO - hist:, :C_A])
        outs["k_p"].append(heads(k[:, tp - keep:tp])[None])
        outs["v_p"].append(heads(v[:, tp - keep:tp])[None])
        outs["k_s"].append(heads(k[:, tp:]).reshape(bs, ts, N_HEADS_B, HEAD_DIM))
        outs["v_s"].append(heads(v[:, tp:]).reshape(bs, ts, N_HEADS_B, HEAD_DIM))
        outs["c_s"].append(vn_s[:, :C_C].reshape(bs, ts, C_C))
    stack = lambda name: jnp.stack(outs[name])
    y_prompt, y_sample = x
    return (y_prompt[None], y_sample.reshape(bs, ts, D_MODEL), stack("a_p"), stack("a_s"),
            stack("k_p"), stack("v_p"), stack("k_s"), stack("v_s"), stack("c_s"))
```

```python
import functools
import math

import jax
import jax.numpy as jnp
import numpy as np
from jax import lax
from jax.experimental import pallas as pl
from jax.experimental.pallas import tpu as pltpu

F32 = jnp.float32
BF16 = jnp.bfloat16

D_MODEL = 1024
HEAD_DIM = 64
N_HEADS_B = 6
C_B = N_HEADS_B * HEAD_DIM
C_A = 320
C_C = 320
N_GROUPS_C = 5
CONV_W = 31
DILATIONS = (1, 4, 16)
BAND = 128
WIN_MAX = 2048
CHUNK = 128
N_EXPERTS = 32
TOP_K = 4
D_FF = 1024
SWIGLU_LIMIT = 7.0
SWIGLU_ALPHA = 1.702
LN_EPS = 1e-5
DEPTH = 2
DN_ALPHA = (2.0 * DEPTH) ** 0.25
DEC_SEQ = 8

LANES = 128
SUBLANES = 8
SEG = 384
N_SEG = 7
N_PAIRS = C_B // LANES
ROW_TILES = D_MODEL // LANES
assert ROW_TILES == SUBLANES

TT = 256
TM = 512
FF_CHUNK = 512
NEG = -1e30
LOG2E = math.log2(math.e)
MIB = 1024 * 1024
VMEM_LIMIT = 48 * MIB
EXPERT_VMEM_LIMIT = 56 * MIB


def _params(n_axes=1, vmem_limit=VMEM_LIMIT):
    return pltpu.CompilerParams(dimension_semantics=("arbitrary",) * n_axes,
                                vmem_limit_bytes=vmem_limit)


def _full(a):
    nd = a.ndim
    return pl.BlockSpec(a.shape, lambda *_: (0,) * nd)


def _ln_valid(x, g, b, n_valid):
    col = lax.broadcasted_iota(jnp.int32, x.shape, 1)
    ok = col < n_valid
    mu = jnp.sum(jnp.where(ok, x, 0.0), axis=-1, keepdims=True) / n_valid
    xc = jnp.where(ok, x - mu, 0.0)
    var = jnp.sum(xc * xc, axis=-1, keepdims=True) / n_valid
    return xc * lax.rsqrt(var + LN_EPS) * g + b


def _ln_full(x, g, b):
    mu = jnp.mean(x, axis=-1, keepdims=True)
    xc = x - mu
    var = jnp.mean(xc * xc, axis=-1, keepdims=True)
    return xc * lax.rsqrt(var + LN_EPS) * g + b


def _spatial_gate(vn, cu, w_ref, bfull, rows, sub):
    r = lax.broadcasted_iota(jnp.int32, (rows, rows), 0)
    c = lax.broadcasted_iota(jnp.int32, (rows, rows), 1)
    causal = c <= r
    if sub is not None:
        causal = jnp.logical_and(causal, lax.shift_right_logical(r, sub) == lax.shift_right_logical(c, sub))
        pos = lax.broadcasted_iota(jnp.int32, (rows, CHUNK), 0) & (2 ** sub - 1)
        spread = (pos == lax.broadcasted_iota(jnp.int32, (rows, CHUNK), 1)).astype(BF16)
        bfull = jnp.concatenate([bfull[0:2 ** sub]] * (rows // 2 ** sub), axis=0)
    lo_half = lax.broadcasted_iota(jnp.int32, (rows, LANES), 1) < HEAD_DIM
    cols = []
    for pair in range(SEG // LANES):
        cs = slice(pair * LANES, (pair + 1) * LANES)
        acc = jnp.zeros((rows, LANES), F32)
        for hh in range(2):
            g = pair * 2 + hh
            if g >= N_GROUPS_C:
                continue
            keep = lo_half if hh == 0 else jnp.logical_not(lo_half)
            w = w_ref[g].astype(BF16)
            if sub is not None:
                w = jnp.dot(spread, w, preferred_element_type=F32).astype(BF16)
                w = lax.dot_general(w, spread, (((1,), (1,)), ((), ())), preferred_element_type=F32).astype(BF16)
            wm = jnp.where(causal, w, 0.0)
            vm = jnp.where(keep, vn[:, cs], 0.0).astype(BF16)
            acc = acc + jnp.dot(wm, vm, preferred_element_type=F32)
        cols.append(cu[:, cs] * (acc + bfull[:, cs]))
    return jnp.concatenate(cols, axis=1)


def _tile_rows(xp_ref, xs_ref, n_prompt):
    return jnp.where(pl.program_id(0) < n_prompt, xp_ref[...], xs_ref[...])


def _tile_row_specs(n_prompt):
    return [pl.BlockSpec((TT, D_MODEL), lambda i, *_: (jnp.minimum(i, n_prompt - 1), 0)),
            pl.BlockSpec((TT, D_MODEL), lambda i, *_: (jnp.maximum(i - n_prompt, 0), 0))]


def _proj_kernel(xp_ref, xs_ref, w_ref, b_ref, lcg_ref, lcb_ref, wsp_ref, bsp_ref,
                 q_ref, k_ref, v_ref, a_ref, c_ref, vns_ref, *, n_prompt):
    x = _tile_rows(xp_ref, xs_ref, n_prompt)
    proj = jnp.dot(x.astype(BF16), w_ref[...], preferred_element_type=F32) + b_ref[...]
    seg = lambda i: proj[:, i * SEG:(i + 1) * SEG]

    for i, ref in enumerate((q_ref, k_ref, v_ref)):
        qkv = seg(i)
        for pair in range(N_PAIRS):
            ref[pair] = qkv[:, pair * LANES:(pair + 1) * LANES]
    a_ref[...] = seg(3) * jax.nn.sigmoid(seg(4))
    cu = seg(5)
    vn = _ln_valid(seg(6), lcg_ref[...], lcb_ref[...], C_C)
    is_prompt = pl.program_id(0) < n_prompt

    @pl.when(is_prompt)
    def _():
        for ch in range(0, TT, CHUNK):
            rows = slice(ch, ch + CHUNK)
            c_ref[rows, :] = _spatial_gate(vn[rows], cu[rows], wsp_ref, bsp_ref[...], CHUNK, None)

    @pl.when(jnp.logical_not(is_prompt))
    def _():
        c_ref[...] = _spatial_gate(vn, cu, wsp_ref, bsp_ref[...], TT, DEC_SEQ.bit_length() - 1)
        vns_ref[...] = vn


def _proj_call(x, w_in_p, b_in_p, lcg, lcb, w_sp, b_full, n, tp):
    row = lambda w: pl.BlockSpec((TT, w), lambda i: (i, 0))
    heads = pl.BlockSpec((N_PAIRS, TT, LANES), lambda i: (0, i, 0))
    out = jax.ShapeDtypeStruct((n, SEG), F32)
    out3 = jax.ShapeDtypeStruct((N_PAIRS, n, LANES), F32)
    consts = (w_in_p, b_in_p, lcg, lcb, w_sp, b_full)
    return pl.pallas_call(
        functools.partial(_proj_kernel, n_prompt=tp // TT), grid=(n // TT,),
        in_specs=_tile_row_specs(tp // TT) + [_full(c) for c in consts],
        out_specs=[heads] * 3 + [row(SEG)] * 2 + [pl.BlockSpec((TT, SEG), lambda i: (0, 0))],
        out_shape=[out3] * 3 + [out] * 2 + [jax.ShapeDtypeStruct((n - tp, SEG), F32)],
        compiler_params=_params(), name="in_proj")(*x, *consts)


HALO = 32
CONV_CHUNK = 32


def _conv_tail(acc, bdw_ref, g_ref, b_ref):
    return jax.nn.silu(_ln_valid(acc + bdw_ref[...], g_ref[...], b_ref[...], C_A))


def _conv_prompt_kernel(halo_ref, a_ref, w_ref, bdw_ref, g_ref, b_ref, o_ref, buf, shifted, *, n_prompt):
    i = pl.program_id(0)

    @pl.when(i < n_prompt)
    def _():
        buf[0:HALO, :] = jnp.where(i > 0, halo_ref[...], 0.0)
        buf[HALO:, :] = a_ref[...]
        n_rows = HALO + TT - SUBLANES
        for sh in range(1, SUBLANES):
            shifted[sh - 1, 0:n_rows, :] = buf[sh:sh + n_rows, :]
        base = HALO - (CONV_W - 1)
        for r in range(0, TT, CONV_CHUNK):
            acc = jnp.zeros((CONV_CHUNK, SEG), F32)
            for j in range(CONV_W):
                tiles, sh = divmod(base + j, SUBLANES)
                src = buf if sh == 0 else shifted.at[sh - 1]
                lo = r + tiles * SUBLANES
                acc = acc + w_ref[j:j + 1, :] * src[lo:lo + CONV_CHUNK, :]
            o_ref[r:r + CONV_CHUNK, :] = _conv_tail(acc, bdw_ref, g_ref, b_ref)

    @pl.when(i >= n_prompt)
    def _():
        o_ref[...] = jnp.zeros_like(o_ref)


def _conv_prompt_call(a, w_dw_p, bdw, g, b, tp):
    n = a.shape[0]
    per = TT // HALO
    halo = pl.BlockSpec((HALO, SEG), lambda i: (jnp.maximum(i * per - 1, 0), 0))
    row = pl.BlockSpec((TT, SEG), lambda i: (i, 0))
    return pl.pallas_call(
        functools.partial(_conv_prompt_kernel, n_prompt=tp // TT), grid=(n // TT,),
        in_specs=[halo, row, _full(w_dw_p), _full(bdw), _full(g), _full(b)],
        out_specs=row, out_shape=jax.ShapeDtypeStruct((n, SEG), F32),
        scratch_shapes=[pltpu.VMEM((HALO + TT, SEG), F32),
                        pltpu.VMEM((SUBLANES - 1, HALO + TT, SEG), F32)],
        compiler_params=_params(), name="conv_prompt")(a, a, w_dw_p, bdw, g, b)


SB = 8


def _conv_sample_kernel(hist_ref, a_ref, w_ref, bdw_ref, g_ref, b_ref, big_ref,
                        o_ref, newa_ref, buf):
    del big_ref
    bs = hist_ref.shape[0]
    buf[:, 0:HALO, :] = hist_ref[...]
    buf[:, HALO:, :] = a_ref[...].reshape(bs, DEC_SEQ, SEG)
    base = HALO - (CONV_W - 1)
    for s in range(0, bs, SB):
        acc = jnp.zeros((SB, DEC_SEQ, SEG), F32)
        for j in range(CONV_W):
            acc = acc + w_ref[j:j + 1, :][None] * buf[s:s + SB, base + j: base + j + DEC_SEQ, :]
        y = _conv_tail(acc.reshape(SB * DEC_SEQ, SEG), bdw_ref, g_ref, b_ref)
        o_ref[s * DEC_SEQ:(s + SB) * DEC_SEQ, :] = y
    newa_ref[...] = buf[:, DEC_SEQ:, :]


def _conv_sample_call(hist_p, a, w_dw_p, bdw, g, b, a_out, tp):
    bs = hist_p.shape[0]
    rows = bs * DEC_SEQ
    blk = pl.BlockSpec((rows, SEG), lambda i: (tp // rows, 0))
    return pl.pallas_call(
        _conv_sample_kernel, grid=(1,),
        in_specs=[_full(hist_p), blk, _full(w_dw_p), _full(bdw), _full(g), _full(b),
                  pl.BlockSpec(memory_space=pl.ANY)],
        out_specs=[blk, pl.BlockSpec((bs, HALO, SEG), lambda i: (0, 0, 0))],
        out_shape=[jax.ShapeDtypeStruct(a_out.shape, F32),
                   jax.ShapeDtypeStruct((bs, HALO, SEG), F32)],
        scratch_shapes=[pltpu.VMEM((bs, HALO + DEC_SEQ, SEG), F32)],
        input_output_aliases={6: 0},
        compiler_params=_params(), name="conv_sample")(hist_p, a, w_dw_p, bdw, g, b, a_out)


QB = 128
assert QB == BAND


SPAN = BAND * max(DILATIONS)
UNITS = SPAN // QB
UNIT_UNROLL = 8


def _attn_prompt_kernel(q_ref, kh_ref, kc_ref, vh_ref, vc_ref, b1_ref, b4_ref, b16_ref, o_ref,
                        kbuf, vbuf, oacc, lacc, *, n_prompt):
    is_prompt = pl.program_id(1) < n_prompt
    refs = (q_ref, kh_ref, kc_ref, vh_ref, vc_ref, (b1_ref, b4_ref, b16_ref), o_ref, kbuf, vbuf, oacc, lacc)
    pl.when(is_prompt)(functools.partial(_attn_prompt_span, refs))

    @pl.when(jnp.logical_not(is_prompt))
    def _():
        o_ref[...] = jnp.zeros_like(o_ref)


def _attn_prompt_span(refs):
    q_ref, kh_ref, kc_ref, vh_ref, vc_ref, bias_refs, o_ref, kbuf, vbuf, oacc, lacc = refs
    pair = pl.program_id(0)
    first_span = pl.program_id(1) == 0
    kbuf[0:SPAN, :] = kh_ref[0]
    kbuf[SPAN:, :] = kc_ref[0]
    vbuf[0:SPAN, :] = vh_ref[0]
    vbuf[SPAN:, :] = vc_ref[0]
    lane = lax.broadcasted_iota(jnp.int32, (QB, LANES), 1)
    lo_half = lane < HEAD_DIM

    for dil, bias_ref in zip(DILATIONS, bias_refs):
        def unit(u, carry, dil=dil, bias_ref=bias_ref):
            sub = u // dil
            t0 = sub * (QB * dil) + (u - sub * dil)
            if dil == 1:
                rows_q = pl.ds(pl.multiple_of(t0, QB), QB)
                rows_k = pl.ds(pl.multiple_of(SPAN + t0 - QB, QB), 2 * QB)
            else:
                rows_q = pl.ds(t0, QB, stride=dil)
                rows_k = pl.ds(SPAN + t0 - QB * dil, 2 * QB, stride=dil)
            table = jnp.logical_and(first_span, sub == 0).astype(jnp.int32) * N_HEADS_B + pair * 2
            qp = q_ref[0, rows_q, :] * (HEAD_DIM ** -0.5 * LOG2E)
            kk = kbuf[rows_k, :].astype(BF16)
            vv = vbuf[rows_k, :].astype(BF16)
            outs, lses = [], []
            for hh in range(2):
                keep = lo_half if hh == 0 else jnp.logical_not(lo_half)
                qm = jnp.where(keep, qp, 0.0).astype(BF16)
                s = lax.dot_general(qm, kk, (((1,), (1,)), ((), ())), preferred_element_type=F32)
                s = s + bias_ref[table + hh]
                m = jnp.max(s, axis=-1, keepdims=True)
                p = jnp.exp2(s - m)
                l = jnp.sum(p, axis=-1, keepdims=True)
                pv = jnp.dot(p.astype(BF16), vv, preferred_element_type=F32)
                outs.append(pv / l)
                lses.append(m + jnp.log2(l))
            o_new = jnp.where(lo_half, outs[0], outs[1])
            l_new = jnp.where(lo_half, lses[0], lses[1])
            if dil != 1:
                l_old = lacc[rows_q, :]
                mx = jnp.maximum(l_old, l_new)
                w_old = jnp.exp2(l_old - mx)
                w_new = jnp.exp2(l_new - mx)
                tot = w_old + w_new
                o_new = (w_old * oacc[rows_q, :] + w_new * o_new) / tot
                l_new = mx + jnp.log2(tot)
            oacc[rows_q, :] = o_new
            lacc[rows_q, :] = l_new
            return carry

        lax.fori_loop(0, UNITS, unit, 0, unroll=UNIT_UNROLL)
    o_ref[0] = oacc[...]


def _attn_prompt_call(q, k, v, biases, tp):
    n = q.shape[1]
    cur = pl.BlockSpec((1, SPAN, LANES), lambda p, m: (p, m, 0))
    halo = pl.BlockSpec((1, SPAN, LANES), lambda p, m: (p, jnp.maximum(m - 1, 0), 0))
    scratch = [pltpu.VMEM((2 * SPAN, LANES), F32), pltpu.VMEM((2 * SPAN, LANES), F32),
               pltpu.VMEM((SPAN, LANES), F32), pltpu.VMEM((SPAN, LANES), F32)]
    return pl.pallas_call(
        functools.partial(_attn_prompt_kernel, n_prompt=tp // SPAN),
        grid=(N_PAIRS, pl.cdiv(n, SPAN)),
        in_specs=[cur, halo, cur, halo, cur] + [_full(b) for b in biases],
        out_specs=cur, out_shape=jax.ShapeDtypeStruct((N_PAIRS, n, LANES), F32),
        scratch_shapes=scratch, compiler_params=_params(2), name="attn_prompt")(q, k, k, v, v, *biases)


def _alibi_slopes():
    return [2.0 ** (-8.0 * (h + 1) / N_HEADS_B) for h in range(N_HEADS_B)]


def _prompt_bias(dil):
    i = np.arange(QB)[:, None]
    j = np.arange(2 * QB)[None, :]
    rel = i + QB - j
    ok = np.logical_and(rel >= 0, rel <= BAND)
    slopes = np.asarray(_alibi_slopes(), np.float32)[:, None, None]
    pen = (-slopes * (rel * dil).astype(np.float32)[None]) * np.float32(LOG2E)
    full = np.where(ok[None], pen, np.float32(NEG)).astype(np.float32)
    start = np.where((j >= QB)[None], full, np.float32(NEG)).astype(np.float32)
    return jnp.asarray(np.concatenate([full, start], axis=0))


def _branch_multiplicity(dist):
    cnt = np.zeros(dist.shape, np.int32)
    for dil in DILATIONS:
        cnt = cnt + np.logical_and(dist % dil == 0, dist <= BAND * dil).astype(np.int32)
    return cnt


def _sample_bias(n_cache):
    t = np.arange(DEC_SEQ)[:, None]
    dist_c = n_cache + t - np.arange(n_cache)[None, :]
    dist_n = t - np.arange(LANES)[None, :]
    ok_n = np.logical_and(dist_n >= 0, np.arange(LANES)[None, :] < DEC_SEQ)
    slopes = np.asarray(_alibi_slopes(), np.float32)[:, None, None]

    def bias(dist, ok):
        mult = _branch_multiplicity(np.maximum(dist, 0))
        ok = np.logical_and(ok, mult > 0)
        val = -slopes * dist.astype(np.float32)[None] + np.log(np.maximum(mult, 1).astype(np.float32))[None]
        return np.where(ok[None], val, np.float32(NEG)).astype(np.float32)

    bc = bias(dist_c, np.ones(dist_c.shape, bool)).reshape(N_HEADS_B * DEC_SEQ, n_cache)
    bn = bias(dist_n, ok_n).reshape(N_HEADS_B * DEC_SEQ, LANES)
    return jnp.asarray(bc), jnp.asarray(bn)


def _attn_sample_kernel(q_ref, kn_ref, vn_ref, kc_ref, vc_ref, bc_ref, bn_ref, big_ref, o_ref):
    del big_ref
    rows = N_HEADS_B * DEC_SEQ
    wide = lambda ref: jnp.concatenate([ref[pair] for pair in range(N_PAIRS)], axis=1)
    q = wide(q_ref) * (HEAD_DIM ** -0.5)
    qrep = jnp.concatenate([q] * N_HEADS_B, axis=0)
    rh = lax.shift_right_logical(lax.broadcasted_iota(jnp.int32, (rows, SEG), 0), 3)
    ch = lax.shift_right_logical(lax.broadcasted_iota(jnp.int32, (rows, SEG), 1), 6)
    own = rh == ch
    qm = jnp.where(own, qrep, 0.0).astype(BF16)
    pad = jnp.zeros((LANES - DEC_SEQ, SEG), F32)
    kn = jnp.concatenate([wide(kn_ref), pad], axis=0).astype(BF16)
    vn = jnp.concatenate([wide(vn_ref), pad], axis=0).astype(BF16)
    nt = (((1,), (1,)), ((), ()))
    s_c = jnp.dot(qm, kc_ref[0, 0].astype(BF16), preferred_element_type=F32) + bc_ref[...]
    s_n = lax.dot_general(qm, kn, nt, preferred_element_type=F32) + bn_ref[...]
    m = jnp.maximum(jnp.max(s_c, axis=-1, keepdims=True), jnp.max(s_n, axis=-1, keepdims=True))
    p_c = jnp.exp(s_c - m)
    p_n = jnp.exp(s_n - m)
    l = jnp.sum(p_c, axis=-1, keepdims=True) + jnp.sum(p_n, axis=-1, keepdims=True)
    r = (lax.dot_general(p_c.astype(BF16), vc_ref[0, 0].astype(BF16), nt, preferred_element_type=F32)
         + jnp.dot(p_n.astype(BF16), vn, preferred_element_type=F32)) / l
    r = jnp.where(own, r, 0.0)
    o = r[0:DEC_SEQ]
    for h in range(1, N_HEADS_B):
        o = o + r[h * DEC_SEQ:(h + 1) * DEC_SEQ]
    for pair in range(N_PAIRS):
        o_ref[pair] = o[:, pair * LANES:(pair + 1) * LANES]


def _attn_sample_call(q, k, v, cache_kt, cache_vt, layer, bc, bn, o_big, tp):
    bs, n_cache = cache_kt.shape[1], cache_kt.shape[3]
    new = pl.BlockSpec((N_PAIRS, DEC_SEQ, LANES), lambda b: (0, tp // DEC_SEQ + b, 0))
    cache = pl.BlockSpec((1, 1, C_B, n_cache), lambda b: (layer, b, 0, 0))
    return pl.pallas_call(
        _attn_sample_kernel, grid=(bs,),
        in_specs=[new, new, new, cache, cache, _full(bc), _full(bn),
                  pl.BlockSpec(memory_space=pl.ANY)],
        out_specs=new, out_shape=jax.ShapeDtypeStruct(o_big.shape, F32),
        input_output_aliases={7: 0},
        compiler_params=_params(), name="attn_sample")(q, k, v, cache_kt, cache_vt, bc, bn, o_big)


def _mix_kernel(xp_ref, xs_ref, a_ref, o_ref, c_ref, wo_ref, bo_ref, g1_ref, b1_ref, wr_ref, br_ref,
                h_ref, xs3_ref, aux_ref, cnt_ref, *, n_prompt):
    mixed = jnp.concatenate([a_ref[...]] + [o_ref[pair] for pair in range(N_PAIRS)] + [c_ref[...]], axis=1)
    mix = jnp.dot(mixed.astype(BF16), wo_ref[...], preferred_element_type=F32) + bo_ref[...]
    h = _ln_full(DN_ALPHA * _tile_rows(xp_ref, xs_ref, n_prompt) + mix, g1_ref[...], b1_ref[...])
    h_ref[...] = h

    hb = h.astype(BF16)
    h_lo = (h - hb.astype(F32)).astype(BF16)
    both = jnp.dot(hb, wr_ref[...], preferred_element_type=F32)
    logits = (both[:, :LANES] + both[:, LANES:]
              + jnp.dot(h_lo, wr_ref[:, :LANES], preferred_element_type=F32) + br_ref[...])
    lane = lax.broadcasted_iota(jnp.int32, (TT, LANES), 1)
    lane_f = lane.astype(F32)
    vals, idxs, sels = [], [], []
    cur = logits
    for _ in range(TOP_K):
        m = jnp.max(cur, axis=-1, keepdims=True)
        idx = jnp.min(jnp.where(cur == m, lane_f, float(LANES)), axis=-1, keepdims=True)
        sel = lane_f == idx
        vals.append(m)
        idxs.append(idx)
        sels.append(sel)
        cur = jnp.where(sel, -jnp.inf, cur)
    exps = [jnp.exp(v - vals[0]) for v in vals]
    den = exps[0] + exps[1] + exps[2] + exps[3]

    onehot = jnp.zeros((TT, LANES), F32)
    for sel in sels:
        onehot = onehot + sel.astype(F32)
    r = lax.broadcasted_iota(jnp.int32, (TT, TT), 0)
    c = lax.broadcasted_iota(jnp.int32, (TT, TT), 1)
    below = (c < r).astype(BF16)
    earlier = jnp.dot(below, onehot.astype(BF16), preferred_element_type=F32)
    cnt = jnp.broadcast_to(jnp.sum(onehot, axis=0, keepdims=True), (SUBLANES, LANES))
    er = lax.broadcasted_iota(jnp.int32, (LANES, LANES), 0)
    ec = lax.broadcasted_iota(jnp.int32, (LANES, LANES), 1)
    off = jnp.dot(cnt.astype(BF16), (er < ec).astype(BF16), preferred_element_type=F32)[0:1]
    place = earlier + off
    aux = jnp.zeros((TT, LANES), F32)
    rows = []
    for k in range(TOP_K):
        row = jnp.sum(jnp.where(sels[k], place, 0.0), axis=-1, keepdims=True)
        rows.append(row)
        aux = aux + jnp.where(lane == k, row, 0.0) + jnp.where(lane == TOP_K + k, exps[k] / den, 0.0)
    aux_ref[...] = aux
    cnt_ref[0] = cnt.astype(jnp.int32)

    aux_t = jnp.transpose(aux)
    dest = lax.broadcasted_iota(jnp.int32, (TOP_K * TT, TT), 0).astype(F32)
    disp = jnp.zeros((TOP_K * TT, TT), F32)
    for k in range(TOP_K):
        disp = disp + (dest == aux_t[k:k + 1, :]).astype(F32)
    xs = jnp.dot(disp.astype(BF16), hb, preferred_element_type=F32)
    for s in range(ROW_TILES):
        xs3_ref[pl.ds(s, TOP_K * TT, stride=SUBLANES), :] = xs[:, s * LANES:(s + 1) * LANES]


def _mix_call(x, a_out, o_b, c_out, wo, bo, g1, b1, wr, br, tp):
    n = a_out.shape[0]
    row = lambda w: pl.BlockSpec((TT, w), lambda i: (i, 0))
    xs3 = pl.BlockSpec((TOP_K * TT * SUBLANES, LANES), lambda i: (i, 0))
    cnt = pl.BlockSpec((1, SUBLANES, LANES), lambda i: (i, 0, 0))
    return pl.pallas_call(
        functools.partial(_mix_kernel, n_prompt=tp // TT), grid=(n // TT,),
        in_specs=_tile_row_specs(tp // TT)
        + [row(SEG), pl.BlockSpec((N_PAIRS, TT, LANES), lambda i: (0, i, 0)), row(SEG),
           _full(wo), _full(bo), _full(g1), _full(b1), _full(wr), _full(br)],
        out_specs=[row(D_MODEL), xs3, row(LANES), cnt],
        out_shape=[jax.ShapeDtypeStruct((n, D_MODEL), F32),
                   jax.ShapeDtypeStruct((n * TOP_K * SUBLANES, LANES), F32),
                   jax.ShapeDtypeStruct((n, LANES), F32),
                   jax.ShapeDtypeStruct((n // TT, SUBLANES, LANES), jnp.int32)],
        compiler_params=_params(), name="mix_ln_router")(
            *x, a_out, o_b, c_out, wo, bo, g1, b1, wr, br)


def _rows(start_row, n_rows):
    return pl.ds(pl.multiple_of(start_row * SUBLANES, SUBLANES), n_rows * SUBLANES)


def _expert_gather(tabs, blk, slot, xs3_hbm, buf, sem):
    blk_e, blk_s0, j_lo, j_hi, cnt, off, cum, tot = tabs
    e = blk_e[blk]
    s0 = blk_s0[blk]
    base = slot * TM

    @pl.when(tot[e] - s0 < TM)
    def _():
        buf[_rows(base, TM), :] = jnp.zeros((TM * SUBLANES, LANES), F32)

    def body(j, carry):
        run = cum[j * N_EXPERTS + e]
        lo = jnp.maximum(run, s0)
        hi = jnp.minimum(run + cnt[j * N_EXPERTS + e], s0 + TM)

        @pl.when(hi > lo)
        def _():
            src = j * (TOP_K * TT) + off[j * N_EXPERTS + e] + (lo - run)
            pltpu.make_async_copy(xs3_hbm.at[_rows(src, hi - lo)],
                                  buf.at[_rows(base + lo - s0, hi - lo)], sem.at[slot]).start()
        return carry

    lax.fori_loop(j_lo[blk], j_hi[blk], body, 0)


def _expert_gather_wait(tabs, blk, slot, xs3_hbm, buf, sem):
    blk_e, blk_s0, _, _, _, _, _, tot = tabs
    valid = jnp.minimum(tot[blk_e[blk]] - blk_s0[blk], TM)
    pltpu.make_async_copy(xs3_hbm.at[_rows(0, valid)], buf.at[_rows(slot * TM, valid)],
                          sem.at[slot]).wait()


def _expert_kernel(blk_e, blk_s0, j_lo, j_hi, cnt, off, cum, tot, n_used_ref,
                   xs3_hbm, w1_ref, b1_ref, w2_ref, b2_ref, y3_ref, buf, xb, w1b, w2b, sem):
    tabs = (blk_e, blk_s0, j_lo, j_hi, cnt, off, cum, tot)
    b = pl.program_id(0)
    n_used = n_used_ref[0]
    slot = lax.rem(b, 2)
    used = b < n_used

    @pl.when(b == 0)
    def _():
        _expert_gather(tabs, 0, 0, xs3_hbm, buf, sem)

    @pl.when(used)
    def _():
        _expert_gather_wait(tabs, b, slot, xs3_hbm, buf, sem)

    @pl.when(b + 1 < n_used)
    def _():
        _expert_gather(tabs, b + 1, 1 - slot, xs3_hbm, buf, sem)

    new_expert = jnp.logical_or(b == 0, blk_e[b] != blk_e[jnp.maximum(b - 1, 0)])

    @pl.when(jnp.logical_and(used, new_expert))
    def _():
        w1b[...] = w1_ref[0].astype(BF16)
        w2b[...] = w2_ref[0].astype(BF16)

    @pl.when(used)
    def _():
        base = slot * (TM * SUBLANES)
        for s in range(ROW_TILES):
            xb[:, s * LANES:(s + 1) * LANES] = buf[pl.ds(base + s, TM, stride=SUBLANES), :].astype(BF16)
        x = xb[...]
        y = jnp.broadcast_to(b2_ref[0], (TM, D_MODEL))
        for c in range(0, D_FF, FF_CHUNK):
            cols = slice(c, c + FF_CHUNK)
            ucols = slice(D_FF + c, D_FF + c + FF_CHUNK)
            g = jnp.dot(x, w1b[:, cols], preferred_element_type=F32) + b1_ref[0, :, cols]
            u = jnp.dot(x, w1b[:, ucols], preferred_element_type=F32) + b1_ref[0, :, ucols]
            gate = jnp.minimum(g, SWIGLU_LIMIT)
            up = jnp.clip(u, -SWIGLU_LIMIT, SWIGLU_LIMIT)
            hh = (up + 1.0) * (gate * jax.nn.sigmoid(SWIGLU_ALPHA * gate))
            y = y + jnp.dot(hh.astype(BF16), w2b[cols, :], preferred_element_type=F32)
        for s in range(ROW_TILES):
            y3_ref[pl.ds(s, TM, stride=SUBLANES), :] = y[:, s * LANES:(s + 1) * LANES]

    @pl.when(jnp.logical_not(used))
    def _():
        y3_ref[...] = jnp.zeros_like(y3_ref)


def _expert_call(tabs, n_used, xs3, w1, b1, w2, b2, layer):
    n_blocks = tabs[0].shape[0]
    by_expert = lambda shape: pl.BlockSpec(shape, lambda b, e, *_: (layer * N_EXPERTS + e[b], 0, 0))
    grid_spec = pltpu.PrefetchScalarGridSpec(
        num_scalar_prefetch=len(tabs) + 1, grid=(n_blocks,),
        in_specs=[pl.BlockSpec(memory_space=pl.ANY),
                  by_expert((1, D_MODEL, 2 * D_FF)), by_expert((1, 1, 2 * D_FF)),
                  by_expert((1, D_FF, D_MODEL)), by_expert((1, 1, D_MODEL))],
        out_specs=pl.BlockSpec((TM * SUBLANES, LANES), lambda b, *_: (b, 0)),
        scratch_shapes=[pltpu.VMEM((2 * TM * SUBLANES, LANES), F32),
                        pltpu.VMEM((TM, D_MODEL), BF16),
                        pltpu.VMEM((D_MODEL, 2 * D_FF), BF16),
                        pltpu.VMEM((D_FF, D_MODEL), BF16),
                        pltpu.SemaphoreType.DMA((2,))])
    return pl.pallas_call(
        _expert_kernel, grid_spec=grid_spec,
        out_shape=jax.ShapeDtypeStruct((n_blocks * TM * SUBLANES, LANES), F32),
        compiler_params=_params(vmem_limit=EXPERT_VMEM_LIMIT), name="expert_ffn")(
            *tabs, n_used, xs3, w1, b1, w2, b2)


PAIRS = TOP_K * TT


def _combine_gather(cnt, off, src, tile, slot, y3_hbm, buf, sem):
    def body(e, carry):
        n = cnt[tile * N_EXPERTS + e]

        @pl.when(n > 0)
        def _():
            pltpu.make_async_copy(y3_hbm.at[_rows(src[tile * N_EXPERTS + e], n)],
                                  buf.at[_rows(slot * PAIRS + off[tile * N_EXPERTS + e], n)],
                                  sem.at[slot]).start()
        return carry

    lax.fori_loop(0, N_EXPERTS, body, 0)


def _combine_kernel(cnt, off, src, y3_hbm, h_ref, aux_ref, g2_ref, b2_ref, op_ref, os_ref, buf, ys, sem,
                    *, n_prompt):
    i = pl.program_id(0)
    n_steps = pl.num_programs(0)
    slot = lax.rem(i, 2)

    @pl.when(i == 0)
    def _():
        _combine_gather(cnt, off, src, 0, 0, y3_hbm, buf, sem)

    pltpu.make_async_copy(y3_hbm.at[_rows(0, PAIRS)], buf.at[_rows(slot * PAIRS, PAIRS)],
                          sem.at[slot]).wait()

    @pl.when(i + 1 < n_steps)
    def _():
        _combine_gather(cnt, off, src, i + 1, 1 - slot, y3_hbm, buf, sem)

    base = slot * (PAIRS * SUBLANES)
    for s in range(ROW_TILES):
        ys[:, s * LANES:(s + 1) * LANES] = buf[pl.ds(base + s, PAIRS, stride=SUBLANES), :].astype(BF16)
    aux = aux_ref[...]
    dest = lax.broadcasted_iota(jnp.int32, (TT, PAIRS), 1).astype(F32)
    weights = jnp.zeros((TT, PAIRS), F32)
    for k in range(TOP_K):
        weights = weights + jnp.where(dest == aux[:, k:k + 1], aux[:, TOP_K + k:TOP_K + k + 1], 0.0)
    moe = jnp.dot(weights.astype(BF16), ys[...], preferred_element_type=F32)
    y = _ln_full(DN_ALPHA * h_ref[...] + moe, g2_ref[...], b2_ref[...])

    @pl.when(i < n_prompt)
    def _():
        op_ref[...] = y

    @pl.when(i >= n_prompt)
    def _():
        os_ref[...] = y


def _combine_call(cnt, off, src, y3, h, aux, g2, b2, tp):
    n = h.shape[0]
    row = lambda w: pl.BlockSpec((TT, w), lambda i, *_: (i, 0))
    n_prompt = tp // TT
    out_specs = _tile_row_specs(n_prompt)
    out_shape = [jax.ShapeDtypeStruct((tp, D_MODEL), F32), jax.ShapeDtypeStruct((n - tp, D_MODEL), F32)]
    grid_spec = pltpu.PrefetchScalarGridSpec(
        num_scalar_prefetch=3, grid=(n // TT,),
        in_specs=[pl.BlockSpec(memory_space=pl.ANY), row(D_MODEL), row(LANES),
                  pl.BlockSpec(g2.shape, lambda i, *_: (0, 0)), pl.BlockSpec(b2.shape, lambda i, *_: (0, 0))],
        out_specs=out_specs,
        scratch_shapes=[pltpu.VMEM((2 * PAIRS * SUBLANES, LANES), F32),
                        pltpu.VMEM((PAIRS, D_MODEL), BF16),
                        pltpu.SemaphoreType.DMA((2,))])
    return pl.pallas_call(
        functools.partial(_combine_kernel, n_prompt=n_prompt), grid_spec=grid_spec, out_shape=out_shape,
        compiler_params=_params(), name="combine_ln")(cnt, off, src, y3, h, aux, g2, b2)


def _routing_tables(cnt_tiles, n):
    cnt = cnt_tiles[:, 0, :N_EXPERTS]
    off = jnp.cumsum(cnt, axis=1) - cnt
    cum = jnp.cumsum(cnt, axis=0) - cnt
    tot = jnp.sum(cnt, axis=0)
    padded = (tot + TM - 1) // TM * TM
    pends = jnp.cumsum(padded)
    pstart = pends - padded
    n_blocks = n * TOP_K // TM + N_EXPERTS
    blk_start = jnp.arange(n_blocks, dtype=jnp.int32) * TM
    blk_e = jnp.minimum(jnp.sum((pends[None, :] <= blk_start[:, None]).astype(jnp.int32), axis=1),
                        N_EXPERTS - 1)
    blk_s0 = blk_start - pstart[blk_e]
    run_lo = cum[:, blk_e].T
    run_hi = run_lo + cnt[:, blk_e].T
    j_lo = jnp.sum((run_hi <= blk_s0[:, None]).astype(jnp.int32), axis=1)
    j_hi = jnp.sum((run_lo < blk_s0[:, None] + TM).astype(jnp.int32), axis=1)
    n_used = (pends[-1:] // TM).astype(jnp.int32)
    flat = lambda a: a.reshape(-1).astype(jnp.int32)
    expert_tabs = (flat(blk_e), flat(blk_s0), flat(j_lo), flat(j_hi), flat(cnt), flat(off), flat(cum), flat(tot))
    combine_tabs = (flat(cnt), flat(off), flat(pstart[None, :] + cum))
    return expert_tabs, n_used, combine_tabs


def _pad_cols(a, width):
    return jnp.pad(a, [(0, 0)] * (a.ndim - 1) + [(0, width - a.shape[-1])])


def _layer_params(l, w_in, b_in, w_dw, b_dw, ln_a_g, ln_a_b, ln_c_g, ln_c_b, w_sp, b_sp,
                  w_out, b_out, ln1_g, ln1_b, w_router, b_router, ln2_g, ln2_b):
    bounds = [0, C_B, 2 * C_B, 3 * C_B, 3 * C_B + C_A, 3 * C_B + 2 * C_A, 3 * C_B + 2 * C_A + C_C,
              3 * C_B + 2 * C_A + 2 * C_C]
    w_segs = [_pad_cols(w_in[l][:, bounds[i]:bounds[i + 1]], SEG) for i in range(N_SEG)]
    b_segs = [_pad_cols(b_in[l][bounds[i]:bounds[i + 1]], SEG) for i in range(N_SEG)]
    row = lambda a: _pad_cols(a, SEG)[None, :]
    wo = w_out[l]
    pad_rows = lambda a: jnp.pad(a, ((0, SEG - a.shape[0]), (0, 0)))
    bsp = b_sp[l]
    bfull = _pad_cols(jnp.repeat(bsp.T, HEAD_DIM, axis=1), SEG)
    wr = _pad_cols(w_router[l], LANES)
    wr_hi = wr.astype(BF16)
    return dict(
        w_in=jnp.concatenate(w_segs, axis=1).astype(BF16),
        b_in=jnp.concatenate(b_segs)[None, :],
        w_dw=jnp.pad(w_dw[l], ((0, HALO - CONV_W), (0, SEG - C_A))),
        b_dw=row(b_dw[l]), ln_a_g=row(ln_a_g[l]), ln_a_b=row(ln_a_b[l]),
        ln_c_g=row(ln_c_g[l]), ln_c_b=row(ln_c_b[l]),
        w_sp=w_sp[l], b_full=bfull,
        w_out=jnp.concatenate([pad_rows(wo[:C_A]), wo[C_A:C_A + C_B], pad_rows(wo[C_A + C_B:])]).astype(BF16),
        b_out=b_out[l][None, :],
        ln1_g=ln1_g[l][None, :], ln1_b=ln1_b[l][None, :],
        w_router=jnp.concatenate([wr_hi, (wr - wr_hi.astype(F32)).astype(BF16)], axis=1),
        b_router=jnp.concatenate([b_router[l], jnp.full((LANES - N_EXPERTS,), NEG, F32)])[None, :],
        ln2_g=ln2_g[l][None, :], ln2_b=ln2_b[l][None, :])


def _layer(x, p, hist_p, caches, experts, biases, n, tp, layer):
    q, k, v, a, c_out, vn_s = _proj_call(x, p["w_in"], p["b_in"], p["ln_c_g"], p["ln_c_b"],
                                         p["w_sp"], p["b_full"], n, tp)

    conv_args = (p["w_dw"], p["b_dw"], p["ln_a_g"], p["ln_a_b"])
    a_out = _conv_prompt_call(a, *conv_args, tp)
    a_out, new_a_s = _conv_sample_call(hist_p, a, *conv_args, a_out, tp)

    o_b = _attn_prompt_call(q, k, v, biases["prompt"], tp)
    o_b = _attn_sample_call(q, k, v, *caches, layer, *biases["sample"], o_b, tp)

    h, xs3, aux, cnt_tiles = _mix_call(x, a_out, o_b, c_out, p["w_out"], p["b_out"],
                                       p["ln1_g"], p["ln1_b"], p["w_router"], p["b_router"], tp)
    expert_tabs, n_used, combine_tabs = _routing_tables(cnt_tiles, n)
    y3 = _expert_call(expert_tabs, n_used, xs3, *experts, layer)
    y = _combine_call(*combine_tabs, y3, h, aux, p["ln2_g"], p["ln2_b"], tp)
    return y, a, k, v, vn_s, new_a_s


def kernel(x_prompt, x_sample, state_a_conv, cache_b_k, cache_b_v, w_in, b_in, w_dw, b_dw, ln_a_g, ln_a_b, ln_c_g, ln_c_b, w_sp, b_sp, w_out, b_out, ln1_g, ln1_b, w_router, b_router, w1, b1, w2, b2, ln2_g, ln2_b):
    bp, tp, _ = x_prompt.shape
    bs, ts, _ = x_sample.shape
    n_cache = cache_b_k.shape[2]
    assert bp == 1 and ts == DEC_SEQ and bs * ts == TT and tp % WIN_MAX == 0
    assert n_cache == WIN_MAX
    n = tp + bs * ts
    keep = min(WIN_MAX, tp)
    hist = CONV_W - 1

    weights = (w_in, b_in, w_dw, b_dw, ln_a_g, ln_a_b, ln_c_g, ln_c_b, w_sp, b_sp, w_out, b_out,
               ln1_g, ln1_b, w_router, b_router, ln2_g, ln2_b)
    biases = {"prompt": [_prompt_bias(d) for d in DILATIONS], "sample": _sample_bias(n_cache)}
    experts = (w1.reshape(DEPTH * N_EXPERTS, D_MODEL, 2 * D_FF), b1.reshape(DEPTH * N_EXPERTS, 1, 2 * D_FF),
               w2.reshape(DEPTH * N_EXPERTS, D_FF, D_MODEL), b2.reshape(DEPTH * N_EXPERTS, 1, D_MODEL))
    to_feature_major = lambda c: jnp.transpose(c, (0, 1, 3, 4, 2)).reshape(DEPTH, bs, C_B, n_cache)
    caches = (to_feature_major(cache_b_k), to_feature_major(cache_b_v))
    heads = lambda a: jnp.transpose(a, (1, 0, 2)).reshape(a.shape[1], N_HEADS_B, HEAD_DIM)

    x = (x_prompt[0], x_sample.reshape(bs * ts, D_MODEL))
    outs = {name: [] for name in ("a_p", "a_s", "k_p", "v_p", "k_s", "v_s", "c_s")}
    for l in range(DEPTH):
        p = _layer_params(l, *weights)
        hist_p = jnp.pad(state_a_conv[l], ((0, 0), (HALO - hist, 0), (0, SEG - C_A)))
        x, a, k, v, vn_s, new_a_s = _layer(x, p, hist_p, caches, experts, biases, n, tp, l)
        outs["a_p"].append(a[tp - hist:tp, :C_A][None])
        outs["a_s"].append(new_a_s[:, HALO - hist:, :C_A])
        outs["k_p"].append(heads(k[:, tp - keep:tp])[None])
        outs["v_p"].append(heads(v[:, tp - keep:tp])[None])
        outs["k_s"].append(heads(k[:, tp:]).reshape(bs, ts, N_HEADS_B, HEAD_DIM))
        outs["v_s"].append(heads(v[:, tp:]).reshape(bs, ts, N_HEADS_B, HEAD_DIM))
        outs["c_s"].append(vn_s[:, :C_C].reshape(bs, ts, C_C))
    stack = lambda name: jnp.stack(outs[name])
    y_prompt, y_sample = x
    return (y_prompt[None], y_sample.reshape(bs, ts, D_MODEL), stack("a_p"), stack("a_s"),
            stack("k_p"), stack("v_p"), stack("k_s"), stack("v_s"), stack("c_s"))
```

```python
import functools
import math

import jax
import jax.numpy as jnp
import numpy as np
from jax import lax
from jax.experimental import pallas as pl
from jax.experimental.pallas import tpu as pltpu

F32 = jnp.float32
BF16 = jnp.bfloat16

D_MODEL = 1024
HEAD_DIM = 64
N_HEADS_B = 6
C_B = N_HEADS_B * HEAD_DIM
C_A = 320
C_C = 320
N_GROUPS_C = 5
CONV_W = 31
DILATIONS = (1, 4, 16)
BAND = 128
WIN_MAX = 2048
CHUNK = 128
N_EXPERTS = 32
TOP_K = 4
D_FF = 1024
SWIGLU_LIMIT = 7.0
SWIGLU_ALPHA = 1.702
LN_EPS = 1e-5
DEPTH = 2
DN_ALPHA = (2.0 * DEPTH) ** 0.25
DEC_SEQ = 8

LANES = 128
SUBLANES = 8
SEG = 384
N_SEG = 7
N_PAIRS = C_B // LANES
ROW_TILES = D_MODEL // LANES
assert ROW_TILES == SUBLANES

TT = 256
TM = 512
FF_CHUNK = 512
NEG = -1e30
LOG2E = math.log2(math.e)
MIB = 1024 * 1024
VMEM_LIMIT = 48 * MIB
EXPERT_VMEM_LIMIT = 56 * MIB


def _params(n_axes=1, vmem_limit=VMEM_LIMIT):
    return pltpu.CompilerParams(dimension_semantics=("arbitrary",) * n_axes,
                                vmem_limit_bytes=vmem_limit)


def _full(a):
    nd = a.ndim
    return pl.BlockSpec(a.shape, lambda *_: (0,) * nd)


def _ln_valid(x, g, b, n_valid):
    col = lax.broadcasted_iota(jnp.int32, x.shape, 1)
    ok = col < n_valid
    mu = jnp.sum(jnp.where(ok, x, 0.0), axis=-1, keepdims=True) / n_valid
    xc = jnp.where(ok, x - mu, 0.0)
    var = jnp.sum(xc * xc, axis=-1, keepdims=True) / n_valid
    return xc * lax.rsqrt(var + LN_EPS) * g + b


def _ln_full(x, g, b):
    mu = jnp.mean(x, axis=-1, keepdims=True)
    xc = x - mu
    var = jnp.mean(xc * xc, axis=-1, keepdims=True)
    return xc * lax.rsqrt(var + LN_EPS) * g + b


def _spatial_gate(vn, cu, w_ref, bfull, rows, sub):
    r = lax.broadcasted_iota(jnp.int32, (rows, rows), 0)
    c = lax.broadcasted_iota(jnp.int32, (rows, rows), 1)
    causal = c <= r
    if sub is not None:
        causal = jnp.logical_and(causal, lax.shift_right_logical(r, sub) == lax.shift_right_logical(c, sub))
        pos = lax.broadcasted_iota(jnp.int32, (rows, CHUNK), 0) & (2 ** sub - 1)
        spread = (pos == lax.broadcasted_iota(jnp.int32, (rows, CHUNK), 1)).astype(BF16)
        bfull = jnp.concatenate([bfull[0:2 ** sub]] * (rows // 2 ** sub), axis=0)
    lo_half = lax.broadcasted_iota(jnp.int32, (rows, LANES), 1) < HEAD_DIM
    cols = []
    for pair in range(SEG // LANES):
        cs = slice(pair * LANES, (pair + 1) * LANES)
        acc = jnp.zeros((rows, LANES), F32)
        for hh in range(2):
            g = pair * 2 + hh
            if g >= N_GROUPS_C:
                continue
            keep = lo_half if hh == 0 else jnp.logical_not(lo_half)
            w = w_ref[g].astype(BF16)
            if sub is not None:
                w = jnp.dot(spread, w, preferred_element_type=F32).astype(BF16)
                w = lax.dot_general(w, spread, (((1,), (1,)), ((), ())), preferred_element_type=F32).astype(BF16)
            wm = jnp.where(causal, w, 0.0)
            vm = jnp.where(keep, vn[:, cs], 0.0).astype(BF16)
            acc = acc + jnp.dot(wm, vm, preferred_element_type=F32)
        cols.append(cu[:, cs] * (acc + bfull[:, cs]))
    return jnp.concatenate(cols, axis=1)


def _tile_rows(xp_ref, xs_ref, n_prompt):
    return jnp.where(pl.program_id(0) < n_prompt, xp_ref[...], xs_ref[...])


def _tile_row_specs(n_prompt):
    return [pl.BlockSpec((TT, D_MODEL), lambda i, *_: (jnp.minimum(i, n_prompt - 1), 0)),
            pl.BlockSpec((TT, D_MODEL), lambda i, *_: (jnp.maximum(i - n_prompt, 0), 0))]


def _proj_kernel(xp_ref, xs_ref, w_ref, b_ref, lcg_ref, lcb_ref, wsp_ref, bsp_ref,
                 q_ref, k_ref, v_ref, a_ref, c_ref, vns_ref, *, n_prompt):
    x = _tile_rows(xp_ref, xs_ref, n_prompt)
    proj = jnp.dot(x.astype(BF16), w_ref[...], preferred_element_type=F32) + b_ref[...]
    seg = lambda i: proj[:, i * SEG:(i + 1) * SEG]

    for i, ref in enumerate((q_ref, k_ref, v_ref)):
        qkv = seg(i)
        for pair in range(N_PAIRS):
            ref[pair] = qkv[:, pair * LANES:(pair + 1) * LANES]
    a_ref[...] = seg(3) * jax.nn.sigmoid(seg(4))
    cu = seg(5)
    vn = _ln_valid(seg(6), lcg_ref[...], lcb_ref[...], C_C)
    is_prompt = pl.program_id(0) < n_prompt

    @pl.when(is_prompt)
    def _():
        for ch in range(0, TT, CHUNK):
            rows = slice(ch, ch + CHUNK)
            c_ref[rows, :] = _spatial_gate(vn[rows], cu[rows], wsp_ref, bsp_ref[...], CHUNK, None)

    @pl.when(jnp.logical_not(is_prompt))
    def _():
        c_ref[...] = _spatial_gate(vn, cu, wsp_ref, bsp_ref[...], TT, DEC_SEQ.bit_length() - 1)
        vns_ref[...] = vn


def _proj_call(x, w_in_p, b_in_p, lcg, lcb, w_sp, b_full, n, tp):
    row = lambda w: pl.BlockSpec((TT, w), lambda i: (i, 0))
    heads = pl.BlockSpec((N_PAIRS, TT, LANES), lambda i: (0, i, 0))
    out = jax.ShapeDtypeStruct((n, SEG), F32)
    out3 = jax.ShapeDtypeStruct((N_PAIRS, n, LANES), F32)
    consts = (w_in_p, b_in_p, lcg, lcb, w_sp, b_full)
    return pl.pallas_call(
        functools.partial(_proj_kernel, n_prompt=tp // TT), grid=(n // TT,),
        in_specs=_tile_row_specs(tp // TT) + [_full(c) for c in consts],
        out_specs=[heads] * 3 + [row(SEG)] * 2 + [pl.BlockSpec((TT, SEG), lambda i: (0, 0))],
        out_shape=[out3] * 3 + [out] * 2 + [jax.ShapeDtypeStruct((n - tp, SEG), F32)],
        compiler_params=_params(), name="in_proj")(*x, *consts)


HALO = 32
CONV_CHUNK = 32


def _conv_tail(acc, bdw_ref, g_ref, b_ref):
    return jax.nn.silu(_ln_valid(acc + bdw_ref[...], g_ref[...], b_ref[...], C_A))


def _conv_prompt_kernel(halo_ref, a_ref, w_ref, bdw_ref, g_ref, b_ref, o_ref, buf, shifted, *, n_prompt):
    i = pl.program_id(0)

    @pl.when(i < n_prompt)
    def _():
        buf[0:HALO, :] = jnp.where(i > 0, halo_ref[...], 0.0)
        buf[HALO:, :] = a_ref[...]
        n_rows = HALO + TT - SUBLANES
        for sh in range(1, SUBLANES):
            shifted[sh - 1, 0:n_rows, :] = buf[sh:sh + n_rows, :]
        base = HALO - (CONV_W - 1)
        for r in range(0, TT, CONV_CHUNK):
            acc = jnp.zeros((CONV_CHUNK, SEG), F32)
            for j in range(CONV_W):
                tiles, sh = divmod(base + j, SUBLANES)
                src = buf if sh == 0 else shifted.at[sh - 1]
                lo = r + tiles * SUBLANES
                acc = acc + w_ref[j:j + 1, :] * src[lo:lo + CONV_CHUNK, :]
            o_ref[r:r + CONV_CHUNK, :] = _conv_tail(acc, bdw_ref, g_ref, b_ref)

    @pl.when(i >= n_prompt)
    def _():
        o_ref[...] = jnp.zeros_like(o_ref)


def _conv_prompt_call(a, w_dw_p, bdw, g, b, tp):
    n = a.shape[0]
    per = TT // HALO
    halo = pl.BlockSpec((HALO, SEG), lambda i: (jnp.maximum(i * per - 1, 0), 0))
    row = pl.BlockSpec((TT, SEG), lambda i: (i, 0))
    return pl.pallas_call(
        functools.partial(_conv_prompt_kernel, n_prompt=tp // TT), grid=(n // TT,),
        in_specs=[halo, row, _full(w_dw_p), _full(bdw), _full(g), _full(b)],
        out_specs=row, out_shape=jax.ShapeDtypeStruct((n, SEG), F32),
        scratch_shapes=[pltpu.VMEM((HALO + TT, SEG), F32),
                        pltpu.VMEM((SUBLANES - 1, HALO + TT, SEG), F32)],
        compiler_params=_params(), name="conv_prompt")(a, a, w_dw_p, bdw, g, b)


SB = 8


def _conv_sample_kernel(hist_ref, a_ref, w_ref, bdw_ref, g_ref, b_ref, big_ref,
                        o_ref, newa_ref, buf):
    del big_ref
    bs = hist_ref.shape[0]
    buf[:, 0:HALO, :] = hist_ref[...]
    buf[:, HALO:, :] = a_ref[...].reshape(bs, DEC_SEQ, SEG)
    base = HALO - (CONV_W - 1)
    for s in range(0, bs, SB):
        acc = jnp.zeros((SB, DEC_SEQ, SEG), F32)
        for j in range(CONV_W):
            acc = acc + w_ref[j:j + 1, :][None] * buf[s:s + SB, base + j: base + j + DEC_SEQ, :]
        y = _conv_tail(acc.reshape(SB * DEC_SEQ, SEG), bdw_ref, g_ref, b_ref)
        o_ref[s * DEC_SEQ:(s + SB) * DEC_SEQ, :] = y
    newa_ref[...] = buf[:, DEC_SEQ:, :]


def _conv_sample_call(hist_p, a, w_dw_p, bdw, g, b, a_out, tp):
    bs = hist_p.shape[0]
    rows = bs * DEC_SEQ
    blk = pl.BlockSpec((rows, SEG), lambda i: (tp // rows, 0))
    return pl.pallas_call(
        _conv_sample_kernel, grid=(1,),
        in_specs=[_full(hist_p), blk, _full(w_dw_p), _full(bdw), _full(g), _full(b),
                  pl.BlockSpec(memory_space=pl.ANY)],
        out_specs=[blk, pl.BlockSpec((bs, HALO, SEG), lambda i: (0, 0, 0))],
        out_shape=[jax.ShapeDtypeStruct(a_out.shape, F32),
                   jax.ShapeDtypeStruct((bs, HALO, SEG), F32)],
        scratch_shapes=[pltpu.VMEM((bs, HALO + DEC_SEQ, SEG), F32)],
        input_output_aliases={6: 0},
        compiler_params=_params(), name="conv_sample")(hist_p, a, w_dw_p, bdw, g, b, a_out)


QB = 128
assert QB == BAND


SPAN = BAND * max(DILATIONS)
UNITS = SPAN // QB
UNIT_UNROLL = 8


def _attn_prompt_kernel(q_ref, kh_ref, kc_ref, vh_ref, vc_ref, b1_ref, b4_ref, b16_ref, o_ref,
                        kbuf, vbuf, oacc, lacc, *, n_prompt):
    is_prompt = pl.program_id(1) < n_prompt
    refs = (q_ref, kh_ref, kc_ref, vh_ref, vc_ref, (b1_ref, b4_ref, b16_ref), o_ref, kbuf, vbuf, oacc, lacc)
    pl.when(is_prompt)(functools.partial(_attn_prompt_span, refs))

    @pl.when(jnp.logical_not(is_prompt))
    def _():
        o_ref[...] = jnp.zeros_like(o_ref)


def _attn_prompt_span(refs):
    q_ref, kh_ref, kc_ref, vh_ref, vc_ref, bias_refs, o_ref, kbuf, vbuf, oacc, lacc = refs
    pair = pl.program_id(0)
    first_span = pl.program_id(1) == 0
    kbuf[0:SPAN, :] = kh_ref[0]
    kbuf[SPAN:, :] = kc_ref[0]
    vbuf[0:SPAN, :] = vh_ref[0]
    vbuf[SPAN:, :] = vc_ref[0]
    lane = lax.broadcasted_iota(jnp.int32, (QB, LANES), 1)
    lo_half = lane < HEAD_DIM

    for dil, bias_ref in zip(DILATIONS, bias_refs):
        def unit(u, carry, dil=dil, bias_ref=bias_ref):
            sub = u // dil
            t0 = sub * (QB * dil) + (u - sub * dil)
            if dil == 1:
                rows_q = pl.ds(pl.multiple_of(t0, QB), QB)
                rows_k = pl.ds(pl.multiple_of(SPAN + t0 - QB, QB), 2 * QB)
            else:
                rows_q = pl.ds(t0, QB, stride=dil)
                rows_k = pl.ds(SPAN + t0 - QB * dil, 2 * QB, stride=dil)
            table = jnp.logical_and(first_span, sub == 0).astype(jnp.int32) * N_HEADS_B + pair * 2
            qp = q_ref[0, rows_q, :] * (HEAD_DIM ** -0.5 * LOG2E)
            kk = kbuf[rows_k, :].astype(BF16)
            vv = vbuf[rows_k, :].astype(BF16)
            outs, lses = [], []
            for hh in range(2):
                keep = lo_half if hh == 0 else jnp.logical_not(lo_half)
                qm = jnp.where(keep, qp, 0.0).astype(BF16)
                s = lax.dot_general(qm, kk, (((1,), (1,)), ((), ())), preferred_element_type=F32)
                s = s + bias_ref[table + hh]
                m = jnp.max(s, axis=-1, keepdims=True)
                p = jnp.exp2(s - m)
                l = jnp.sum(p, axis=-1, keepdims=True)
                pv = jnp.dot(p.astype(BF16), vv, preferred_element_type=F32)
                outs.append(pv / l)
                lses.append(m + jnp.log2(l))
            o_new = jnp.where(lo_half, outs[0], outs[1])
            l_new = jnp.where(lo_half, lses[0], lses[1])
            if dil != 1:
                l_old = lacc[rows_q, :]
                mx = jnp.maximum(l_old, l_new)
                w_old = jnp.exp2(l_old - mx)
                w_new = jnp.exp2(l_new - mx)
                tot = w_old + w_new
                o_new = (w_old * oacc[rows_q, :] + w_new * o_new) / tot
                l_new = mx + jnp.log2(tot)
            oacc[rows_q, :] = o_new
            lacc[rows_q, :] = l_new
            return carry

        lax.fori_loop(0, UNITS, unit, 0, unroll=UNIT_UNROLL)
    o_ref[0] = oacc[...]


def _attn_prompt_call(q, k, v, biases, tp):
    n = q.shape[1]
    cur = pl.BlockSpec((1, SPAN, LANES), lambda p, m: (p, m, 0))
    halo = pl.BlockSpec((1, SPAN, LANES), lambda p, m: (p, jnp.maximum(m - 1, 0), 0))
    scratch = [pltpu.VMEM((2 * SPAN, LANES), F32), pltpu.VMEM((2 * SPAN, LANES), F32),
               pltpu.VMEM((SPAN, LANES), F32), pltpu.VMEM((SPAN, LANES), F32)]
    return pl.pallas_call(
        functools.partial(_attn_prompt_kernel, n_prompt=tp // SPAN),
        grid=(N_PAIRS, pl.cdiv(n, SPAN)),
        in_specs=[cur, halo, cur, halo, cur] + [_full(b) for b in biases],
        out_specs=cur, out_shape=jax.ShapeDtypeStruct((N_PAIRS, n, LANES), F32),
        scratch_shapes=scratch, compiler_params=_params(2), name="attn_prompt")(q, k, k, v, v, *biases)


def _alibi_slopes():
    return [2.0 ** (-8.0 * (h + 1) / N_HEADS_B) for h in range(N_HEADS_B)]


def _prompt_bias(dil):
    i = np.arange(QB)[:, None]
    j = np.arange(2 * QB)[None, :]
    rel = i + QB - j
    ok = np.logical_and(rel >= 0, rel <= BAND)
    slopes = np.asarray(_alibi_slopes(), np.float32)[:, None, None]
    pen = (-slopes * (rel * dil).astype(np.float32)[None]) * np.float32(LOG2E)
    full = np.where(ok[None], pen, np.float32(NEG)).astype(np.float32)
    start = np.where((j >= QB)[None], full, np.float32(NEG)).astype(np.float32)
    return jnp.asarray(np.concatenate([full, start], axis=0))


def _branch_multiplicity(dist):
    cnt = np.zeros(dist.shape, np.int32)
    for dil in DILATIONS:
        cnt = cnt + np.logical_and(dist % dil == 0, dist <= BAND * dil).astype(np.int32)
    return cnt


def _sample_bias(n_cache):
    t = np.arange(DEC_SEQ)[:, None]
    dist_c = n_cache + t - np.arange(n_cache)[None, :]
    dist_n = t - np.arange(LANES)[None, :]
    ok_n = np.logical_and(dist_n >= 0, np.arange(LANES)[None, :] < DEC_SEQ)
    slopes = np.asarray(_alibi_slopes(), np.float32)[:, None, None]

    def bias(dist, ok):
        mult = _branch_multiplicity(np.maximum(dist, 0))
        ok = np.logical_and(ok, mult > 0)
        val = -slopes * dist.astype(np.float32)[None] + np.log(np.maximum(mult, 1).astype(np.float32))[None]
        return np.where(ok[None], val, np.float32(NEG)).astype(np.float32)

    bc = bias(dist_c, np.ones(dist_c.shape, bool)).reshape(N_HEADS_B * DEC_SEQ, n_cache)
    bn = bias(dist_n, ok_n).reshape(N_HEADS_B * DEC_SEQ, LANES)
    return jnp.asarray(bc), jnp.asarray(bn)


def _attn_sample_kernel(q_ref, kn_ref, vn_ref, kc_ref, vc_ref, bc_ref, bn_ref, big_ref, o_ref):
    del big_ref
    rows = N_HEADS_B * DEC_SEQ
    wide = lambda ref: jnp.concatenate([ref[pair] for pair in range(N_PAIRS)], axis=1)
    q = wide(q_ref) * (HEAD_DIM ** -0.5)
    qrep = jnp.concatenate([q] * N_HEADS_B, axis=0)
    rh = lax.shift_right_logical(lax.broadcasted_iota(jnp.int32, (rows, SEG), 0), 3)
    ch = lax.shift_right_logical(lax.broadcasted_iota(jnp.int32, (rows, SEG), 1), 6)
    own = rh == ch
    qm = jnp.where(own, qrep, 0.0).astype(BF16)
    pad = jnp.zeros((LANES - DEC_SEQ, SEG), F32)
    kn = jnp.concatenate([wide(kn_ref), pad], axis=0).astype(BF16)
    vn = jnp.concatenate([wide(vn_ref), pad], axis=0).astype(BF16)
    nt = (((1,), (1,)), ((), ()))
    s_c = jnp.dot(qm, kc_ref[0, 0].astype(BF16), preferred_element_type=F32) + bc_ref[...]
    s_n = lax.dot_general(qm, kn, nt, preferred_element_type=F32) + bn_ref[...]
    m = jnp.maximum(jnp.max(s_c, axis=-1, keepdims=True), jnp.max(s_n, axis=-1, keepdims=True))
    p_c = jnp.exp(s_c - m)
    p_n = jnp.exp(s_n - m)
    l = jnp.sum(p_c, axis=-1, keepdims=True) + jnp.sum(p_n, axis=-1, keepdims=True)
    r = (lax.dot_general(p_c.astype(BF16), vc_ref[0, 0].astype(BF16), nt, preferred_element_type=F32)
         + jnp.dot(p_n.astype(BF16), vn, preferred_element_type=F32)) / l
    r = jnp.where(own, r, 0.0)
    o = r[0:DEC_SEQ]
    for h in range(1, N_HEADS_B):
        o = o + r[h * DEC_SEQ:(h + 1) * DEC_SEQ]
    for pair in range(N_PAIRS):
        o_ref[pair] = o[:, pair * LANES:(pair + 1) * LANES]


def _attn_sample_call(q, k, v, cache_kt, cache_vt, layer, bc, bn, o_big, tp):
    bs, n_cache = cache_kt.shape[1], cache_kt.shape[3]
    new = pl.BlockSpec((N_PAIRS, DEC_SEQ, LANES), lambda b: (0, tp // DEC_SEQ + b, 0))
    cache = pl.BlockSpec((1, 1, C_B, n_cache), lambda b: (layer, b, 0, 0))
    return pl.pallas_call(
        _attn_sample_kernel, grid=(bs,),
        in_specs=[new, new, new, cache, cache, _full(bc), _full(bn),
                  pl.BlockSpec(memory_space=pl.ANY)],
        out_specs=new, out_shape=jax.ShapeDtypeStruct(o_big.shape, F32),
        input_output_aliases={7: 0},
        compiler_params=_params(), name="attn_sample")(q, k, v, cache_kt, cache_vt, bc, bn, o_big)


def _mix_kernel(xp_ref, xs_ref, a_ref, o_ref, c_ref, wo_ref, bo_ref, g1_ref, b1_ref, wr_ref, br_ref,
                h_ref, xs3_ref, aux_ref, cnt_ref, *, n_prompt):
    mixed = jnp.concatenate([a_ref[...]] + [o_ref[pair] for pair in range(N_PAIRS)] + [c_ref[...]], axis=1)
    mix = jnp.dot(mixed.astype(BF16), wo_ref[...], preferred_element_type=F32) + bo_ref[...]
    h = _ln_full(DN_ALPHA * _tile_rows(xp_ref, xs_ref, n_prompt) + mix, g1_ref[...], b1_ref[...])
    h_ref[...] = h

    hb = h.astype(BF16)
    h_lo = (h - hb.astype(F32)).astype(BF16)
    both = jnp.dot(hb, wr_ref[...], preferred_element_type=F32)
    logits = (both[:, :LANES] + both[:, LANES:]
              + jnp.dot(h_lo, wr_ref[:, :LANES], preferred_element_type=F32) + br_ref[...])
    lane = lax.broadcasted_iota(jnp.int32, (TT, LANES), 1)
    lane_f = lane.astype(F32)
    vals, idxs, sels = [], [], []
    cur = logits
    for _ in range(TOP_K):
        m = jnp.max(cur, axis=-1, keepdims=True)
        idx = jnp.min(jnp.where(cur == m, lane_f, float(LANES)), axis=-1, keepdims=True)
        sel = lane_f == idx
        vals.append(m)
        idxs.append(idx)
        sels.append(sel)
        cur = jnp.where(sel, -jnp.inf, cur)
    exps = [jnp.exp(v - vals[0]) for v in vals]
    den = exps[0] + exps[1] + exps[2] + exps[3]

    onehot = jnp.zeros((TT, LANES), F32)
    for sel in sels:
        onehot = onehot + sel.astype(F32)
    r = lax.broadcasted_iota(jnp.int32, (TT, TT), 0)
    c = lax.broadcasted_iota(jnp.int32, (TT, TT), 1)
    below = (c < r).astype(BF16)
    earlier = jnp.dot(below, onehot.astype(BF16), preferred_element_type=F32)
    cnt = jnp.broadcast_to(jnp.sum(onehot, axis=0, keepdims=True), (SUBLANES, LANES))
    er = lax.broadcasted_iota(jnp.int32, (LANES, LANES), 0)
    ec = lax.broadcasted_iota(jnp.int32, (LANES, LANES), 1)
    off = jnp.dot(cnt.astype(BF16), (er < ec).astype(BF16), preferred_element_type=F32)[0:1]
    place = earlier + off
    aux = jnp.zeros((TT, LANES), F32)
    rows = []
    for k in range(TOP_K):
        row = jnp.sum(jnp.where(sels[k], place, 0.0), axis=-1, keepdims=True)
        rows.append(row)
        aux = aux + jnp.where(lane == k, row, 0.0) + jnp.where(lane == TOP_K + k, exps[k] / den, 0.0)
    aux_ref[...] = aux
    cnt_ref[0] = cnt.astype(jnp.int32)

    aux_t = jnp.transpose(aux)
    dest = lax.broadcasted_iota(jnp.int32, (TOP_K * TT, TT), 0).astype(F32)
    disp = jnp.zeros((TOP_K * TT, TT), F32)
    for k in range(TOP_K):
        disp = disp + (dest == aux_t[k:k + 1, :]).astype(F32)
    xs = jnp.dot(disp.astype(BF16), hb, preferred_element_type=F32)
    for s in range(ROW_TILES):
        xs3_ref[pl.ds(s, TOP_K * TT, stride=SUBLANES), :] = xs[:, s * LANES:(s + 1) * LANES]


def _mix_call(x, a_out, o_b, c_out, wo, bo, g1, b1, wr, br, tp):
    n = a_out.shape[0]
    row = lambda w: pl.BlockSpec((TT, w), lambda i: (i, 0))
    xs3 = pl.BlockSpec((TOP_K * TT * SUBLANES, LANES), lambda i: (i, 0))
    cnt = pl.BlockSpec((1, SUBLANES, LANES), lambda i: (i, 0, 0))
    return pl.pallas_call(
        functools.partial(_mix_kernel, n_prompt=tp // TT), grid=(n // TT,),
        in_specs=_tile_row_specs(tp // TT)
        + [row(SEG), pl.BlockSpec((N_PAIRS, TT, LANES), lambda i: (0, i, 0)), row(SEG),
           _full(wo), _full(bo), _full(g1), _full(b1), _full(wr), _full(br)],
        out_specs=[row(D_MODEL), xs3, row(LANES), cnt],
        out_shape=[jax.ShapeDtypeStruct((n, D_MODEL), F32),
                   jax.ShapeDtypeStruct((n * TOP_K * SUBLANES, LANES), F32),
                   jax.ShapeDtypeStruct((n, LANES), F32),
                   jax.ShapeDtypeStruct((n // TT, SUBLANES, LANES), jnp.int32)],
        compiler_params=_params(), name="mix_ln_router")(
            *x, a_out, o_b, c_out, wo, bo, g1, b1, wr, br)


def _rows(start_row, n_rows):
    return pl.ds(pl.multiple_of(start_row * SUBLANES, SUBLANES), n_rows * SUBLANES)


def _expert_gather(tabs, blk, slot, xs3_hbm, buf, sem):
    blk_e, blk_s0, j_lo, j_hi, cnt, off, cum, tot = tabs[:8]
    e = blk_e[blk]
    s0 = blk_s0[blk]
    base = slot * TM

    @pl.when(tot[e] - s0 < TM)
    def _():
        buf[_rows(base, TM), :] = jnp.zeros((TM * SUBLANES, LANES), F32)

    def body(j, carry):
        run = cum[j * N_EXPERTS + e]
        lo = jnp.maximum(run, s0)
        hi = jnp.minimum(run + cnt[j * N_EXPERTS + e], s0 + TM)

        @pl.when(hi > lo)
        def _():
            src = j * (TOP_K * TT) + off[j * N_EXPERTS + e] + (lo - run)
            pltpu.make_async_copy(xs3_hbm.at[_rows(src, hi - lo)],
                                  buf.at[_rows(base + lo - s0, hi - lo)], sem.at[slot]).start()
        return carry

    lax.fori_loop(j_lo[blk], j_hi[blk], body, 0)


def _expert_gather_wait(tabs, blk, slot, xs3_hbm, buf, sem):
    blk_e, blk_s0, tot = tabs[0], tabs[1], tabs[7]
    valid = jnp.minimum(tot[blk_e[blk]] - blk_s0[blk], TM)
    pltpu.make_async_copy(xs3_hbm.at[_rows(0, valid)], buf.at[_rows(slot * TM, valid)],
                          sem.at[slot]).wait()


def _expert_weight_copies(w1_hbm, w2_hbm, expert, w1f, w2f, wslot, wsem):
    return (pltpu.make_async_copy(w1_hbm.at[expert], w1f.at[wslot], wsem.at[0, wslot]),
            pltpu.make_async_copy(w2_hbm.at[expert], w2f.at[wslot], wsem.at[1, wslot]))


def _expert_kernel(*refs, first_expert):
    tabs = refs[:10]
    blk_e, group, next_e = tabs[0], tabs[8], tabs[9]
    (n_used_ref, xs3_hbm, w1_hbm, b1_ref, w2_hbm, b2_ref, y3_ref,
     buf, xb, w1f, w2f, w1b, w2b, sem, wsem) = refs[10:]
    b = pl.program_id(0)
    n_used = n_used_ref[0]
    slot = lax.rem(b, 2)
    used = b < n_used
    wslot = lax.rem(group[b], 2)

    @pl.when(b == 0)
    def _():
        for copy in _expert_weight_copies(w1_hbm, w2_hbm, first_expert + blk_e[0], w1f, w2f, 0, wsem):
            copy.start()

    @pl.when(b == 0)
    def _():
        _expert_gather(tabs, 0, 0, xs3_hbm, buf, sem)

    @pl.when(used)
    def _():
        _expert_gather_wait(tabs, b, slot, xs3_hbm, buf, sem)

    @pl.when(b + 1 < n_used)
    def _():
        _expert_gather(tabs, b + 1, 1 - slot, xs3_hbm, buf, sem)

    new_expert = jnp.logical_or(b == 0, blk_e[b] != blk_e[jnp.maximum(b - 1, 0)])

    @pl.when(jnp.logical_and(used, new_expert))
    def _():
        for copy in _expert_weight_copies(w1_hbm, w2_hbm, first_expert + blk_e[b], w1f, w2f, wslot, wsem):
            copy.wait()

        @pl.when(next_e[b] >= 0)
        def _():
            nxt = first_expert + next_e[b]
            for copy in _expert_weight_copies(w1_hbm, w2_hbm, nxt, w1f, w2f, 1 - wslot, wsem):
                copy.start()

        w1b[...] = w1f[wslot].astype(BF16)
        w2b[...] = w2f[wslot].astype(BF16)

    @pl.when(used)
    def _():
        base = slot * (TM * SUBLANES)
        for s in range(ROW_TILES):
            xb[:, s * LANES:(s + 1) * LANES] = buf[pl.ds(base + s, TM, stride=SUBLANES), :].astype(BF16)
        x = xb[...]
        y = jnp.broadcast_to(b2_ref[0], (TM, D_MODEL))
        for c in range(0, D_FF, FF_CHUNK):
            cols = slice(c, c + FF_CHUNK)
            ucols = slice(D_FF + c, D_FF + c + FF_CHUNK)
            g = jnp.dot(x, w1b[:, cols], preferred_element_type=F32) + b1_ref[0, :, cols]
            u = jnp.dot(x, w1b[:, ucols], preferred_element_type=F32) + b1_ref[0, :, ucols]
            gate = jnp.minimum(g, SWIGLU_LIMIT)
            up = jnp.clip(u, -SWIGLU_LIMIT, SWIGLU_LIMIT)
            hh = (up + 1.0) * (gate * jax.nn.sigmoid(SWIGLU_ALPHA * gate))
            y = y + jnp.dot(hh.astype(BF16), w2b[cols, :], preferred_element_type=F32)
        for s in range(ROW_TILES):
            y3_ref[pl.ds(s, TM, stride=SUBLANES), :] = y[:, s * LANES:(s + 1) * LANES]

    @pl.when(jnp.logical_not(used))
    def _():
        y3_ref[...] = jnp.zeros_like(y3_ref)


def _expert_call(tabs, n_used, xs3, w1, b1, w2, b2, layer):
    n_blocks = tabs[0].shape[0]
    first_expert = layer * N_EXPERTS
    by_expert = lambda shape: pl.BlockSpec(shape, lambda b, e, *_: (first_expert + e[b], 0, 0))
    anywhere = pl.BlockSpec(memory_space=pl.ANY)
    grid_spec = pltpu.PrefetchScalarGridSpec(
        num_scalar_prefetch=len(tabs) + 1, grid=(n_blocks,),
        in_specs=[anywhere, anywhere, by_expert((1, 1, 2 * D_FF)), anywhere, by_expert((1, 1, D_MODEL))],
        out_specs=pl.BlockSpec((TM * SUBLANES, LANES), lambda b, *_: (b, 0)),
        scratch_shapes=[pltpu.VMEM((2 * TM * SUBLANES, LANES), F32),
                        pltpu.VMEM((TM, D_MODEL), BF16),
                        pltpu.VMEM((2, D_MODEL, 2 * D_FF), F32),
                        pltpu.VMEM((2, D_FF, D_MODEL), F32),
                        pltpu.VMEM((D_MODEL, 2 * D_FF), BF16),
                        pltpu.VMEM((D_FF, D_MODEL), BF16),
                        pltpu.SemaphoreType.DMA((2,)),
                        pltpu.SemaphoreType.DMA((2, 2))])
    return pl.pallas_call(
        functools.partial(_expert_kernel, first_expert=first_expert), grid_spec=grid_spec,
        out_shape=jax.ShapeDtypeStruct((n_blocks * TM * SUBLANES, LANES), F32),
        compiler_params=_params(vmem_limit=EXPERT_VMEM_LIMIT), name="expert_ffn")(
            *tabs, n_used, xs3, w1, b1, w2, b2)


PAIRS = TOP_K * TT


def _combine_gather(cnt, off, src, tile, slot, y3_hbm, buf, sem):
    def body(e, carry):
        n = cnt[tile * N_EXPERTS + e]

        @pl.when(n > 0)
        def _():
            pltpu.make_async_copy(y3_hbm.at[_rows(src[tile * N_EXPERTS + e], n)],
                                  buf.at[_rows(slot * PAIRS + off[tile * N_EXPERTS + e], n)],
                                  sem.at[slot]).start()
        return carry

    lax.fori_loop(0, N_EXPERTS, body, 0)


def _combine_kernel(cnt, off, src, y3_hbm, h_ref, aux_ref, g2_ref, b2_ref, op_ref, os_ref, buf, ys, sem,
                    *, n_prompt):
    i = pl.program_id(0)
    n_steps = pl.num_programs(0)
    slot = lax.rem(i, 2)

    @pl.when(i == 0)
    def _():
        _combine_gather(cnt, off, src, 0, 0, y3_hbm, buf, sem)

    pltpu.make_async_copy(y3_hbm.at[_rows(0, PAIRS)], buf.at[_rows(slot * PAIRS, PAIRS)],
                          sem.at[slot]).wait()

    @pl.when(i + 1 < n_steps)
    def _():
        _combine_gather(cnt, off, src, i + 1, 1 - slot, y3_hbm, buf, sem)

    base = slot * (PAIRS * SUBLANES)
    for s in range(ROW_TILES):
        ys[:, s * LANES:(s + 1) * LANES] = buf[pl.ds(base + s, PAIRS, stride=SUBLANES), :].astype(BF16)
    aux = aux_ref[...]
    dest = lax.broadcasted_iota(jnp.int32, (TT, PAIRS), 1).astype(F32)
    weights = jnp.zeros((TT, PAIRS), F32)
    for k in range(TOP_K):
        weights = weights + jnp.where(dest == aux[:, k:k + 1], aux[:, TOP_K + k:TOP_K + k + 1], 0.0)
    moe = jnp.dot(weights.astype(BF16), ys[...], preferred_element_type=F32)
    y = _ln_full(DN_ALPHA * h_ref[...] + moe, g2_ref[...], b2_ref[...])

    @pl.when(i < n_prompt)
    def _():
        op_ref[...] = y

    @pl.when(i >= n_prompt)
    def _():
        os_ref[...] = y


def _combine_call(cnt, off, src, y3, h, aux, g2, b2, tp):
    n = h.shape[0]
    row = lambda w: pl.BlockSpec((TT, w), lambda i, *_: (i, 0))
    n_prompt = tp // TT
    out_specs = _tile_row_specs(n_prompt)
    out_shape = [jax.ShapeDtypeStruct((tp, D_MODEL), F32), jax.ShapeDtypeStruct((n - tp, D_MODEL), F32)]
    grid_spec = pltpu.PrefetchScalarGridSpec(
        num_scalar_prefetch=3, grid=(n // TT,),
        in_specs=[pl.BlockSpec(memory_space=pl.ANY), row(D_MODEL), row(LANES),
                  pl.BlockSpec(g2.shape, lambda i, *_: (0, 0)), pl.BlockSpec(b2.shape, lambda i, *_: (0, 0))],
        out_specs=out_specs,
        scratch_shapes=[pltpu.VMEM((2 * PAIRS * SUBLANES, LANES), F32),
                        pltpu.VMEM((PAIRS, D_MODEL), BF16),
                        pltpu.SemaphoreType.DMA((2,))])
    return pl.pallas_call(
        functools.partial(_combine_kernel, n_prompt=n_prompt), grid_spec=grid_spec, out_shape=out_shape,
        compiler_params=_params(), name="combine_ln")(cnt, off, src, y3, h, aux, g2, b2)


def _routing_tables(cnt_tiles, n):
    cnt = cnt_tiles[:, 0, :N_EXPERTS]
    off = jnp.cumsum(cnt, axis=1) - cnt
    cum = jnp.cumsum(cnt, axis=0) - cnt
    tot = jnp.sum(cnt, axis=0)
    padded = (tot + TM - 1) // TM * TM
    pends = jnp.cumsum(padded)
    pstart = pends - padded
    n_blocks = n * TOP_K // TM + N_EXPERTS
    blk_start = jnp.arange(n_blocks, dtype=jnp.int32) * TM
    blk_e = jnp.minimum(jnp.sum((pends[None, :] <= blk_start[:, None]).astype(jnp.int32), axis=1),
                        N_EXPERTS - 1)
    blk_s0 = blk_start - pstart[blk_e]
    run_lo = cum[:, blk_e].T
    run_hi = run_lo + cnt[:, blk_e].T
    j_lo = jnp.sum((run_hi <= blk_s0[:, None]).astype(jnp.int32), axis=1)
    j_hi = jnp.sum((run_lo < blk_s0[:, None] + TM).astype(jnp.int32), axis=1)
    n_used = (pends[-1:] // TM).astype(jnp.int32)
    ids = jnp.arange(N_EXPERTS, dtype=jnp.int32)
    has = tot > 0
    group_of = jnp.cumsum(has.astype(jnp.int32)) - 1
    later = jnp.where(jnp.logical_and(has[None, :], ids[None, :] > ids[:, None]), ids[None, :], N_EXPERTS)
    next_of = jnp.min(later, axis=1)
    next_of = jnp.where(next_of == N_EXPERTS, -1, next_of)
    flat = lambda a: a.reshape(-1).astype(jnp.int32)
    expert_tabs = (flat(blk_e), flat(blk_s0), flat(j_lo), flat(j_hi), flat(cnt), flat(off), flat(cum), flat(tot),
                   flat(group_of[blk_e]), flat(next_of[blk_e]))
    combine_tabs = (flat(cnt), flat(off), flat(pstart[None, :] + cum))
    return expert_tabs, n_used, combine_tabs


def _pad_cols(a, width):
    return jnp.pad(a, [(0, 0)] * (a.ndim - 1) + [(0, width - a.shape[-1])])


def _layer_params(l, w_in, b_in, w_dw, b_dw, ln_a_g, ln_a_b, ln_c_g, ln_c_b, w_sp, b_sp,
                  w_out, b_out, ln1_g, ln1_b, w_router, b_router, ln2_g, ln2_b):
    bounds = [0, C_B, 2 * C_B, 3 * C_B, 3 * C_B + C_A, 3 * C_B + 2 * C_A, 3 * C_B + 2 * C_A + C_C,
              3 * C_B + 2 * C_A + 2 * C_C]
    w_segs = [_pad_cols(w_in[l][:, bounds[i]:bounds[i + 1]], SEG) for i in range(N_SEG)]
    b_segs = [_pad_cols(b_in[l][bounds[i]:bounds[i + 1]], SEG) for i in range(N_SEG)]
    row = lambda a: _pad_cols(a, SEG)[None, :]
    wo = w_out[l]
    pad_rows = lambda a: jnp.pad(a, ((0, SEG - a.shape[0]), (0, 0)))
    bsp = b_sp[l]
    bfull = _pad_cols(jnp.repeat(bsp.T, HEAD_DIM, axis=1), SEG)
    wr = _pad_cols(w_router[l], LANES)
    wr_hi = wr.astype(BF16)
    return dict(
        w_in=jnp.concatenate(w_segs, axis=1).astype(BF16),
        b_in=jnp.concatenate(b_segs)[None, :],
        w_dw=jnp.pad(w_dw[l], ((0, HALO - CONV_W), (0, SEG - C_A))),
        b_dw=row(b_dw[l]), ln_a_g=row(ln_a_g[l]), ln_a_b=row(ln_a_b[l]),
        ln_c_g=row(ln_c_g[l]), ln_c_b=row(ln_c_b[l]),
        w_sp=w_sp[l], b_full=bfull,
        w_out=jnp.concatenate([pad_rows(wo[:C_A]), wo[C_A:C_A + C_B], pad_rows(wo[C_A + C_B:])]).astype(BF16),
        b_out=b_out[l][None, :],
        ln1_g=ln1_g[l][None, :], ln1_b=ln1_b[l][None, :],
        w_router=jnp.concatenate([wr_hi, (wr - wr_hi.astype(F32)).astype(BF16)], axis=1),
        b_router=jnp.concatenate([b_router[l], jnp.full((LANES - N_EXPERTS,), NEG, F32)])[None, :],
        ln2_g=ln2_g[l][None, :], ln2_b=ln2_b[l][None, :])


def _layer(x, p, hist_p, caches, experts, biases, n, tp, layer):
    q, k, v, a, c_out, vn_s = _proj_call(x, p["w_in"], p["b_in"], p["ln_c_g"], p["ln_c_b"],
                                         p["w_sp"], p["b_full"], n, tp)

    conv_args = (p["w_dw"], p["b_dw"], p["ln_a_g"], p["ln_a_b"])
    a_out = _conv_prompt_call(a, *conv_args, tp)
    a_out, new_a_s = _conv_sample_call(hist_p, a, *conv_args, a_out, tp)

    o_b = _attn_prompt_call(q, k, v, biases["prompt"], tp)
    o_b = _attn_sample_call(q, k, v, *caches, layer, *biases["sample"], o_b, tp)

    h, xs3, aux, cnt_tiles = _mix_call(x, a_out, o_b, c_out, p["w_out"], p["b_out"],
                                       p["ln1_g"], p["ln1_b"], p["w_router"], p["b_router"], tp)
    expert_tabs, n_used, combine_tabs = _routing_tables(cnt_tiles, n)
    y3 = _expert_call(expert_tabs, n_used, xs3, *experts, layer)
    y = _combine_call(*combine_tabs, y3, h, aux, p["ln2_g"], p["ln2_b"], tp)
    return y, a, k, v, vn_s, new_a_s


def kernel(x_prompt, x_sample, state_a_conv, cache_b_k, cache_b_v, w_in, b_in, w_dw, b_dw, ln_a_g, ln_a_b, ln_c_g, ln_c_b, w_sp, b_sp, w_out, b_out, ln1_g, ln1_b, w_router, b_router, w1, b1, w2, b2, ln2_g, ln2_b):
    bp, tp, _ = x_prompt.shape
    bs, ts, _ = x_sample.shape
    n_cache = cache_b_k.shape[2]
    assert bp == 1 and ts == DEC_SEQ and bs * ts == TT and tp % WIN_MAX == 0
    assert n_cache == WIN_MAX
    n = tp + bs * ts
    keep = min(WIN_MAX, tp)
    hist = CONV_W - 1

    weights = (w_in, b_in, w_dw, b_dw, ln_a_g, ln_a_b, ln_c_g, ln_c_b, w_sp, b_sp, w_out, b_out,
               ln1_g, ln1_b, w_router, b_router, ln2_g, ln2_b)
    biases = {"prompt": [_prompt_bias(d) for d in DILATIONS], "sample": _sample_bias(n_cache)}
    experts = (w1.reshape(DEPTH * N_EXPERTS, D_MODEL, 2 * D_FF), b1.reshape(DEPTH * N_EXPERTS, 1, 2 * D_FF),
               w2.reshape(DEPTH * N_EXPERTS, D_FF, D_MODEL), b2.reshape(DEPTH * N_EXPERTS, 1, D_MODEL))
    to_feature_major = lambda c: jnp.transpose(c, (0, 1, 3, 4, 2)).reshape(DEPTH, bs, C_B, n_cache)
    caches = (to_feature_major(cache_b_k), to_feature_major(cache_b_v))
    heads = lambda a: jnp.transpose(a, (1, 0, 2)).reshape(a.shape[1], N_HEADS_B, HEAD_DIM)

    x = (x_prompt[0], x_sample.reshape(bs * ts, D_MODEL))
    outs = {name: [] for name in ("a_p", "a_s", "k_p", "v_p", "k_s", "v_s", "c_s")}
    for l in range(DEPTH):
        p = _layer_params(l, *weights)
        hist_p = jnp.pad(state_a_conv[l], ((0, 0), (HALO - hist, 0), (0, SEG - C_A)))
        x, a, k, v, vn_s, new_a_s = _layer(x, p, hist_p, caches, experts, biases, n, tp, l)
        outs["a_p"].append(a[tp - hist:tp, :C_A][None])
        outs["a_s"].append(new_a_s[:, HALO - hist:, :C_A])
        outs["k_p"].append(heads(k[:, tp - keep:tp])[None])
        outs["v_p"].append(heads(v[:, tp - keep:tp])[None])
        outs["k_s"].append(heads(k[:, tp:]).reshape(bs, ts, N_HEADS_B, HEAD_DIM))
        outs["v_s"].append(heads(v[:, tp:]).reshape(bs, ts, N_HEADS_B, HEAD_DIM))
        outs["c_s"].append(vn_s[:, :C_C].reshape(bs, ts, C_C))
    stack = lambda name: jnp.stack(outs[name])
    y_prompt, y_sample = x
    return (y_prompt[None], y_sample.reshape(bs, ts, D_MODEL), stack("a_p"), stack("a_s"),
            stack("k_p"), stack("v_p"), stack("k_s"), stack("v_s"), stack("c_s"))
```

```python
import functools
import math

import jax
import jax.numpy as jnp
import numpy as np
from jax import lax
from jax.experimental import pallas as pl
from jax.experimental.pallas import tpu as pltpu

F32 = jnp.float32
BF16 = jnp.bfloat16

D_MODEL = 1024
HEAD_DIM = 64
N_HEADS_B = 6
C_B = N_HEADS_B * HEAD_DIM
C_A = 320
C_C = 320
N_GROUPS_C = 5
CONV_W = 31
DILATIONS = (1, 4, 16)
BAND = 128
WIN_MAX = 2048
CHUNK = 128
N_EXPERTS = 32
TOP_K = 4
D_FF = 1024
SWIGLU_LIMIT = 7.0
SWIGLU_ALPHA = 1.702
LN_EPS = 1e-5
DEPTH = 2
DN_ALPHA = (2.0 * DEPTH) ** 0.25
DEC_SEQ = 8

LANES = 128
SUBLANES = 8
SEG = 384
N_SEG = 7
N_PAIRS = C_B // LANES
ROW_TILES = D_MODEL // LANES
assert ROW_TILES == SUBLANES

TT = 256
TM = 512
FF_CHUNK = 512
NEG = -1e30
LOG2E = math.log2(math.e)
MIB = 1024 * 1024
VMEM_LIMIT = 48 * MIB
EXPERT_VMEM_LIMIT = 56 * MIB


def _params(n_axes=1, vmem_limit=VMEM_LIMIT):
    return pltpu.CompilerParams(dimension_semantics=("arbitrary",) * n_axes,
                                vmem_limit_bytes=vmem_limit)


def _full(a):
    nd = a.ndim
    return pl.BlockSpec(a.shape, lambda *_: (0,) * nd)


def _ln_valid(x, g, b, n_valid):
    col = lax.broadcasted_iota(jnp.int32, x.shape, 1)
    ok = col < n_valid
    mu = jnp.sum(jnp.where(ok, x, 0.0), axis=-1, keepdims=True) / n_valid
    xc = jnp.where(ok, x - mu, 0.0)
    var = jnp.sum(xc * xc, axis=-1, keepdims=True) / n_valid
    return xc * lax.rsqrt(var + LN_EPS) * g + b


def _ln_full(x, g, b):
    mu = jnp.mean(x, axis=-1, keepdims=True)
    xc = x - mu
    var = jnp.mean(xc * xc, axis=-1, keepdims=True)
    return xc * lax.rsqrt(var + LN_EPS) * g + b


def _spatial_gate(vn, cu, w_ref, bfull, rows, sub):
    r = lax.broadcasted_iota(jnp.int32, (rows, rows), 0)
    c = lax.broadcasted_iota(jnp.int32, (rows, rows), 1)
    causal = c <= r
    if sub is not None:
        causal = jnp.logical_and(causal, lax.shift_right_logical(r, sub) == lax.shift_right_logical(c, sub))
        pos = lax.broadcasted_iota(jnp.int32, (rows, CHUNK), 0) & (2 ** sub - 1)
        spread = (pos == lax.broadcasted_iota(jnp.int32, (rows, CHUNK), 1)).astype(BF16)
        bfull = jnp.concatenate([bfull[0:2 ** sub]] * (rows // 2 ** sub), axis=0)
    lo_half = lax.broadcasted_iota(jnp.int32, (rows, LANES), 1) < HEAD_DIM
    cols = []
    for pair in range(SEG // LANES):
        cs = slice(pair * LANES, (pair + 1) * LANES)
        acc = jnp.zeros((rows, LANES), F32)
        for hh in range(2):
            g = pair * 2 + hh
            if g >= N_GROUPS_C:
                continue
            keep = lo_half if hh == 0 else jnp.logical_not(lo_half)
            w = w_ref[g].astype(BF16)
            if sub is not None:
                w = jnp.dot(spread, w, preferred_element_type=F32).astype(BF16)
                w = lax.dot_general(w, spread, (((1,), (1,)), ((), ())), preferred_element_type=F32).astype(BF16)
            wm = jnp.where(causal, w, 0.0)
            vm = jnp.where(keep, vn[:, cs], 0.0).astype(BF16)
            acc = acc + jnp.dot(wm, vm, preferred_element_type=F32)
        cols.append(cu[:, cs] * (acc + bfull[:, cs]))
    return jnp.concatenate(cols, axis=1)


def _tile_rows(xp_ref, xs_ref, n_prompt):
    return jnp.where(pl.program_id(0) < n_prompt, xp_ref[...], xs_ref[...])


def _tile_row_specs(n_prompt):
    return [pl.BlockSpec((TT, D_MODEL), lambda i, *_: (jnp.minimum(i, n_prompt - 1), 0)),
            pl.BlockSpec((TT, D_MODEL), lambda i, *_: (jnp.maximum(i - n_prompt, 0), 0))]


def _proj_kernel(xp_ref, xs_ref, w_ref, b_ref, lcg_ref, lcb_ref, wsp_ref, bsp_ref,
                 q_ref, k_ref, v_ref, a_ref, c_ref, vns_ref, *, n_prompt):
    x = _tile_rows(xp_ref, xs_ref, n_prompt)
    proj = jnp.dot(x.astype(BF16), w_ref[...], preferred_element_type=F32) + b_ref[...]
    seg = lambda i: proj[:, i * SEG:(i + 1) * SEG]

    for i, ref in enumerate((q_ref, k_ref, v_ref)):
        qkv = seg(i)
        for pair in range(N_PAIRS):
            ref[pair] = qkv[:, pair * LANES:(pair + 1) * LANES]
    a_ref[...] = seg(3) * jax.nn.sigmoid(seg(4))
    cu = seg(5)
    vn = _ln_valid(seg(6), lcg_ref[...], lcb_ref[...], C_C)
    is_prompt = pl.program_id(0) < n_prompt

    @pl.when(is_prompt)
    def _():
        for ch in range(0, TT, CHUNK):
            rows = slice(ch, ch + CHUNK)
            c_ref[rows, :] = _spatial_gate(vn[rows], cu[rows], wsp_ref, bsp_ref[...], CHUNK, None)

    @pl.when(jnp.logical_not(is_prompt))
    def _():
        c_ref[...] = _spatial_gate(vn, cu, wsp_ref, bsp_ref[...], TT, DEC_SEQ.bit_length() - 1)
        vns_ref[...] = vn


def _proj_call(x, w_in_p, b_in_p, lcg, lcb, w_sp, b_full, n, tp):
    row = lambda w: pl.BlockSpec((TT, w), lambda i: (i, 0))
    heads = pl.BlockSpec((N_PAIRS, TT, LANES), lambda i: (0, i, 0))
    out = jax.ShapeDtypeStruct((n, SEG), F32)
    out3 = jax.ShapeDtypeStruct((N_PAIRS, n, LANES), F32)
    consts = (w_in_p, b_in_p, lcg, lcb, w_sp, b_full)
    return pl.pallas_call(
        functools.partial(_proj_kernel, n_prompt=tp // TT), grid=(n // TT,),
        in_specs=_tile_row_specs(tp // TT) + [_full(c) for c in consts],
        out_specs=[heads] * 3 + [row(SEG)] * 2 + [pl.BlockSpec((TT, SEG), lambda i: (0, 0))],
        out_shape=[out3] * 3 + [out] * 2 + [jax.ShapeDtypeStruct((n - tp, SEG), F32)],
        compiler_params=_params(), name="in_proj")(*x, *consts)


HALO = 32
CONV_CHUNK = 32


def _conv_tail(acc, bdw_ref, g_ref, b_ref):
    return jax.nn.silu(_ln_valid(acc + bdw_ref[...], g_ref[...], b_ref[...], C_A))


def _conv_prompt_kernel(halo_ref, a_ref, w_ref, bdw_ref, g_ref, b_ref, o_ref, buf, shifted, *, n_prompt):
    i = pl.program_id(0)

    @pl.when(i < n_prompt)
    def _():
        buf[0:HALO, :] = jnp.where(i > 0, halo_ref[...], 0.0)
        buf[HALO:, :] = a_ref[...]
        n_rows = HALO + TT - SUBLANES
        for sh in range(1, SUBLANES):
            shifted[sh - 1, 0:n_rows, :] = buf[sh:sh + n_rows, :]
        base = HALO - (CONV_W - 1)
        for r in range(0, TT, CONV_CHUNK):
            acc = jnp.zeros((CONV_CHUNK, SEG), F32)
            for j in range(CONV_W):
                tiles, sh = divmod(base + j, SUBLANES)
                src = buf if sh == 0 else shifted.at[sh - 1]
                lo = r + tiles * SUBLANES
                acc = acc + w_ref[j:j + 1, :] * src[lo:lo + CONV_CHUNK, :]
            o_ref[r:r + CONV_CHUNK, :] = _conv_tail(acc, bdw_ref, g_ref, b_ref)

    @pl.when(i >= n_prompt)
    def _():
        o_ref[...] = jnp.zeros_like(o_ref)


def _conv_prompt_call(a, w_dw_p, bdw, g, b, tp):
    n = a.shape[0]
    per = TT // HALO
    halo = pl.BlockSpec((HALO, SEG), lambda i: (jnp.maximum(i * per - 1, 0), 0))
    row = pl.BlockSpec((TT, SEG), lambda i: (i, 0))
    return pl.pallas_call(
        functools.partial(_conv_prompt_kernel, n_prompt=tp // TT), grid=(n // TT,),
        in_specs=[halo, row, _full(w_dw_p), _full(bdw), _full(g), _full(b)],
        out_specs=row, out_shape=jax.ShapeDtypeStruct((n, SEG), F32),
        scratch_shapes=[pltpu.VMEM((HALO + TT, SEG), F32),
                        pltpu.VMEM((SUBLANES - 1, HALO + TT, SEG), F32)],
        compiler_params=_params(), name="conv_prompt")(a, a, w_dw_p, bdw, g, b)


SB = 8


def _conv_sample_kernel(hist_ref, a_ref, w_ref, bdw_ref, g_ref, b_ref, big_ref,
                        o_ref, newa_ref, buf):
    del big_ref
    bs = hist_ref.shape[0]
    buf[:, 0:HALO, :] = hist_ref[...]
    buf[:, HALO:, :] = a_ref[...].reshape(bs, DEC_SEQ, SEG)
    base = HALO - (CONV_W - 1)
    for s in range(0, bs, SB):
        acc = jnp.zeros((SB, DEC_SEQ, SEG), F32)
        for j in range(CONV_W):
            acc = acc + w_ref[j:j + 1, :][None] * buf[s:s + SB, base + j: base + j + DEC_SEQ, :]
        y = _conv_tail(acc.reshape(SB * DEC_SEQ, SEG), bdw_ref, g_ref, b_ref)
        o_ref[s * DEC_SEQ:(s + SB) * DEC_SEQ, :] = y
    newa_ref[...] = buf[:, DEC_SEQ:, :]


def _conv_sample_call(hist_p, a, w_dw_p, bdw, g, b, a_out, tp):
    bs = hist_p.shape[0]
    rows = bs * DEC_SEQ
    blk = pl.BlockSpec((rows, SEG), lambda i: (tp // rows, 0))
    return pl.pallas_call(
        _conv_sample_kernel, grid=(1,),
        in_specs=[_full(hist_p), blk, _full(w_dw_p), _full(bdw), _full(g), _full(b),
                  pl.BlockSpec(memory_space=pl.ANY)],
        out_specs=[blk, pl.BlockSpec((bs, HALO, SEG), lambda i: (0, 0, 0))],
        out_shape=[jax.ShapeDtypeStruct(a_out.shape, F32),
                   jax.ShapeDtypeStruct((bs, HALO, SEG), F32)],
        scratch_shapes=[pltpu.VMEM((bs, HALO + DEC_SEQ, SEG), F32)],
        input_output_aliases={6: 0},
        compiler_params=_params(), name="conv_sample")(hist_p, a, w_dw_p, bdw, g, b, a_out)


QB = 128
assert QB == BAND


SPAN = BAND * max(DILATIONS)
UNITS = SPAN // QB
UNIT_UNROLL = 8


def _attn_prompt_kernel(q_ref, kh_ref, kc_ref, vh_ref, vc_ref, b1_ref, b4_ref, b16_ref, o_ref,
                        kbuf, vbuf, oacc, lacc, *, n_prompt):
    is_prompt = pl.program_id(1) < n_prompt
    refs = (q_ref, kh_ref, kc_ref, vh_ref, vc_ref, (b1_ref, b4_ref, b16_ref), o_ref, kbuf, vbuf, oacc, lacc)
    pl.when(is_prompt)(functools.partial(_attn_prompt_span, refs))

    @pl.when(jnp.logical_not(is_prompt))
    def _():
        o_ref[...] = jnp.zeros_like(o_ref)


def _attn_prompt_span(refs):
    q_ref, kh_ref, kc_ref, vh_ref, vc_ref, bias_refs, o_ref, kbuf, vbuf, oacc, lacc = refs
    pair = pl.program_id(0)
    first_span = pl.program_id(1) == 0
    kbuf[0:SPAN, :] = kh_ref[0]
    kbuf[SPAN:, :] = kc_ref[0]
    vbuf[0:SPAN, :] = vh_ref[0]
    vbuf[SPAN:, :] = vc_ref[0]
    lane = lax.broadcasted_iota(jnp.int32, (QB, LANES), 1)
    lo_half = lane < HEAD_DIM

    for dil, bias_ref in zip(DILATIONS, bias_refs):
        def unit(u, carry, dil=dil, bias_ref=bias_ref):
            sub = u // dil
            t0 = sub * (QB * dil) + (u - sub * dil)
            if dil == 1:
                rows_q = pl.ds(pl.multiple_of(t0, QB), QB)
                rows_k = pl.ds(pl.multiple_of(SPAN + t0 - QB, QB), 2 * QB)
            else:
                rows_q = pl.ds(t0, QB, stride=dil)
                rows_k = pl.ds(SPAN + t0 - QB * dil, 2 * QB, stride=dil)
            table = jnp.logical_and(first_span, sub == 0).astype(jnp.int32) * N_HEADS_B + pair * 2
            qp = q_ref[0, rows_q, :] * (HEAD_DIM ** -0.5 * LOG2E)
            kk = kbuf[rows_k, :].astype(BF16)
            vv = vbuf[rows_k, :].astype(BF16)
            outs, lses = [], []
            for hh in range(2):
                keep = lo_half if hh == 0 else jnp.logical_not(lo_half)
                qm = jnp.where(keep, qp, 0.0).astype(BF16)
                s = lax.dot_general(qm, kk, (((1,), (1,)), ((), ())), preferred_element_type=F32)
                s = s + bias_ref[table + hh]
                m = jnp.max(s, axis=-1, keepdims=True)
                p = jnp.exp2(s - m)
                l = jnp.sum(p, axis=-1, keepdims=True)
                pv = jnp.dot(p.astype(BF16), vv, preferred_element_type=F32)
                outs.append(pv / l)
                lses.append(m + jnp.log2(l))
            o_new = jnp.where(lo_half, outs[0], outs[1])
            l_new = jnp.where(lo_half, lses[0], lses[1])
            if dil != 1:
                l_old = lacc[rows_q, :]
                mx = jnp.maximum(l_old, l_new)
                w_old = jnp.exp2(l_old - mx)
                w_new = jnp.exp2(l_new - mx)
                tot = w_old + w_new
                o_new = (w_old * oacc[rows_q, :] + w_new * o_new) / tot
                l_new = mx + jnp.log2(tot)
            oacc[rows_q, :] = o_new
            lacc[rows_q, :] = l_new
            return carry

        lax.fori_loop(0, UNITS, unit, 0, unroll=UNIT_UNROLL)
    o_ref[0] = oacc[...]


def _attn_prompt_call(q, k, v, biases, tp):
    n = q.shape[1]
    cur = pl.BlockSpec((1, SPAN, LANES), lambda p, m: (p, m, 0))
    halo = pl.BlockSpec((1, SPAN, LANES), lambda p, m: (p, jnp.maximum(m - 1, 0), 0))
    scratch = [pltpu.VMEM((2 * SPAN, LANES), F32), pltpu.VMEM((2 * SPAN, LANES), F32),
               pltpu.VMEM((SPAN, LANES), F32), pltpu.VMEM((SPAN, LANES), F32)]
    return pl.pallas_call(
        functools.partial(_attn_prompt_kernel, n_prompt=tp // SPAN),
        grid=(N_PAIRS, pl.cdiv(n, SPAN)),
        in_specs=[cur, halo, cur, halo, cur] + [_full(b) for b in biases],
        out_specs=cur, out_shape=jax.ShapeDtypeStruct((N_PAIRS, n, LANES), F32),
        scratch_shapes=scratch, compiler_params=_params(2), name="attn_prompt")(q, k, k, v, v, *biases)


def _alibi_slopes():
    return [2.0 ** (-8.0 * (h + 1) / N_HEADS_B) for h in range(N_HEADS_B)]


def _prompt_bias(dil):
    i = np.arange(QB)[:, None]
    j = np.arange(2 * QB)[None, :]
    rel = i + QB - j
    ok = np.logical_and(rel >= 0, rel <= BAND)
    slopes = np.asarray(_alibi_slopes(), np.float32)[:, None, None]
    pen = (-slopes * (rel * dil).astype(np.float32)[None]) * np.float32(LOG2E)
    full = np.where(ok[None], pen, np.float32(NEG)).astype(np.float32)
    start = np.where((j >= QB)[None], full, np.float32(NEG)).astype(np.float32)
    return jnp.asarray(np.concatenate([full, start], axis=0))


def _branch_multiplicity(dist):
    cnt = np.zeros(dist.shape, np.int32)
    for dil in DILATIONS:
        cnt = cnt + np.logical_and(dist % dil == 0, dist <= BAND * dil).astype(np.int32)
    return cnt


def _sample_bias(n_cache):
    t = np.arange(DEC_SEQ)[:, None]
    dist_c = n_cache + t - np.arange(n_cache)[None, :]
    dist_n = t - np.arange(LANES)[None, :]
    ok_n = np.logical_and(dist_n >= 0, np.arange(LANES)[None, :] < DEC_SEQ)
    slopes = np.asarray(_alibi_slopes(), np.float32)[:, None, None]

    def bias(dist, ok):
        mult = _branch_multiplicity(np.maximum(dist, 0))
        ok = np.logical_and(ok, mult > 0)
        val = -slopes * dist.astype(np.float32)[None] + np.log(np.maximum(mult, 1).astype(np.float32))[None]
        return np.where(ok[None], val, np.float32(NEG)).astype(np.float32)

    bc = bias(dist_c, np.ones(dist_c.shape, bool)).reshape(N_HEADS_B * DEC_SEQ, n_cache)
    bn = bias(dist_n, ok_n).reshape(N_HEADS_B * DEC_SEQ, LANES)
    return jnp.asarray(bc), jnp.asarray(bn)


def _attn_sample_kernel(q_ref, kn_ref, vn_ref, kc_ref, vc_ref, bc_ref, bn_ref, big_ref, o_ref):
    del big_ref
    rows = N_HEADS_B * DEC_SEQ
    wide = lambda ref: jnp.concatenate([ref[pair] for pair in range(N_PAIRS)], axis=1)
    q = wide(q_ref) * (HEAD_DIM ** -0.5)
    qrep = jnp.concatenate([q] * N_HEADS_B, axis=0)
    rh = lax.shift_right_logical(lax.broadcasted_iota(jnp.int32, (rows, SEG), 0), 3)
    ch = lax.shift_right_logical(lax.broadcasted_iota(jnp.int32, (rows, SEG), 1), 6)
    own = rh == ch
    qm = jnp.where(own, qrep, 0.0).astype(BF16)
    pad = jnp.zeros((LANES - DEC_SEQ, SEG), F32)
    kn = jnp.concatenate([wide(kn_ref), pad], axis=0).astype(BF16)
    vn = jnp.concatenate([wide(vn_ref), pad], axis=0).astype(BF16)
    nt = (((1,), (1,)), ((), ()))
    s_c = jnp.dot(qm, kc_ref[0, 0].astype(BF16), preferred_element_type=F32) + bc_ref[...]
    s_n = lax.dot_general(qm, kn, nt, preferred_element_type=F32) + bn_ref[...]
    m = jnp.maximum(jnp.max(s_c, axis=-1, keepdims=True), jnp.max(s_n, axis=-1, keepdims=True))
    p_c = jnp.exp(s_c - m)
    p_n = jnp.exp(s_n - m)
    l = jnp.sum(p_c, axis=-1, keepdims=True) + jnp.sum(p_n, axis=-1, keepdims=True)
    r = (lax.dot_general(p_c.astype(BF16), vc_ref[0, 0].astype(BF16), nt, preferred_element_type=F32)
         + jnp.dot(p_n.astype(BF16), vn, preferred_element_type=F32)) / l
    r = jnp.where(own, r, 0.0)
    o = r[0:DEC_SEQ]
    for h in range(1, N_HEADS_B):
        o = o + r[h * DEC_SEQ:(h + 1) * DEC_SEQ]
    for pair in range(N_PAIRS):
        o_ref[pair] = o[:, pair * LANES:(pair + 1) * LANES]


def _attn_sample_call(q, k, v, cache_kt, cache_vt, layer, bc, bn, o_big, tp):
    bs, n_cache = cache_kt.shape[1], cache_kt.shape[3]
    new = pl.BlockSpec((N_PAIRS, DEC_SEQ, LANES), lambda b: (0, tp // DEC_SEQ + b, 0))
    cache = pl.BlockSpec((1, 1, C_B, n_cache), lambda b: (layer, b, 0, 0))
    return pl.pallas_call(
        _attn_sample_kernel, grid=(bs,),
        in_specs=[new, new, new, cache, cache, _full(bc), _full(bn),
                  pl.BlockSpec(memory_space=pl.ANY)],
        out_specs=new, out_shape=jax.ShapeDtypeStruct(o_big.shape, F32),
        input_output_aliases={7: 0},
        compiler_params=_params(), name="attn_sample")(q, k, v, cache_kt, cache_vt, bc, bn, o_big)


def _mix_kernel(xp_ref, xs_ref, a_ref, o_ref, c_ref, wo_ref, bo_ref, g1_ref, b1_ref, wr_ref, br_ref,
                h_ref, xs3_ref, aux_ref, cnt_ref, *, n_prompt):
    mixed = jnp.concatenate([a_ref[...]] + [o_ref[pair] for pair in range(N_PAIRS)] + [c_ref[...]], axis=1)
    mix = jnp.dot(mixed.astype(BF16), wo_ref[...], preferred_element_type=F32) + bo_ref[...]
    h = _ln_full(DN_ALPHA * _tile_rows(xp_ref, xs_ref, n_prompt) + mix, g1_ref[...], b1_ref[...])
    h_ref[...] = h

    hb = h.astype(BF16)
    h_lo = (h - hb.astype(F32)).astype(BF16)
    both = jnp.dot(hb, wr_ref[...], preferred_element_type=F32)
    logits = (both[:, :LANES] + both[:, LANES:]
              + jnp.dot(h_lo, wr_ref[:, :LANES], preferred_element_type=F32) + br_ref[...])
    lane = lax.broadcasted_iota(jnp.int32, (TT, LANES), 1)
    lane_f = lane.astype(F32)
    vals, idxs, sels = [], [], []
    cur = logits
    for _ in range(TOP_K):
        m = jnp.max(cur, axis=-1, keepdims=True)
        idx = jnp.min(jnp.where(cur == m, lane_f, float(LANES)), axis=-1, keepdims=True)
        sel = lane_f == idx
        vals.append(m)
        idxs.append(idx)
        sels.append(sel)
        cur = jnp.where(sel, -jnp.inf, cur)
    exps = [jnp.exp(v - vals[0]) for v in vals]
    den = exps[0] + exps[1] + exps[2] + exps[3]

    onehot = jnp.zeros((TT, LANES), F32)
    for sel in sels:
        onehot = onehot + sel.astype(F32)
    r = lax.broadcasted_iota(jnp.int32, (TT, TT), 0)
    c = lax.broadcasted_iota(jnp.int32, (TT, TT), 1)
    below = (c < r).astype(BF16)
    earlier = jnp.dot(below, onehot.astype(BF16), preferred_element_type=F32)
    cnt = jnp.broadcast_to(jnp.sum(onehot, axis=0, keepdims=True), (SUBLANES, LANES))
    er = lax.broadcasted_iota(jnp.int32, (LANES, LANES), 0)
    ec = lax.broadcasted_iota(jnp.int32, (LANES, LANES), 1)
    off = jnp.dot(cnt.astype(BF16), (er < ec).astype(BF16), preferred_element_type=F32)[0:1]
    place = earlier + off
    aux = jnp.zeros((TT, LANES), F32)
    rows = []
    for k in range(TOP_K):
        row = jnp.sum(jnp.where(sels[k], place, 0.0), axis=-1, keepdims=True)
        rows.append(row)
        aux = aux + jnp.where(lane == k, row, 0.0) + jnp.where(lane == TOP_K + k, exps[k] / den, 0.0)
    aux_ref[...] = aux
    cnt_ref[0] = cnt.astype(jnp.int32)

    aux_t = jnp.transpose(aux)
    dest = lax.broadcasted_iota(jnp.int32, (TOP_K * TT, TT), 0).astype(F32)
    disp = jnp.zeros((TOP_K * TT, TT), F32)
    for k in range(TOP_K):
        disp = disp + (dest == aux_t[k:k + 1, :]).astype(F32)
    xs = jnp.dot(disp.astype(BF16), hb, preferred_element_type=F32)
    for s in range(ROW_TILES):
        xs3_ref[pl.ds(s, TOP_K * TT, stride=SUBLANES), :] = xs[:, s * LANES:(s + 1) * LANES]


def _mix_call(x, a_out, o_b, c_out, wo, bo, g1, b1, wr, br, tp):
    n = a_out.shape[0]
    row = lambda w: pl.BlockSpec((TT, w), lambda i: (i, 0))
    xs3 = pl.BlockSpec((TOP_K * TT * SUBLANES, LANES), lambda i: (i, 0))
    cnt = pl.BlockSpec((1, SUBLANES, LANES), lambda i: (i, 0, 0))
    return pl.pallas_call(
        functools.partial(_mix_kernel, n_prompt=tp // TT), grid=(n // TT,),
        in_specs=_tile_row_specs(tp // TT)
        + [row(SEG), pl.BlockSpec((N_PAIRS, TT, LANES), lambda i: (0, i, 0)), row(SEG),
           _full(wo), _full(bo), _full(g1), _full(b1), _full(wr), _full(br)],
        out_specs=[row(D_MODEL), xs3, row(LANES), cnt],
        out_shape=[jax.ShapeDtypeStruct((n, D_MODEL), F32),
                   jax.ShapeDtypeStruct((n * TOP_K * SUBLANES, LANES), F32),
                   jax.ShapeDtypeStruct((n, LANES), F32),
                   jax.ShapeDtypeStruct((n // TT, SUBLANES, LANES), jnp.int32)],
        compiler_params=_params(), name="mix_ln_router")(
            *x, a_out, o_b, c_out, wo, bo, g1, b1, wr, br)


def _rows(start_row, n_rows):
    return pl.ds(pl.multiple_of(start_row * SUBLANES, SUBLANES), n_rows * SUBLANES)


def _expert_gather(tabs, blk, slot, xs3_hbm, buf, sem):
    blk_e, blk_s0, j_lo, j_hi, cnt, off, cum, tot = tabs[:8]
    e = blk_e[blk]
    s0 = blk_s0[blk]
    base = slot * TM

    @pl.when(tot[e] - s0 < TM)
    def _():
        buf[_rows(base, TM), :] = jnp.zeros((TM * SUBLANES, LANES), F32)

    def body(j, carry):
        run = cum[j * N_EXPERTS + e]
        lo = jnp.maximum(run, s0)
        hi = jnp.minimum(run + cnt[j * N_EXPERTS + e], s0 + TM)

        @pl.when(hi > lo)
        def _():
            src = j * (TOP_K * TT) + off[j * N_EXPERTS + e] + (lo - run)
            pltpu.make_async_copy(xs3_hbm.at[_rows(src, hi - lo)],
                                  buf.at[_rows(base + lo - s0, hi - lo)], sem.at[slot]).start()
        return carry

    lax.fori_loop(j_lo[blk], j_hi[blk], body, 0)


def _expert_gather_wait(tabs, blk, slot, xs3_hbm, buf, sem):
    blk_e, blk_s0, tot = tabs[0], tabs[1], tabs[7]
    valid = jnp.minimum(tot[blk_e[blk]] - blk_s0[blk], TM)
    pltpu.make_async_copy(xs3_hbm.at[_rows(0, valid)], buf.at[_rows(slot * TM, valid)],
                          sem.at[slot]).wait()


def _expert_weight_copies(w1_hbm, w2_hbm, expert, w1f, w2f, wslot, wsem):
    return (pltpu.make_async_copy(w1_hbm.at[expert], w1f.at[wslot], wsem.at[0, wslot]),
            pltpu.make_async_copy(w2_hbm.at[expert], w2f.at[wslot], wsem.at[1, wslot]))


def _expert_kernel(*refs, first_expert):
    tabs = refs[:10]
    blk_e, group, next_e = tabs[0], tabs[8], tabs[9]
    (n_used_ref, xs3_hbm, w1_hbm, b1_ref, w2_hbm, b2_ref, y3_ref,
     buf, xb, w1f, w2f, w1b, w2b, sem, wsem) = refs[10:]
    b = pl.program_id(0)
    n_used = n_used_ref[0]
    slot = lax.rem(b, 2)
    used = b < n_used
    wslot = lax.rem(group[b], 2)

    @pl.when(b == 0)
    def _():
        for copy in _expert_weight_copies(w1_hbm, w2_hbm, first_expert + blk_e[0], w1f, w2f, 0, wsem):
            copy.start()

    @pl.when(b == 0)
    def _():
        _expert_gather(tabs, 0, 0, xs3_hbm, buf, sem)

    @pl.when(used)
    def _():
        _expert_gather_wait(tabs, b, slot, xs3_hbm, buf, sem)

    @pl.when(b + 1 < n_used)
    def _():
        _expert_gather(tabs, b + 1, 1 - slot, xs3_hbm, buf, sem)

    new_expert = jnp.logical_or(b == 0, blk_e[b] != blk_e[jnp.maximum(b - 1, 0)])

    @pl.when(jnp.logical_and(used, new_expert))
    def _():
        for copy in _expert_weight_copies(w1_hbm, w2_hbm, first_expert + blk_e[b], w1f, w2f, wslot, wsem):
            copy.wait()

        @pl.when(next_e[b] >= 0)
        def _():
            nxt = first_expert + next_e[b]
            for copy in _expert_weight_copies(w1_hbm, w2_hbm, nxt, w1f, w2f, 1 - wslot, wsem):
                copy.start()

        w1b[...] = w1f[wslot].astype(BF16)
        w2b[...] = w2f[wslot].astype(BF16)

    @pl.when(used)
    def _():
        base = slot * (TM * SUBLANES)
        for s in range(ROW_TILES):
            xb[:, s * LANES:(s + 1) * LANES] = buf[pl.ds(base + s, TM, stride=SUBLANES), :].astype(BF16)
        x = xb[...]
        y = jnp.broadcast_to(b2_ref[0], (TM, D_MODEL))
        for c in range(0, D_FF, FF_CHUNK):
            cols = slice(c, c + FF_CHUNK)
            ucols = slice(D_FF + c, D_FF + c + FF_CHUNK)
            g = jnp.dot(x, w1b[:, cols], preferred_element_type=F32) + b1_ref[0, :, cols]
            u = jnp.dot(x, w1b[:, ucols], preferred_element_type=F32) + b1_ref[0, :, ucols]
            gate = jnp.minimum(g, SWIGLU_LIMIT)
            up = jnp.clip(u, -SWIGLU_LIMIT, SWIGLU_LIMIT)
            hh = (up + 1.0) * (gate * jax.nn.sigmoid(SWIGLU_ALPHA * gate))
            y = y + jnp.dot(hh.astype(BF16), w2b[cols, :], preferred_element_type=F32)
        for s in range(ROW_TILES):
            y3_ref[pl.ds(s, TM, stride=SUBLANES), :] = y[:, s * LANES:(s + 1) * LANES]

    @pl.when(jnp.logical_not(used))
    def _():
        y3_ref[...] = jnp.zeros_like(y3_ref)


def _expert_call(tabs, n_used, xs3, w1, b1, w2, b2, layer):
    n_blocks = tabs[0].shape[0]
    first_expert = layer * N_EXPERTS
    by_expert = lambda shape: pl.BlockSpec(shape, lambda b, e, *_: (first_expert + e[b], 0, 0))
    anywhere = pl.BlockSpec(memory_space=pl.ANY)
    grid_spec = pltpu.PrefetchScalarGridSpec(
        num_scalar_prefetch=len(tabs) + 1, grid=(n_blocks,),
        in_specs=[anywhere, anywhere, by_expert((1, 1, 2 * D_FF)), anywhere, by_expert((1, 1, D_MODEL))],
        out_specs=pl.BlockSpec((TM * SUBLANES, LANES), lambda b, *_: (b, 0)),
        scratch_shapes=[pltpu.VMEM((2 * TM * SUBLANES, LANES), F32),
                        pltpu.VMEM((TM, D_MODEL), BF16),
                        pltpu.VMEM((2, D_MODEL, 2 * D_FF), F32),
                        pltpu.VMEM((2, D_FF, D_MODEL), F32),
                        pltpu.VMEM((D_MODEL, 2 * D_FF), BF16),
                        pltpu.VMEM((D_FF, D_MODEL), BF16),
                        pltpu.SemaphoreType.DMA((2,)),
                        pltpu.SemaphoreType.DMA((2, 2))])
    return pl.pallas_call(
        functools.partial(_expert_kernel, first_expert=first_expert), grid_spec=grid_spec,
        out_shape=jax.ShapeDtypeStruct((n_blocks * TM * SUBLANES, LANES), F32),
        compiler_params=_params(vmem_limit=EXPERT_VMEM_LIMIT), name="expert_ffn")(
            *tabs, n_used, xs3, w1, b1, w2, b2)


PAIRS = TOP_K * TT


def _combine_gather(cnt, off, src, tile, slot, y3_hbm, buf, sem):
    def body(e, carry):
        n = cnt[tile * N_EXPERTS + e]

        @pl.when(n > 0)
        def _():
            pltpu.make_async_copy(y3_hbm.at[_rows(src[tile * N_EXPERTS + e], n)],
                                  buf.at[_rows(slot * PAIRS + off[tile * N_EXPERTS + e], n)],
                                  sem.at[slot]).start()
        return carry

    lax.fori_loop(0, N_EXPERTS, body, 0, unroll=4)


def _combine_kernel(cnt, off, src, y3_hbm, h_ref, aux_ref, g2_ref, b2_ref, op_ref, os_ref, buf, ys, sem,
                    *, n_prompt):
    i = pl.program_id(0)
    n_steps = pl.num_programs(0)
    slot = lax.rem(i, 2)

    @pl.when(i == 0)
    def _():
        _combine_gather(cnt, off, src, 0, 0, y3_hbm, buf, sem)

    pltpu.make_async_copy(y3_hbm.at[_rows(0, PAIRS)], buf.at[_rows(slot * PAIRS, PAIRS)],
                          sem.at[slot]).wait()

    @pl.when(i + 1 < n_steps)
    def _():
        _combine_gather(cnt, off, src, i + 1, 1 - slot, y3_hbm, buf, sem)

    base = slot * (PAIRS * SUBLANES)
    for s in range(ROW_TILES):
        ys[:, s * LANES:(s + 1) * LANES] = buf[pl.ds(base + s, PAIRS, stride=SUBLANES), :].astype(BF16)
    aux = aux_ref[...]
    dest = lax.broadcasted_iota(jnp.int32, (TT, PAIRS), 1).astype(F32)
    weights = jnp.zeros((TT, PAIRS), F32)
    for k in range(TOP_K):
        weights = weights + jnp.where(dest == aux[:, k:k + 1], aux[:, TOP_K + k:TOP_K + k + 1], 0.0)
    moe = jnp.dot(weights.astype(BF16), ys[...], preferred_element_type=F32)
    y = _ln_full(DN_ALPHA * h_ref[...] + moe, g2_ref[...], b2_ref[...])

    @pl.when(i < n_prompt)
    def _():
        op_ref[...] = y

    @pl.when(i >= n_prompt)
    def _():
        os_ref[...] = y


def _combine_call(cnt, off, src, y3, h, aux, g2, b2, tp):
    n = h.shape[0]
    row = lambda w: pl.BlockSpec((TT, w), lambda i, *_: (i, 0))
    n_prompt = tp // TT
    out_specs = _tile_row_specs(n_prompt)
    out_shape = [jax.ShapeDtypeStruct((tp, D_MODEL), F32), jax.ShapeDtypeStruct((n - tp, D_MODEL), F32)]
    grid_spec = pltpu.PrefetchScalarGridSpec(
        num_scalar_prefetch=3, grid=(n // TT,),
        in_specs=[pl.BlockSpec(memory_space=pl.ANY), row(D_MODEL), row(LANES),
                  pl.BlockSpec(g2.shape, lambda i, *_: (0, 0)), pl.BlockSpec(b2.shape, lambda i, *_: (0, 0))],
        out_specs=out_specs,
        scratch_shapes=[pltpu.VMEM((2 * PAIRS * SUBLANES, LANES), F32),
                        pltpu.VMEM((PAIRS, D_MODEL), BF16),
                        pltpu.SemaphoreType.DMA((2,))])
    return pl.pallas_call(
        functools.partial(_combine_kernel, n_prompt=n_prompt), grid_spec=grid_spec, out_shape=out_shape,
        compiler_params=_params(), name="combine_ln")(cnt, off, src, y3, h, aux, g2, b2)


def _routing_tables(cnt_tiles, n):
    cnt = cnt_tiles[:, 0, :N_EXPERTS]
    off = jnp.cumsum(cnt, axis=1) - cnt
    cum = jnp.cumsum(cnt, axis=0) - cnt
    tot = jnp.sum(cnt, axis=0)
    padded = (tot + TM - 1) // TM * TM
    pends = jnp.cumsum(padded)
    pstart = pends - padded
    n_blocks = n * TOP_K // TM + N_EXPERTS
    blk_start = jnp.arange(n_blocks, dtype=jnp.int32) * TM
    blk_e = jnp.minimum(jnp.sum((pends[None, :] <= blk_start[:, None]).astype(jnp.int32), axis=1),
                        N_EXPERTS - 1)
    ids = jnp.arange(N_EXPERTS, dtype=jnp.int32)
    mine = blk_e[:, None] == ids[None, :]
    pick = lambda per_expert: jnp.sum(jnp.where(mine, per_expert[None, :], 0), axis=1)
    blk_s0 = blk_start - pick(pstart)
    lo3, hi3, mine3 = blk_s0[:, None, None], blk_s0[:, None, None] + TM, mine[:, None, :]
    j_lo = jnp.sum(jnp.logical_and(mine3, (cum + cnt)[None] <= lo3).astype(jnp.int32), axis=(1, 2))
    j_hi = jnp.sum(jnp.logical_and(mine3, cum[None] < hi3).astype(jnp.int32), axis=(1, 2))
    n_used = (pends[-1:] // TM).astype(jnp.int32)
    has = tot > 0
    group_of = jnp.cumsum(has.astype(jnp.int32)) - 1
    later = jnp.where(jnp.logical_and(has[None, :], ids[None, :] > ids[:, None]), ids[None, :], N_EXPERTS)
    next_of = jnp.min(later, axis=1)
    next_of = jnp.where(next_of == N_EXPERTS, -1, next_of)
    flat = lambda a: a.reshape(-1).astype(jnp.int32)
    expert_tabs = (flat(blk_e), flat(blk_s0), flat(j_lo), flat(j_hi), flat(cnt), flat(off), flat(cum), flat(tot),
                   flat(pick(group_of)), flat(pick(next_of)))
    combine_tabs = (flat(cnt), flat(off), flat(pstart[None, :] + cum))
    return expert_tabs, n_used, combine_tabs


def _pad_cols(a, width):
    return jnp.pad(a, [(0, 0)] * (a.ndim - 1) + [(0, width - a.shape[-1])])


def _layer_params(l, w_in, b_in, w_dw, b_dw, ln_a_g, ln_a_b, ln_c_g, ln_c_b, w_sp, b_sp,
                  w_out, b_out, ln1_g, ln1_b, w_router, b_router, ln2_g, ln2_b):
    bounds = [0, C_B, 2 * C_B, 3 * C_B, 3 * C_B + C_A, 3 * C_B + 2 * C_A, 3 * C_B + 2 * C_A + C_C,
              3 * C_B + 2 * C_A + 2 * C_C]
    w_segs = [_pad_cols(w_in[l][:, bounds[i]:bounds[i + 1]], SEG) for i in range(N_SEG)]
    b_segs = [_pad_cols(b_in[l][bounds[i]:bounds[i + 1]], SEG) for i in range(N_SEG)]
    row = lambda a: _pad_cols(a, SEG)[None, :]
    wo = w_out[l]
    pad_rows = lambda a: jnp.pad(a, ((0, SEG - a.shape[0]), (0, 0)))
    bsp = b_sp[l]
    bfull = _pad_cols(jnp.repeat(bsp.T, HEAD_DIM, axis=1), SEG)
    wr = _pad_cols(w_router[l], LANES)
    wr_hi = wr.astype(BF16)
    return dict(
        w_in=jnp.concatenate(w_segs, axis=1).astype(BF16),
        b_in=jnp.concatenate(b_segs)[None, :],
        w_dw=jnp.pad(w_dw[l], ((0, HALO - CONV_W), (0, SEG - C_A))),
        b_dw=row(b_dw[l]), ln_a_g=row(ln_a_g[l]), ln_a_b=row(ln_a_b[l]),
        ln_c_g=row(ln_c_g[l]), ln_c_b=row(ln_c_b[l]),
        w_sp=w_sp[l], b_full=bfull,
        w_out=jnp.concatenate([pad_rows(wo[:C_A]), wo[C_A:C_A + C_B], pad_rows(wo[C_A + C_B:])]).astype(BF16),
        b_out=b_out[l][None, :],
        ln1_g=ln1_g[l][None, :], ln1_b=ln1_b[l][None, :],
        w_router=jnp.concatenate([wr_hi, (wr - wr_hi.astype(F32)).astype(BF16)], axis=1),
        b_router=jnp.concatenate([b_router[l], jnp.full((LANES - N_EXPERTS,), NEG, F32)])[None, :],
        ln2_g=ln2_g[l][None, :], ln2_b=ln2_b[l][None, :])


def _layer(x, p, hist_p, caches, experts, biases, n, tp, layer):
    q, k, v, a, c_out, vn_s = _proj_call(x, p["w_in"], p["b_in"], p["ln_c_g"], p["ln_c_b"],
                                         p["w_sp"], p["b_full"], n, tp)

    conv_args = (p["w_dw"], p["b_dw"], p["ln_a_g"], p["ln_a_b"])
    a_out = _conv_prompt_call(a, *conv_args, tp)
    a_out, new_a_s = _conv_sample_call(hist_p, a, *conv_args, a_out, tp)

    o_b = _attn_prompt_call(q, k, v, biases["prompt"], tp)
    o_b = _attn_sample_call(q, k, v, *caches, layer, *biases["sample"], o_b, tp)

    h, xs3, aux, cnt_tiles = _mix_call(x, a_out, o_b, c_out, p["w_out"], p["b_out"],
                                       p["ln1_g"], p["ln1_b"], p["w_router"], p["b_router"], tp)
    expert_tabs, n_used, combine_tabs = _routing_tables(cnt_tiles, n)
    y3 = _expert_call(expert_tabs, n_used, xs3, *experts, layer)
    y = _combine_call(*combine_tabs, y3, h, aux, p["ln2_g"], p["ln2_b"], tp)
    return y, a, k, v, vn_s, new_a_s


def kernel(x_prompt, x_sample, state_a_conv, cache_b_k, cache_b_v, w_in, b_in, w_dw, b_dw, ln_a_g, ln_a_b, ln_c_g, ln_c_b, w_sp, b_sp, w_out, b_out, ln1_g, ln1_b, w_router, b_router, w1, b1, w2, b2, ln2_g, ln2_b):
    bp, tp, _ = x_prompt.shape
    bs, ts, _ = x_sample.shape
    n_cache = cache_b_k.shape[2]
    assert bp == 1 and ts == DEC_SEQ and bs * ts == TT and tp % WIN_MAX == 0
    assert n_cache == WIN_MAX
    n = tp + bs * ts
    keep = min(WIN_MAX, tp)
    hist = CONV_W - 1

    weights = (w_in, b_in, w_dw, b_dw, ln_a_g, ln_a_b, ln_c_g, ln_c_b, w_sp, b_sp, w_out, b_out,
               ln1_g, ln1_b, w_router, b_router, ln2_g, ln2_b)
    biases = {"prompt": [_prompt_bias(d) for d in DILATIONS], "sample": _sample_bias(n_cache)}
    experts = (w1.reshape(DEPTH * N_EXPERTS, D_MODEL, 2 * D_FF), b1.reshape(DEPTH * N_EXPERTS, 1, 2 * D_FF),
               w2.reshape(DEPTH * N_EXPERTS, D_FF, D_MODEL), b2.reshape(DEPTH * N_EXPERTS, 1, D_MODEL))
    to_feature_major = lambda c: jnp.transpose(c, (0, 1, 3, 4, 2)).reshape(DEPTH, bs, C_B, n_cache)
    caches = (to_feature_major(cache_b_k), to_feature_major(cache_b_v))
    heads = lambda a: jnp.transpose(a, (1, 0, 2)).reshape(a.shape[1], N_HEADS_B, HEAD_DIM)

    x = (x_prompt[0], x_sample.reshape(bs * ts, D_MODEL))
    outs = {name: [] for name in ("a_p", "a_s", "k_p", "v_p", "k_s", "v_s", "c_s")}
    for l in range(DEPTH):
        p = _layer_params(l, *weights)
        hist_p = jnp.pad(state_a_conv[l], ((0, 0), (HALO - hist, 0), (0, SEG - C_A)))
        x, a, k, v, vn_s, new_a_s = _layer(x, p, hist_p, caches, experts, biases, n, tp, l)
        outs["a_p"].append(a[tp - hist:tp, :C_A][None])
        outs["a_s"].append(new_a_s[:, HALO - hist:, :C_A])
        outs["k_p"].append(heads(k[:, tp - keep:tp])[None])
        outs["v_p"].append(heads(v[:, tp - keep:tp])[None])
        outs["k_s"].append(heads(k[:, tp:]).reshape(bs, ts, N_HEADS_B, HEAD_DIM))
        outs["v_s"].append(heads(v[:, tp:]).reshape(bs, ts, N_HEADS_B, HEAD_DIM))
        outs["c_s"].append(vn_s[:, :C_C].reshape(bs, ts, C_C))
    stack = lambda name: jnp.stack(outs[name])
    y_prompt, y_sample = x
    return (y_prompt[None], y_sample.reshape(bs, ts, D_MODEL), stack("a_p"), stack("a_s"),
            stack("k_p"), stack("v_p"), stack("k_s"), stack("v_s"), stack("c_s"))
```

```python
import functools
import math

import jax
import jax.numpy as jnp
import numpy as np
from jax import lax
from jax.experimental import pallas as pl
from jax.experimental.pallas import tpu as pltpu

F32 = jnp.float32
BF16 = jnp.bfloat16

D_MODEL = 1024
HEAD_DIM = 64
N_HEADS_B = 6
C_B = N_HEADS_B * HEAD_DIM
C_A = 320
C_C = 320
N_GROUPS_C = 5
CONV_W = 31
DILATIONS = (1, 4, 16)
BAND = 128
WIN_MAX = 2048
CHUNK = 128
N_EXPERTS = 32
TOP_K = 4
D_FF = 1024
SWIGLU_LIMIT = 7.0
SWIGLU_ALPHA = 1.702
LN_EPS = 1e-5
DEPTH = 2
DN_ALPHA = (2.0 * DEPTH) ** 0.25
DEC_SEQ = 8

LANES = 128
SUBLANES = 8
SEG = 384
N_SEG = 7
N_PAIRS = C_B // LANES
ROW_TILES = D_MODEL // LANES
assert ROW_TILES == SUBLANES

TT = 256
TM = 512
FF_CHUNK = 1024
NEG = -1e30
LOG2E = math.log2(math.e)
MIB = 1024 * 1024
VMEM_LIMIT = 48 * MIB
EXPERT_VMEM_LIMIT = 56 * MIB


def _params(n_axes=1, vmem_limit=VMEM_LIMIT):
    return pltpu.CompilerParams(dimension_semantics=("arbitrary",) * n_axes,
                                vmem_limit_bytes=vmem_limit)


def _full(a):
    nd = a.ndim
    return pl.BlockSpec(a.shape, lambda *_: (0,) * nd)


def _ln_valid(x, g, b, n_valid):
    col = lax.broadcasted_iota(jnp.int32, x.shape, 1)
    ok = col < n_valid
    mu = jnp.sum(jnp.where(ok, x, 0.0), axis=-1, keepdims=True) / n_valid
    xc = jnp.where(ok, x - mu, 0.0)
    var = jnp.sum(xc * xc, axis=-1, keepdims=True) / n_valid
    return xc * lax.rsqrt(var + LN_EPS) * g + b


def _ln_full(x, g, b):
    mu = jnp.mean(x, axis=-1, keepdims=True)
    xc = x - mu
    var = jnp.mean(xc * xc, axis=-1, keepdims=True)
    return xc * lax.rsqrt(var + LN_EPS) * g + b


def _spatial_gate(vn, cu, w_ref, bfull, rows, sub):
    r = lax.broadcasted_iota(jnp.int32, (rows, rows), 0)
    c = lax.broadcasted_iota(jnp.int32, (rows, rows), 1)
    causal = c <= r
    if sub is not None:
        causal = jnp.logical_and(causal, lax.shift_right_logical(r, sub) == lax.shift_right_logical(c, sub))
        pos = lax.broadcasted_iota(jnp.int32, (rows, CHUNK), 0) & (2 ** sub - 1)
        spread = (pos == lax.broadcasted_iota(jnp.int32, (rows, CHUNK), 1)).astype(BF16)
        bfull = jnp.concatenate([bfull[0:2 ** sub]] * (rows // 2 ** sub), axis=0)
    lo_half = lax.broadcasted_iota(jnp.int32, (rows, LANES), 1) < HEAD_DIM
    cols = []
    for pair in range(SEG // LANES):
        cs = slice(pair * LANES, (pair + 1) * LANES)
        acc = jnp.zeros((rows, LANES), F32)
        for hh in range(2):
            g = pair * 2 + hh
            if g >= N_GROUPS_C:
                continue
            keep = lo_half if hh == 0 else jnp.logical_not(lo_half)
            w = w_ref[g].astype(BF16)
            if sub is not None:
                w = jnp.dot(spread, w, preferred_element_type=F32).astype(BF16)
                w = lax.dot_general(w, spread, (((1,), (1,)), ((), ())), preferred_element_type=F32).astype(BF16)
            wm = jnp.where(causal, w, 0.0)
            vm = jnp.where(keep, vn[:, cs], 0.0).astype(BF16)
            acc = acc + jnp.dot(wm, vm, preferred_element_type=F32)
        cols.append(cu[:, cs] * (acc + bfull[:, cs]))
    return jnp.concatenate(cols, axis=1)


def _tile_rows(xp_ref, xs_ref, n_prompt):
    return jnp.where(pl.program_id(0) < n_prompt, xp_ref[...], xs_ref[...])


def _tile_row_specs(n_prompt):
    return [pl.BlockSpec((TT, D_MODEL), lambda i, *_: (jnp.minimum(i, n_prompt - 1), 0)),
            pl.BlockSpec((TT, D_MODEL), lambda i, *_: (jnp.maximum(i - n_prompt, 0), 0))]


def _proj_kernel(xp_ref, xs_ref, w_ref, b_ref, lcg_ref, lcb_ref, wsp_ref, bsp_ref,
                 q_ref, k_ref, v_ref, a_ref, c_ref, vns_ref, *, n_prompt):
    x = _tile_rows(xp_ref, xs_ref, n_prompt)
    proj = jnp.dot(x.astype(BF16), w_ref[...], preferred_element_type=F32) + b_ref[...]
    seg = lambda i: proj[:, i * SEG:(i + 1) * SEG]

    for i, ref in enumerate((q_ref, k_ref, v_ref)):
        qkv = seg(i)
        for pair in range(N_PAIRS):
            ref[pair] = qkv[:, pair * LANES:(pair + 1) * LANES]
    a_ref[...] = seg(3) * jax.nn.sigmoid(seg(4))
    cu = seg(5)
    vn = _ln_valid(seg(6), lcg_ref[...], lcb_ref[...], C_C)
    is_prompt = pl.program_id(0) < n_prompt

    @pl.when(is_prompt)
    def _():
        for ch in range(0, TT, CHUNK):
            rows = slice(ch, ch + CHUNK)
            c_ref[rows, :] = _spatial_gate(vn[rows], cu[rows], wsp_ref, bsp_ref[...], CHUNK, None)

    @pl.when(jnp.logical_not(is_prompt))
    def _():
        c_ref[...] = _spatial_gate(vn, cu, wsp_ref, bsp_ref[...], TT, DEC_SEQ.bit_length() - 1)
        vns_ref[...] = vn


def _proj_call(x, w_in_p, b_in_p, lcg, lcb, w_sp, b_full, n, tp):
    row = lambda w: pl.BlockSpec((TT, w), lambda i: (i, 0))
    heads = pl.BlockSpec((N_PAIRS, TT, LANES), lambda i: (0, i, 0))
    out = jax.ShapeDtypeStruct((n, SEG), F32)
    out3 = jax.ShapeDtypeStruct((N_PAIRS, n, LANES), F32)
    consts = (w_in_p, b_in_p, lcg, lcb, w_sp, b_full)
    return pl.pallas_call(
        functools.partial(_proj_kernel, n_prompt=tp // TT), grid=(n // TT,),
        in_specs=_tile_row_specs(tp // TT) + [_full(c) for c in consts],
        out_specs=[heads] * 3 + [row(SEG)] * 2 + [pl.BlockSpec((TT, SEG), lambda i: (0, 0))],
        out_shape=[out3] * 3 + [out] * 2 + [jax.ShapeDtypeStruct((n - tp, SEG), F32)],
        compiler_params=_params(), name="in_proj")(*x, *consts)


HALO = 32
CONV_CHUNK = 32


def _conv_tail(acc, bdw_ref, g_ref, b_ref):
    return jax.nn.silu(_ln_valid(acc + bdw_ref[...], g_ref[...], b_ref[...], C_A))


def _conv_prompt_kernel(halo_ref, a_ref, w_ref, bdw_ref, g_ref, b_ref, o_ref, buf, shifted, *, n_prompt):
    i = pl.program_id(0)

    @pl.when(i < n_prompt)
    def _():
        buf[0:HALO, :] = jnp.where(i > 0, halo_ref[...], 0.0)
        buf[HALO:, :] = a_ref[...]
        n_rows = HALO + TT - SUBLANES
        for sh in range(1, SUBLANES):
            shifted[sh - 1, 0:n_rows, :] = buf[sh:sh + n_rows, :]
        base = HALO - (CONV_W - 1)
        for r in range(0, TT, CONV_CHUNK):
            acc = jnp.zeros((CONV_CHUNK, SEG), F32)
            for j in range(CONV_W):
                tiles, sh = divmod(base + j, SUBLANES)
                src = buf if sh == 0 else shifted.at[sh - 1]
                lo = r + tiles * SUBLANES
                acc = acc + w_ref[j:j + 1, :] * src[lo:lo + CONV_CHUNK, :]
            o_ref[r:r + CONV_CHUNK, :] = _conv_tail(acc, bdw_ref, g_ref, b_ref)

    @pl.when(i >= n_prompt)
    def _():
        o_ref[...] = jnp.zeros_like(o_ref)


def _conv_prompt_call(a, w_dw_p, bdw, g, b, tp):
    n = a.shape[0]
    per = TT // HALO
    halo = pl.BlockSpec((HALO, SEG), lambda i: (jnp.maximum(i * per - 1, 0), 0))
    row = pl.BlockSpec((TT, SEG), lambda i: (i, 0))
    return pl.pallas_call(
        functools.partial(_conv_prompt_kernel, n_prompt=tp // TT), grid=(n // TT,),
        in_specs=[halo, row, _full(w_dw_p), _full(bdw), _full(g), _full(b)],
        out_specs=row, out_shape=jax.ShapeDtypeStruct((n, SEG), F32),
        scratch_shapes=[pltpu.VMEM((HALO + TT, SEG), F32),
                        pltpu.VMEM((SUBLANES - 1, HALO + TT, SEG), F32)],
        compiler_params=_params(), name="conv_prompt")(a, a, w_dw_p, bdw, g, b)


SB = 8


def _conv_sample_kernel(hist_ref, a_ref, w_ref, bdw_ref, g_ref, b_ref, big_ref,
                        o_ref, newa_ref, buf):
    del big_ref
    bs = hist_ref.shape[0]
    buf[:, 0:HALO, :] = hist_ref[...]
    buf[:, HALO:, :] = a_ref[...].reshape(bs, DEC_SEQ, SEG)
    base = HALO - (CONV_W - 1)
    for s in range(0, bs, SB):
        acc = jnp.zeros((SB, DEC_SEQ, SEG), F32)
        for j in range(CONV_W):
            acc = acc + w_ref[j:j + 1, :][None] * buf[s:s + SB, base + j: base + j + DEC_SEQ, :]
        y = _conv_tail(acc.reshape(SB * DEC_SEQ, SEG), bdw_ref, g_ref, b_ref)
        o_ref[s * DEC_SEQ:(s + SB) * DEC_SEQ, :] = y
    newa_ref[...] = buf[:, DEC_SEQ:, :]


def _conv_sample_call(hist_p, a, w_dw_p, bdw, g, b, a_out, tp):
    bs = hist_p.shape[0]
    rows = bs * DEC_SEQ
    blk = pl.BlockSpec((rows, SEG), lambda i: (tp // rows, 0))
    return pl.pallas_call(
        _conv_sample_kernel, grid=(1,),
        in_specs=[_full(hist_p), blk, _full(w_dw_p), _full(bdw), _full(g), _full(b),
                  pl.BlockSpec(memory_space=pl.ANY)],
        out_specs=[blk, pl.BlockSpec((bs, HALO, SEG), lambda i: (0, 0, 0))],
        out_shape=[jax.ShapeDtypeStruct(a_out.shape, F32),
                   jax.ShapeDtypeStruct((bs, HALO, SEG), F32)],
        scratch_shapes=[pltpu.VMEM((bs, HALO + DEC_SEQ, SEG), F32)],
        input_output_aliases={6: 0},
        compiler_params=_params(), name="conv_sample")(hist_p, a, w_dw_p, bdw, g, b, a_out)


QB = 128
assert QB == BAND


SPAN = BAND * max(DILATIONS)
UNITS = SPAN // QB
UNIT_UNROLL = 8


def _attn_prompt_kernel(q_ref, kh_ref, kc_ref, vh_ref, vc_ref, b1_ref, b4_ref, b16_ref, o_ref,
                        kbuf, vbuf, oacc, lacc, *, n_prompt):
    is_prompt = pl.program_id(1) < n_prompt
    refs = (q_ref, kh_ref, kc_ref, vh_ref, vc_ref, (b1_ref, b4_ref, b16_ref), o_ref, kbuf, vbuf, oacc, lacc)
    pl.when(is_prompt)(functools.partial(_attn_prompt_span, refs))

    @pl.when(jnp.logical_not(is_prompt))
    def _():
        o_ref[...] = jnp.zeros_like(o_ref)


def _attn_prompt_span(refs):
    q_ref, kh_ref, kc_ref, vh_ref, vc_ref, bias_refs, o_ref, kbuf, vbuf, oacc, lacc = refs
    pair = pl.program_id(0)
    first_span = pl.program_id(1) == 0
    kbuf[0:SPAN, :] = kh_ref[0]
    kbuf[SPAN:, :] = kc_ref[0]
    vbuf[0:SPAN, :] = vh_ref[0]
    vbuf[SPAN:, :] = vc_ref[0]
    lane = lax.broadcasted_iota(jnp.int32, (QB, LANES), 1)
    lo_half = lane < HEAD_DIM

    for dil, bias_ref in zip(DILATIONS, bias_refs):
        def unit(u, carry, dil=dil, bias_ref=bias_ref):
            sub = u // dil
            t0 = sub * (QB * dil) + (u - sub * dil)
            if dil == 1:
                rows_q = pl.ds(pl.multiple_of(t0, QB), QB)
                rows_k = pl.ds(pl.multiple_of(SPAN + t0 - QB, QB), 2 * QB)
            else:
                rows_q = pl.ds(t0, QB, stride=dil)
                rows_k = pl.ds(SPAN + t0 - QB * dil, 2 * QB, stride=dil)
            table = jnp.logical_and(first_span, sub == 0).astype(jnp.int32) * N_HEADS_B + pair * 2
            qp = q_ref[0, rows_q, :] * (HEAD_DIM ** -0.5 * LOG2E)
            kk = kbuf[rows_k, :].astype(BF16)
            vv = vbuf[rows_k, :].astype(BF16)
            outs, lses = [], []
            for hh in range(2):
                keep = lo_half if hh == 0 else jnp.logical_not(lo_half)
                qm = jnp.where(keep, qp, 0.0).astype(BF16)
                s = lax.dot_general(qm, kk, (((1,), (1,)), ((), ())), preferred_element_type=F32)
                s = s + bias_ref[table + hh]
                m = jnp.max(s, axis=-1, keepdims=True)
                p = jnp.exp2(s - m)
                l = jnp.sum(p, axis=-1, keepdims=True)
                pv = jnp.dot(p.astype(BF16), vv, preferred_element_type=F32)
                outs.append(pv / l)
                lses.append(m + jnp.log2(l))
            o_new = jnp.where(lo_half, outs[0], outs[1])
            l_new = jnp.where(lo_half, lses[0], lses[1])
            if dil != 1:
                l_old = lacc[rows_q, :]
                mx = jnp.maximum(l_old, l_new)
                w_old = jnp.exp2(l_old - mx)
                w_new = jnp.exp2(l_new - mx)
                tot = w_old + w_new
                o_new = (w_old * oacc[rows_q, :] + w_new * o_new) / tot
                l_new = mx + jnp.log2(tot)
            oacc[rows_q, :] = o_new
            lacc[rows_q, :] = l_new
            return carry

        lax.fori_loop(0, UNITS, unit, 0, unroll=UNIT_UNROLL)
    o_ref[0] = oacc[...]


def _attn_prompt_call(q, k, v, biases, tp):
    n = q.shape[1]
    cur = pl.BlockSpec((1, SPAN, LANES), lambda p, m: (p, m, 0))
    halo = pl.BlockSpec((1, SPAN, LANES), lambda p, m: (p, jnp.maximum(m - 1, 0), 0))
    scratch = [pltpu.VMEM((2 * SPAN, LANES), F32), pltpu.VMEM((2 * SPAN, LANES), F32),
               pltpu.VMEM((SPAN, LANES), F32), pltpu.VMEM((SPAN, LANES), F32)]
    return pl.pallas_call(
        functools.partial(_attn_prompt_kernel, n_prompt=tp // SPAN),
        grid=(N_PAIRS, pl.cdiv(n, SPAN)),
        in_specs=[cur, halo, cur, halo, cur] + [_full(b) for b in biases],
        out_specs=cur, out_shape=jax.ShapeDtypeStruct((N_PAIRS, n, LANES), F32),
        scratch_shapes=scratch, compiler_params=_params(2), name="attn_prompt")(q, k, k, v, v, *biases)


def _alibi_slopes():
    return [2.0 ** (-8.0 * (h + 1) / N_HEADS_B) for h in range(N_HEADS_B)]


def _prompt_bias(dil):
    i = np.arange(QB)[:, None]
    j = np.arange(2 * QB)[None, :]
    rel = i + QB - j
    ok = np.logical_and(rel >= 0, rel <= BAND)
    slopes = np.asarray(_alibi_slopes(), np.float32)[:, None, None]
    pen = (-slopes * (rel * dil).astype(np.float32)[None]) * np.float32(LOG2E)
    full = np.where(ok[None], pen, np.float32(NEG)).astype(np.float32)
    start = np.where((j >= QB)[None], full, np.float32(NEG)).astype(np.float32)
    return jnp.asarray(np.concatenate([full, start], axis=0))


def _branch_multiplicity(dist):
    cnt = np.zeros(dist.shape, np.int32)
    for dil in DILATIONS:
        cnt = cnt + np.logical_and(dist % dil == 0, dist <= BAND * dil).astype(np.int32)
    return cnt


def _sample_bias(n_cache):
    t = np.arange(DEC_SEQ)[:, None]
    dist_c = n_cache + t - np.arange(n_cache)[None, :]
    dist_n = t - np.arange(LANES)[None, :]
    ok_n = np.logical_and(dist_n >= 0, np.arange(LANES)[None, :] < DEC_SEQ)
    slopes = np.asarray(_alibi_slopes(), np.float32)[:, None, None]

    def bias(dist, ok):
        mult = _branch_multiplicity(np.maximum(dist, 0))
        ok = np.logical_and(ok, mult > 0)
        val = -slopes * dist.astype(np.float32)[None] + np.log(np.maximum(mult, 1).astype(np.float32))[None]
        return np.where(ok[None], val, np.float32(NEG)).astype(np.float32)

    bc = bias(dist_c, np.ones(dist_c.shape, bool)).reshape(N_HEADS_B * DEC_SEQ, n_cache)
    bn = bias(dist_n, ok_n).reshape(N_HEADS_B * DEC_SEQ, LANES)
    return jnp.asarray(bc), jnp.asarray(bn)


def _attn_sample_kernel(q_ref, kn_ref, vn_ref, kc_ref, vc_ref, bc_ref, bn_ref, big_ref, o_ref):
    del big_ref
    rows = N_HEADS_B * DEC_SEQ
    wide = lambda ref: jnp.concatenate([ref[pair] for pair in range(N_PAIRS)], axis=1)
    q = wide(q_ref) * (HEAD_DIM ** -0.5)
    qrep = jnp.concatenate([q] * N_HEADS_B, axis=0)
    rh = lax.shift_right_logical(lax.broadcasted_iota(jnp.int32, (rows, SEG), 0), 3)
    ch = lax.shift_right_logical(lax.broadcasted_iota(jnp.int32, (rows, SEG), 1), 6)
    own = rh == ch
    qm = jnp.where(own, qrep, 0.0).astype(BF16)
    pad = jnp.zeros((LANES - DEC_SEQ, SEG), F32)
    kn = jnp.concatenate([wide(kn_ref), pad], axis=0).astype(BF16)
    vn = jnp.concatenate([wide(vn_ref), pad], axis=0).astype(BF16)
    nt = (((1,), (1,)), ((), ()))
    s_c = jnp.dot(qm, kc_ref[0, 0].astype(BF16), preferred_element_type=F32) + bc_ref[...]
    s_n = lax.dot_general(qm, kn, nt, preferred_element_type=F32) + bn_ref[...]
    m = jnp.maximum(jnp.max(s_c, axis=-1, keepdims=True), jnp.max(s_n, axis=-1, keepdims=True))
    p_c = jnp.exp(s_c - m)
    p_n = jnp.exp(s_n - m)
    l = jnp.sum(p_c, axis=-1, keepdims=True) + jnp.sum(p_n, axis=-1, keepdims=True)
    r = (lax.dot_general(p_c.astype(BF16), vc_ref[0, 0].astype(BF16), nt, preferred_element_type=F32)
         + jnp.dot(p_n.astype(BF16), vn, preferred_element_type=F32)) / l
    r = jnp.where(own, r, 0.0)
    o = r[0:DEC_SEQ]
    for h in range(1, N_HEADS_B):
        o = o + r[h * DEC_SEQ:(h + 1) * DEC_SEQ]
    for pair in range(N_PAIRS):
        o_ref[pair] = o[:, pair * LANES:(pair + 1) * LANES]


def _attn_sample_call(q, k, v, cache_kt, cache_vt, layer, bc, bn, o_big, tp):
    bs, n_cache = cache_kt.shape[1], cache_kt.shape[3]
    new = pl.BlockSpec((N_PAIRS, DEC_SEQ, LANES), lambda b: (0, tp // DEC_SEQ + b, 0))
    cache = pl.BlockSpec((1, 1, C_B, n_cache), lambda b: (layer, b, 0, 0))
    return pl.pallas_call(
        _attn_sample_kernel, grid=(bs,),
        in_specs=[new, new, new, cache, cache, _full(bc), _full(bn),
                  pl.BlockSpec(memory_space=pl.ANY)],
        out_specs=new, out_shape=jax.ShapeDtypeStruct(o_big.shape, F32),
        input_output_aliases={7: 0},
        compiler_params=_params(), name="attn_sample")(q, k, v, cache_kt, cache_vt, bc, bn, o_big)


def _mix_kernel(xp_ref, xs_ref, a_ref, o_ref, c_ref, wo_ref, bo_ref, g1_ref, b1_ref, wr_ref, br_ref,
                h_ref, xs3_ref, aux_ref, cnt_ref, *, n_prompt):
    mixed = jnp.concatenate([a_ref[...]] + [o_ref[pair] for pair in range(N_PAIRS)] + [c_ref[...]], axis=1)
    mix = jnp.dot(mixed.astype(BF16), wo_ref[...], preferred_element_type=F32) + bo_ref[...]
    h = _ln_full(DN_ALPHA * _tile_rows(xp_ref, xs_ref, n_prompt) + mix, g1_ref[...], b1_ref[...])
    h_ref[...] = h

    hb = h.astype(BF16)
    h_lo = (h - hb.astype(F32)).astype(BF16)
    both = jnp.dot(hb, wr_ref[...], preferred_element_type=F32)
    logits = (both[:, :LANES] + both[:, LANES:]
              + jnp.dot(h_lo, wr_ref[:, :LANES], preferred_element_type=F32) + br_ref[...])
    lane = lax.broadcasted_iota(jnp.int32, (TT, LANES), 1)
    lane_f = lane.astype(F32)
    vals, sels = [], []
    cur = logits
    for _ in range(TOP_K):
        m = jnp.max(cur, axis=-1, keepdims=True)
        idx = jnp.min(jnp.where(cur == m, lane_f, float(LANES)), axis=-1, keepdims=True)
        sel = lane_f == idx
        vals.append(m)
        sels.append(sel)
        cur = jnp.where(sel, -jnp.inf, cur)
    exps = [jnp.exp(v - vals[0]) for v in vals]
    den = exps[0] + exps[1] + exps[2] + exps[3]

    onehot = jnp.zeros((TT, LANES), F32)
    for sel in sels:
        onehot = onehot + sel.astype(F32)
    r = lax.broadcasted_iota(jnp.int32, (TT, TT), 0)
    c = lax.broadcasted_iota(jnp.int32, (TT, TT), 1)
    below = (c < r).astype(BF16)
    earlier = jnp.dot(below, onehot.astype(BF16), preferred_element_type=F32)
    cnt = jnp.broadcast_to(jnp.sum(onehot, axis=0, keepdims=True), (SUBLANES, LANES))
    er = lax.broadcasted_iota(jnp.int32, (LANES, LANES), 0)
    ec = lax.broadcasted_iota(jnp.int32, (LANES, LANES), 1)
    off = jnp.dot(cnt.astype(BF16), (er < ec).astype(BF16), preferred_element_type=F32)[0:1]
    place = earlier + off
    aux = jnp.zeros((TT, LANES), F32)
    rows = []
    for k in range(TOP_K):
        row = jnp.sum(jnp.where(sels[k], place, 0.0), axis=-1, keepdims=True)
        rows.append(row)
        aux = aux + jnp.where(lane == k, row, 0.0) + jnp.where(lane == TOP_K + k, exps[k] / den, 0.0)
    aux_ref[...] = aux
    cnt_ref[0] = cnt.astype(jnp.int32)

    aux_t = jnp.transpose(aux)
    dest = lax.broadcasted_iota(jnp.int32, (TOP_K * TT, TT), 0).astype(F32)
    disp = jnp.zeros((TOP_K * TT, TT), F32)
    for k in range(TOP_K):
        disp = jnp.where(dest == aux_t[k:k + 1, :], 1.0, disp)
    xs = jnp.dot(disp.astype(BF16), hb, preferred_element_type=F32)
    for s in range(ROW_TILES):
        xs3_ref[pl.ds(s, TOP_K * TT, stride=SUBLANES), :] = xs[:, s * LANES:(s + 1) * LANES]


def _mix_call(x, a_out, o_b, c_out, wo, bo, g1, b1, wr, br, tp):
    n = a_out.shape[0]
    row = lambda w: pl.BlockSpec((TT, w), lambda i: (i, 0))
    xs3 = pl.BlockSpec((TOP_K * TT * SUBLANES, LANES), lambda i: (i, 0))
    cnt = pl.BlockSpec((1, SUBLANES, LANES), lambda i: (i, 0, 0))
    return pl.pallas_call(
        functools.partial(_mix_kernel, n_prompt=tp // TT), grid=(n // TT,),
        in_specs=_tile_row_specs(tp // TT)
        + [row(SEG), pl.BlockSpec((N_PAIRS, TT, LANES), lambda i: (0, i, 0)), row(SEG),
           _full(wo), _full(bo), _full(g1), _full(b1), _full(wr), _full(br)],
        out_specs=[row(D_MODEL), xs3, row(LANES), cnt],
        out_shape=[jax.ShapeDtypeStruct((n, D_MODEL), F32),
                   jax.ShapeDtypeStruct((n * TOP_K * SUBLANES, LANES), F32),
                   jax.ShapeDtypeStruct((n, LANES), F32),
                   jax.ShapeDtypeStruct((n // TT, SUBLANES, LANES), jnp.int32)],
        compiler_params=_params(), name="mix_ln_router")(
            *x, a_out, o_b, c_out, wo, bo, g1, b1, wr, br)


def _rows(start_row, n_rows):
    return pl.ds(pl.multiple_of(start_row * SUBLANES, SUBLANES), n_rows * SUBLANES)


def _expert_gather(tabs, blk, slot, xs3_hbm, buf, sem):
    blk_e, blk_s0, j_lo, j_hi, cnt, off, cum, tot = tabs[:8]
    e = blk_e[blk]
    s0 = blk_s0[blk]
    base = slot * TM

    @pl.when(tot[e] - s0 < TM)
    def _():
        buf[_rows(base, TM), :] = jnp.zeros((TM * SUBLANES, LANES), F32)

    def body(j, carry):
        run = cum[j * N_EXPERTS + e]
        lo = jnp.maximum(run, s0)
        hi = jnp.minimum(run + cnt[j * N_EXPERTS + e], s0 + TM)

        @pl.when(hi > lo)
        def _():
            src = j * (TOP_K * TT) + off[j * N_EXPERTS + e] + (lo - run)
            pltpu.make_async_copy(xs3_hbm.at[_rows(src, hi - lo)],
                                  buf.at[_rows(base + lo - s0, hi - lo)], sem.at[slot]).start()
        return carry

    lax.fori_loop(j_lo[blk], j_hi[blk], body, 0)


def _expert_gather_wait(tabs, blk, slot, xs3_hbm, buf, sem):
    blk_e, blk_s0, tot = tabs[0], tabs[1], tabs[7]
    valid = jnp.minimum(tot[blk_e[blk]] - blk_s0[blk], TM)
    pltpu.make_async_copy(xs3_hbm.at[_rows(0, valid)], buf.at[_rows(slot * TM, valid)],
                          sem.at[slot]).wait()


def _expert_weight_copies(w1_hbm, w2_hbm, expert, w1f, w2f, wslot, wsem):
    return (pltpu.make_async_copy(w1_hbm.at[expert], w1f.at[wslot], wsem.at[0, wslot]),
            pltpu.make_async_copy(w2_hbm.at[expert], w2f.at[wslot], wsem.at[1, wslot]))


def _expert_kernel(*refs, first_expert):
    tabs = refs[:10]
    blk_e, group, next_e = tabs[0], tabs[8], tabs[9]
    (n_used_ref, xs3_hbm, w1_hbm, b1_ref, w2_hbm, b2_ref, y3_ref,
     buf, xb, w1f, w2f, w1b, w2b, sem, wsem) = refs[10:]
    b = pl.program_id(0)
    n_used = n_used_ref[0]
    slot = lax.rem(b, 2)
    used = b < n_used
    wslot = lax.rem(group[b], 2)

    @pl.when(b == 0)
    def _():
        for copy in _expert_weight_copies(w1_hbm, w2_hbm, first_expert + blk_e[0], w1f, w2f, 0, wsem):
            copy.start()

    @pl.when(b == 0)
    def _():
        _expert_gather(tabs, 0, 0, xs3_hbm, buf, sem)

    @pl.when(used)
    def _():
        _expert_gather_wait(tabs, b, slot, xs3_hbm, buf, sem)

    @pl.when(b + 1 < n_used)
    def _():
        _expert_gather(tabs, b + 1, 1 - slot, xs3_hbm, buf, sem)

    new_expert = jnp.logical_or(b == 0, blk_e[b] != blk_e[jnp.maximum(b - 1, 0)])

    @pl.when(jnp.logical_and(used, new_expert))
    def _():
        for copy in _expert_weight_copies(w1_hbm, w2_hbm, first_expert + blk_e[b], w1f, w2f, wslot, wsem):
            copy.wait()

        @pl.when(next_e[b] >= 0)
        def _():
            nxt = first_expert + next_e[b]
            for copy in _expert_weight_copies(w1_hbm, w2_hbm, nxt, w1f, w2f, 1 - wslot, wsem):
                copy.start()

        w1b[...] = w1f[wslot].astype(BF16)
        w2b[...] = w2f[wslot].astype(BF16)

    @pl.when(used)
    def _():
        base = slot * (TM * SUBLANES)
        for s in range(ROW_TILES):
            xb[:, s * LANES:(s + 1) * LANES] = buf[pl.ds(base + s, TM, stride=SUBLANES), :].astype(BF16)
        x = xb[...]
        y = jnp.broadcast_to(b2_ref[0], (TM, D_MODEL))
        for c in range(0, D_FF, FF_CHUNK):
            cols = slice(c, c + FF_CHUNK)
            ucols = slice(D_FF + c, D_FF + c + FF_CHUNK)
            g = jnp.dot(x, w1b[:, cols], preferred_element_type=F32) + b1_ref[0, :, cols]
            u = jnp.dot(x, w1b[:, ucols], preferred_element_type=F32) + b1_ref[0, :, ucols]
            gate = jnp.minimum(g, SWIGLU_LIMIT)
            up = jnp.clip(u, -SWIGLU_LIMIT, SWIGLU_LIMIT)
            hh = (up + 1.0) * (gate * jax.nn.sigmoid(SWIGLU_ALPHA * gate))
            y = y + jnp.dot(hh.astype(BF16), w2b[cols, :], preferred_element_type=F32)
        for s in range(ROW_TILES):
            y3_ref[pl.ds(s, TM, stride=SUBLANES), :] = y[:, s * LANES:(s + 1) * LANES]

    @pl.when(jnp.logical_not(used))
    def _():
        y3_ref[...] = jnp.zeros_like(y3_ref)


def _expert_call(tabs, n_used, xs3, w1, b1, w2, b2, layer):
    n_blocks = tabs[0].shape[0]
    first_expert = layer * N_EXPERTS
    by_expert = lambda shape: pl.BlockSpec(shape, lambda b, e, *_: (first_expert + e[b], 0, 0))
    anywhere = pl.BlockSpec(memory_space=pl.ANY)
    grid_spec = pltpu.PrefetchScalarGridSpec(
        num_scalar_prefetch=len(tabs) + 1, grid=(n_blocks,),
        in_specs=[anywhere, anywhere, by_expert((1, 1, 2 * D_FF)), anywhere, by_expert((1, 1, D_MODEL))],
        out_specs=pl.BlockSpec((TM * SUBLANES, LANES), lambda b, *_: (b, 0)),
        scratch_shapes=[pltpu.VMEM((2 * TM * SUBLANES, LANES), F32),
                        pltpu.VMEM((TM, D_MODEL), BF16),
                        pltpu.VMEM((2, D_MODEL, 2 * D_FF), F32),
                        pltpu.VMEM((2, D_FF, D_MODEL), F32),
                        pltpu.VMEM((D_MODEL, 2 * D_FF), BF16),
                        pltpu.VMEM((D_FF, D_MODEL), BF16),
                        pltpu.SemaphoreType.DMA((2,)),
                        pltpu.SemaphoreType.DMA((2, 2))])
    return pl.pallas_call(
        functools.partial(_expert_kernel, first_expert=first_expert), grid_spec=grid_spec,
        out_shape=jax.ShapeDtypeStruct((n_blocks * TM * SUBLANES, LANES), F32),
        compiler_params=_params(vmem_limit=EXPERT_VMEM_LIMIT), name="expert_ffn")(
            *tabs, n_used, xs3, w1, b1, w2, b2)


PAIRS = TOP_K * TT


def _combine_gather(cnt, off, src, tile, slot, y3_hbm, buf, sem):
    def body(e, carry):
        n = cnt[tile * N_EXPERTS + e]

        @pl.when(n > 0)
        def _():
            pltpu.make_async_copy(y3_hbm.at[_rows(src[tile * N_EXPERTS + e], n)],
                                  buf.at[_rows(slot * PAIRS + off[tile * N_EXPERTS + e], n)],
                                  sem.at[slot]).start()
        return carry

    lax.fori_loop(0, N_EXPERTS, body, 0, unroll=4)


def _combine_kernel(cnt, off, src, y3_hbm, h_ref, aux_ref, g2_ref, b2_ref, op_ref, os_ref, buf, ys, sem,
                    *, n_prompt):
    i = pl.program_id(0)
    n_steps = pl.num_programs(0)
    slot = lax.rem(i, 2)

    @pl.when(i == 0)
    def _():
        _combine_gather(cnt, off, src, 0, 0, y3_hbm, buf, sem)

    pltpu.make_async_copy(y3_hbm.at[_rows(0, PAIRS)], buf.at[_rows(slot * PAIRS, PAIRS)],
                          sem.at[slot]).wait()

    @pl.when(i + 1 < n_steps)
    def _():
        _combine_gather(cnt, off, src, i + 1, 1 - slot, y3_hbm, buf, sem)

    base = slot * (PAIRS * SUBLANES)
    for s in range(ROW_TILES):
        ys[:, s * LANES:(s + 1) * LANES] = buf[pl.ds(base + s, PAIRS, stride=SUBLANES), :].astype(BF16)
    aux = aux_ref[...]
    dest = lax.broadcasted_iota(jnp.int32, (TT, PAIRS), 1).astype(F32)
    weights = jnp.zeros((TT, PAIRS), F32)
    for k in range(TOP_K):
        weights = jnp.where(dest == aux[:, k:k + 1], aux[:, TOP_K + k:TOP_K + k + 1], weights)
    moe = jnp.dot(weights.astype(BF16), ys[...], preferred_element_type=F32)
    y = _ln_full(DN_ALPHA * h_ref[...] + moe, g2_ref[...], b2_ref[...])

    @pl.when(i < n_prompt)
    def _():
        op_ref[...] = y

    @pl.when(i >= n_prompt)
    def _():
        os_ref[...] = y


def _combine_call(cnt, off, src, y3, h, aux, g2, b2, tp):
    n = h.shape[0]
    row = lambda w: pl.BlockSpec((TT, w), lambda i, *_: (i, 0))
    n_prompt = tp // TT
    out_specs = _tile_row_specs(n_prompt)
    out_shape = [jax.ShapeDtypeStruct((tp, D_MODEL), F32), jax.ShapeDtypeStruct((n - tp, D_MODEL), F32)]
    grid_spec = pltpu.PrefetchScalarGridSpec(
        num_scalar_prefetch=3, grid=(n // TT,),
        in_specs=[pl.BlockSpec(memory_space=pl.ANY), row(D_MODEL), row(LANES),
                  pl.BlockSpec(g2.shape, lambda i, *_: (0, 0)), pl.BlockSpec(b2.shape, lambda i, *_: (0, 0))],
        out_specs=out_specs,
        scratch_shapes=[pltpu.VMEM((2 * PAIRS * SUBLANES, LANES), F32),
                        pltpu.VMEM((PAIRS, D_MODEL), BF16),
                        pltpu.SemaphoreType.DMA((2,))])
    return pl.pallas_call(
        functools.partial(_combine_kernel, n_prompt=n_prompt), grid_spec=grid_spec, out_shape=out_shape,
        compiler_params=_params(), name="combine_ln")(cnt, off, src, y3, h, aux, g2, b2)


def _routing_tables(cnt_tiles, n):
    cnt = cnt_tiles[:, 0, :N_EXPERTS]
    off = jnp.cumsum(cnt, axis=1) - cnt
    cum = jnp.cumsum(cnt, axis=0) - cnt
    tot = jnp.sum(cnt, axis=0)
    padded = (tot + TM - 1) // TM * TM
    pends = jnp.cumsum(padded)
    pstart = pends - padded
    n_blocks = n * TOP_K // TM + N_EXPERTS
    blk_start = jnp.arange(n_blocks, dtype=jnp.int32) * TM
    blk_e = jnp.minimum(jnp.sum((pends[None, :] <= blk_start[:, None]).astype(jnp.int32), axis=1),
                        N_EXPERTS - 1)
    ids = jnp.arange(N_EXPERTS, dtype=jnp.int32)
    mine = blk_e[:, None] == ids[None, :]
    pick = lambda per_expert: jnp.sum(jnp.where(mine, per_expert[None, :], 0), axis=1)
    blk_s0 = blk_start - pick(pstart)
    lo3, hi3, mine3 = blk_s0[:, None, None], blk_s0[:, None, None] + TM, mine[:, None, :]
    j_lo = jnp.sum(jnp.logical_and(mine3, (cum + cnt)[None] <= lo3).astype(jnp.int32), axis=(1, 2))
    j_hi = jnp.sum(jnp.logical_and(mine3, cum[None] < hi3).astype(jnp.int32), axis=(1, 2))
    n_used = (pends[-1:] // TM).astype(jnp.int32)
    has = tot > 0
    group_of = jnp.cumsum(has.astype(jnp.int32)) - 1
    later = jnp.where(jnp.logical_and(has[None, :], ids[None, :] > ids[:, None]), ids[None, :], N_EXPERTS)
    next_of = jnp.min(later, axis=1)
    next_of = jnp.where(next_of == N_EXPERTS, -1, next_of)
    flat = lambda a: a.reshape(-1).astype(jnp.int32)
    expert_tabs = (flat(blk_e), flat(blk_s0), flat(j_lo), flat(j_hi), flat(cnt), flat(off), flat(cum), flat(tot),
                   flat(pick(group_of)), flat(pick(next_of)))
    combine_tabs = (flat(cnt), flat(off), flat(pstart[None, :] + cum))
    return expert_tabs, n_used, combine_tabs


def _pad_cols(a, width):
    return jnp.pad(a, [(0, 0)] * (a.ndim - 1) + [(0, width - a.shape[-1])])


def _layer_params(l, w_in, b_in, w_dw, b_dw, ln_a_g, ln_a_b, ln_c_g, ln_c_b, w_sp, b_sp,
                  w_out, b_out, ln1_g, ln1_b, w_router, b_router, ln2_g, ln2_b):
    bounds = [0, C_B, 2 * C_B, 3 * C_B, 3 * C_B + C_A, 3 * C_B + 2 * C_A, 3 * C_B + 2 * C_A + C_C,
              3 * C_B + 2 * C_A + 2 * C_C]
    w_segs = [_pad_cols(w_in[l][:, bounds[i]:bounds[i + 1]], SEG) for i in range(N_SEG)]
    b_segs = [_pad_cols(b_in[l][bounds[i]:bounds[i + 1]], SEG) for i in range(N_SEG)]
    row = lambda a: _pad_cols(a, SEG)[None, :]
    wo = w_out[l]
    pad_rows = lambda a: jnp.pad(a, ((0, SEG - a.shape[0]), (0, 0)))
    bsp = b_sp[l]
    bfull = _pad_cols(jnp.repeat(bsp.T, HEAD_DIM, axis=1), SEG)
    wr = _pad_cols(w_router[l], LANES)
    wr_hi = wr.astype(BF16)
    return dict(
        w_in=jnp.concatenate(w_segs, axis=1).astype(BF16),
        b_in=jnp.concatenate(b_segs)[None, :],
        w_dw=jnp.pad(w_dw[l], ((0, HALO - CONV_W), (0, SEG - C_A))),
        b_dw=row(b_dw[l]), ln_a_g=row(ln_a_g[l]), ln_a_b=row(ln_a_b[l]),
        ln_c_g=row(ln_c_g[l]), ln_c_b=row(ln_c_b[l]),
        w_sp=w_sp[l], b_full=bfull,
        w_out=jnp.concatenate([pad_rows(wo[:C_A]), wo[C_A:C_A + C_B], pad_rows(wo[C_A + C_B:])]).astype(BF16),
        b_out=b_out[l][None, :],
        ln1_g=ln1_g[l][None, :], ln1_b=ln1_b[l][None, :],
        w_router=jnp.concatenate([wr_hi, (wr - wr_hi.astype(F32)).astype(BF16)], axis=1),
        b_router=jnp.concatenate([b_router[l], jnp.full((LANES - N_EXPERTS,), NEG, F32)])[None, :],
        ln2_g=ln2_g[l][None, :], ln2_b=ln2_b[l][None, :])


def _layer(x, p, hist_p, caches, experts, biases, n, tp, layer):
    q, k, v, a, c_out, vn_s = _proj_call(x, p["w_in"], p["b_in"], p["ln_c_g"], p["ln_c_b"],
                                         p["w_sp"], p["b_full"], n, tp)

    conv_args = (p["w_dw"], p["b_dw"], p["ln_a_g"], p["ln_a_b"])
    a_out = _conv_prompt_call(a, *conv_args, tp)
    a_out, new_a_s = _conv_sample_call(hist_p, a, *conv_args, a_out, tp)

    o_b = _attn_prompt_call(q, k, v, biases["prompt"], tp)
    o_b = _attn_sample_call(q, k, v, *caches, layer, *biases["sample"], o_b, tp)

    h, xs3, aux, cnt_tiles = _mix_call(x, a_out, o_b, c_out, p["w_out"], p["b_out"],
                                       p["ln1_g"], p["ln1_b"], p["w_router"], p["b_router"], tp)
    expert_tabs, n_used, combine_tabs = _routing_tables(cnt_tiles, n)
    y3 = _expert_call(expert_tabs, n_used, xs3, *experts, layer)
    y = _combine_call(*combine_tabs, y3, h, aux, p["ln2_g"], p["ln2_b"], tp)
    return y, a, k, v, vn_s, new_a_s


def kernel(x_prompt, x_sample, state_a_conv, cache_b_k, cache_b_v, w_in, b_in, w_dw, b_dw, ln_a_g, ln_a_b, ln_c_g, ln_c_b, w_sp, b_sp, w_out, b_out, ln1_g, ln1_b, w_router, b_router, w1, b1, w2, b2, ln2_g, ln2_b):
    bp, tp, _ = x_prompt.shape
    bs, ts, _ = x_sample.shape
    n_cache = cache_b_k.shape[2]
    assert bp == 1 and ts == DEC_SEQ and bs * ts == TT and tp % WIN_MAX == 0
    assert n_cache == WIN_MAX
    n = tp + bs * ts
    keep = min(WIN_MAX, tp)
    hist = CONV_W - 1

    weights = (w_in, b_in, w_dw, b_dw, ln_a_g, ln_a_b, ln_c_g, ln_c_b, w_sp, b_sp, w_out, b_out,
               ln1_g, ln1_b, w_router, b_router, ln2_g, ln2_b)
    biases = {"prompt": [_prompt_bias(d) for d in DILATIONS], "sample": _sample_bias(n_cache)}
    experts = (w1.reshape(DEPTH * N_EXPERTS, D_MODEL, 2 * D_FF), b1.reshape(DEPTH * N_EXPERTS, 1, 2 * D_FF),
               w2.reshape(DEPTH * N_EXPERTS, D_FF, D_MODEL), b2.reshape(DEPTH * N_EXPERTS, 1, D_MODEL))
    to_feature_major = lambda c: jnp.transpose(c, (0, 1, 3, 4, 2)).reshape(DEPTH, bs, C_B, n_cache)
    caches = (to_feature_major(cache_b_k), to_feature_major(cache_b_v))
    heads = lambda a: jnp.transpose(a, (1, 0, 2)).reshape(a.shape[1], N_HEADS_B, HEAD_DIM)

    x = (x_prompt[0], x_sample.reshape(bs * ts, D_MODEL))
    outs = {name: [] for name in ("a_p", "a_s", "k_p", "v_p", "k_s", "v_s", "c_s")}
    for l in range(DEPTH):
        p = _layer_params(l, *weights)
        hist_p = jnp.pad(state_a_conv[l], ((0, 0), (HALO - hist, 0), (0, SEG - C_A)))
        x, a, k, v, vn_s, new_a_s = _layer(x, p, hist_p, caches, experts, biases, n, tp, l)
        outs["a_p"].append(a[tp - hist:tp, :C_A][None])
        outs["a_s"].append(new_a_s[:, HALO - hist:, :C_A])
        outs["k_p"].append(heads(k[:, tp - keep:tp])[None])
        outs["v_p"].append(heads(v[:, tp - keep:tp])[None])
        outs["k_s"].append(heads(k[:, tp:]).reshape(bs, ts, N_HEADS_B, HEAD_DIM))
        outs["v_s"].append(heads(v[:, tp:]).reshape(bs, ts, N_HEADS_B, HEAD_DIM))
        outs["c_s"].append(vn_s[:, :C_C].reshape(bs, ts, C_C))
    stack = lambda name: jnp.stack(outs[name])
    y_prompt, y_sample = x
    return (y_prompt[None], y_sample.reshape(bs, ts, D_MODEL), stack("a_p"), stack("a_s"),
            stack("k_p"), stack("v_p"), stack("k_s"), stack("v_s"), stack("c_s"))
```

```python
import functools
import math

import jax
import jax.numpy as jnp
import numpy as np
from jax import lax
from jax.experimental import pallas as pl
from jax.experimental.pallas import tpu as pltpu

F32 = jnp.float32
BF16 = jnp.bfloat16

D_MODEL = 1024
HEAD_DIM = 64
N_HEADS_B = 6
C_B = N_HEADS_B * HEAD_DIM
C_A = 320
C_C = 320
N_GROUPS_C = 5
CONV_W = 31
DILATIONS = (1, 4, 16)
BAND = 128
WIN_MAX = 2048
CHUNK = 128
N_EXPERTS = 32
TOP_K = 4
D_FF = 1024
SWIGLU_LIMIT = 7.0
SWIGLU_ALPHA = 1.702
LN_EPS = 1e-5
DEPTH = 2
DN_ALPHA = (2.0 * DEPTH) ** 0.25
DEC_SEQ = 8

LANES = 128
SUBLANES = 8
SEG = 384
N_SEG = 7
N_PAIRS = C_B // LANES
ROW_TILES = D_MODEL // LANES
assert ROW_TILES == SUBLANES

TT = 256
TM = 512
FF_CHUNK = 1024
NEG = -1e30
LOG2E = math.log2(math.e)
MIB = 1024 * 1024
VMEM_LIMIT = 48 * MIB
EXPERT_VMEM_LIMIT = 56 * MIB


def _params(n_axes=1, vmem_limit=VMEM_LIMIT):
    return pltpu.CompilerParams(dimension_semantics=("arbitrary",) * n_axes,
                                vmem_limit_bytes=vmem_limit)


def _full(a):
    nd = a.ndim
    return pl.BlockSpec(a.shape, lambda *_: (0,) * nd)


def _ln_valid(x, g, b, n_valid):
    col = lax.broadcasted_iota(jnp.int32, x.shape, 1)
    ok = col < n_valid
    mu = jnp.sum(jnp.where(ok, x, 0.0), axis=-1, keepdims=True) / n_valid
    xc = jnp.where(ok, x - mu, 0.0)
    var = jnp.sum(xc * xc, axis=-1, keepdims=True) / n_valid
    return xc * lax.rsqrt(var + LN_EPS) * g + b


def _ln_full(x, g, b):
    mu = jnp.mean(x, axis=-1, keepdims=True)
    xc = x - mu
    var = jnp.mean(xc * xc, axis=-1, keepdims=True)
    return xc * lax.rsqrt(var + LN_EPS) * g + b


def _spatial_gate(vn, cu, w_ref, bfull, rows, sub):
    r = lax.broadcasted_iota(jnp.int32, (rows, rows), 0)
    c = lax.broadcasted_iota(jnp.int32, (rows, rows), 1)
    causal = c <= r
    if sub is not None:
        causal = jnp.logical_and(causal, lax.shift_right_logical(r, sub) == lax.shift_right_logical(c, sub))
        pos = lax.broadcasted_iota(jnp.int32, (rows, CHUNK), 0) & (2 ** sub - 1)
        spread = (pos == lax.broadcasted_iota(jnp.int32, (rows, CHUNK), 1)).astype(BF16)
        bfull = jnp.concatenate([bfull[0:2 ** sub]] * (rows // 2 ** sub), axis=0)
    lo_half = lax.broadcasted_iota(jnp.int32, (rows, LANES), 1) < HEAD_DIM
    cols = []
    for pair in range(SEG // LANES):
        cs = slice(pair * LANES, (pair + 1) * LANES)
        acc = jnp.zeros((rows, LANES), F32)
        for hh in range(2):
            g = pair * 2 + hh
            if g >= N_GROUPS_C:
                continue
            keep = lo_half if hh == 0 else jnp.logical_not(lo_half)
            w = w_ref[g].astype(BF16)
            if sub is not None:
                w = jnp.dot(spread, w, preferred_element_type=F32).astype(BF16)
                w = lax.dot_general(w, spread, (((1,), (1,)), ((), ())), preferred_element_type=F32).astype(BF16)
            wm = jnp.where(causal, w, 0.0)
            vm = jnp.where(keep, vn[:, cs], 0.0).astype(BF16)
            acc = acc + jnp.dot(wm, vm, preferred_element_type=F32)
        cols.append(cu[:, cs] * (acc + bfull[:, cs]))
    return jnp.concatenate(cols, axis=1)


def _tile_rows(xp_ref, xs_ref, n_prompt):
    return jnp.where(pl.program_id(0) < n_prompt, xp_ref[...], xs_ref[...])


def _tile_row_specs(n_prompt):
    return [pl.BlockSpec((TT, D_MODEL), lambda i, *_: (jnp.minimum(i, n_prompt - 1), 0)),
            pl.BlockSpec((TT, D_MODEL), lambda i, *_: (jnp.maximum(i - n_prompt, 0), 0))]


def _proj_kernel(xp_ref, xs_ref, w_ref, b_ref, lcg_ref, lcb_ref, wsp_ref, bsp_ref,
                 wdw_ref, bdw_ref, lag_ref, lab_ref,
                 q_ref, k_ref, v_ref, a_ref, ao_ref, c_ref, vns_ref, abuf, shifted, *, n_prompt):
    x = _tile_rows(xp_ref, xs_ref, n_prompt)
    proj = jnp.dot(x.astype(BF16), w_ref[...], preferred_element_type=F32) + b_ref[...]
    seg = lambda i: proj[:, i * SEG:(i + 1) * SEG]

    for i, ref in enumerate((q_ref, k_ref, v_ref)):
        qkv = seg(i)
        for pair in range(N_PAIRS):
            ref[pair] = qkv[:, pair * LANES:(pair + 1) * LANES]
    a = seg(3) * jax.nn.sigmoid(seg(4))
    a_ref[...] = a
    cu = seg(5)
    vn = _ln_valid(seg(6), lcg_ref[...], lcb_ref[...], C_C)
    i = pl.program_id(0)
    is_prompt = i < n_prompt

    @pl.when(i == 0)
    def _():
        abuf[0:HALO, :] = jnp.zeros((HALO, SEG), F32)

    @pl.when(jnp.logical_and(i > 0, is_prompt))
    def _():
        abuf[0:HALO, :] = abuf[TT:TT + HALO, :]

    @pl.when(is_prompt)
    def _():
        abuf[HALO:, :] = a
        _conv_tile(abuf, shifted, wdw_ref, bdw_ref, lag_ref, lab_ref, ao_ref)
        for ch in range(0, TT, CHUNK):
            rows = slice(ch, ch + CHUNK)
            c_ref[rows, :] = _spatial_gate(vn[rows], cu[rows], wsp_ref, bsp_ref[...], CHUNK, None)

    @pl.when(jnp.logical_not(is_prompt))
    def _():
        ao_ref[...] = jnp.zeros_like(ao_ref)
        c_ref[...] = _spatial_gate(vn, cu, wsp_ref, bsp_ref[...], TT, DEC_SEQ.bit_length() - 1)
        vns_ref[...] = vn


def _proj_call(x, w_in_p, b_in_p, lcg, lcb, w_sp, b_full, w_dw_p, bdw, lag, lab, n, tp):
    row = lambda w: pl.BlockSpec((TT, w), lambda i: (i, 0))
    heads = pl.BlockSpec((N_PAIRS, TT, LANES), lambda i: (0, i, 0))
    out = jax.ShapeDtypeStruct((n, SEG), F32)
    out3 = jax.ShapeDtypeStruct((N_PAIRS, n, LANES), F32)
    consts = (w_in_p, b_in_p, lcg, lcb, w_sp, b_full, w_dw_p, bdw, lag, lab)
    return pl.pallas_call(
        functools.partial(_proj_kernel, n_prompt=tp // TT), grid=(n // TT,),
        in_specs=_tile_row_specs(tp // TT) + [_full(c) for c in consts],
        out_specs=[heads] * 3 + [row(SEG)] * 3 + [pl.BlockSpec((TT, SEG), lambda i: (0, 0))],
        out_shape=[out3] * 3 + [out] * 3 + [jax.ShapeDtypeStruct((n - tp, SEG), F32)],
        scratch_shapes=[pltpu.VMEM((HALO + TT, SEG), F32),
                        pltpu.VMEM((SUBLANES - 1, HALO + TT, SEG), F32)],
        compiler_params=_params(), name="in_proj")(*x, *consts)


HALO = 32
CONV_CHUNK = 32


def _conv_tail(acc, bdw_ref, g_ref, b_ref):
    return jax.nn.silu(_ln_valid(acc + bdw_ref[...], g_ref[...], b_ref[...], C_A))


def _conv_tile(buf, shifted, w_ref, bdw_ref, g_ref, b_ref, o_ref):
    n_rows = HALO + TT - SUBLANES
    for sh in range(1, SUBLANES):
        shifted[sh - 1, 0:n_rows, :] = buf[sh:sh + n_rows, :]
    base = HALO - (CONV_W - 1)
    for r in range(0, TT, CONV_CHUNK):
        acc = jnp.zeros((CONV_CHUNK, SEG), F32)
        for j in range(CONV_W):
            tiles, sh = divmod(base + j, SUBLANES)
            src = buf if sh == 0 else shifted.at[sh - 1]
            lo = r + tiles * SUBLANES
            acc = acc + w_ref[j:j + 1, :] * src[lo:lo + CONV_CHUNK, :]
        o_ref[r:r + CONV_CHUNK, :] = _conv_tail(acc, bdw_ref, g_ref, b_ref)


SB = 8


def _conv_sample_kernel(hist_ref, a_ref, w_ref, bdw_ref, g_ref, b_ref, big_ref,
                        o_ref, newa_ref, buf):
    del big_ref
    bs = hist_ref.shape[0]
    buf[:, 0:HALO, :] = hist_ref[...]
    buf[:, HALO:, :] = a_ref[...].reshape(bs, DEC_SEQ, SEG)
    base = HALO - (CONV_W - 1)
    for s in range(0, bs, SB):
        acc = jnp.zeros((SB, DEC_SEQ, SEG), F32)
        for j in range(CONV_W):
            acc = acc + w_ref[j:j + 1, :][None] * buf[s:s + SB, base + j: base + j + DEC_SEQ, :]
        y = _conv_tail(acc.reshape(SB * DEC_SEQ, SEG), bdw_ref, g_ref, b_ref)
        o_ref[s * DEC_SEQ:(s + SB) * DEC_SEQ, :] = y
    newa_ref[...] = buf[:, DEC_SEQ:, :]


def _conv_sample_call(hist_p, a, w_dw_p, bdw, g, b, a_out, tp):
    bs = hist_p.shape[0]
    rows = bs * DEC_SEQ
    blk = pl.BlockSpec((rows, SEG), lambda i: (tp // rows, 0))
    return pl.pallas_call(
        _conv_sample_kernel, grid=(1,),
        in_specs=[_full(hist_p), blk, _full(w_dw_p), _full(bdw), _full(g), _full(b),
                  pl.BlockSpec(memory_space=pl.ANY)],
        out_specs=[blk, pl.BlockSpec((bs, HALO, SEG), lambda i: (0, 0, 0))],
        out_shape=[jax.ShapeDtypeStruct(a_out.shape, F32),
                   jax.ShapeDtypeStruct((bs, HALO, SEG), F32)],
        scratch_shapes=[pltpu.VMEM((bs, HALO + DEC_SEQ, SEG), F32)],
        input_output_aliases={6: 0},
        compiler_params=_params(), name="conv_sample")(hist_p, a, w_dw_p, bdw, g, b, a_out)


QB = 128
assert QB == BAND


SPAN = BAND * max(DILATIONS)
UNITS = SPAN // QB
UNIT_UNROLL = 8


def _attn_prompt_kernel(q_ref, kh_ref, kc_ref, vh_ref, vc_ref, b1_ref, b4_ref, b16_ref, o_ref,
                        kbuf, vbuf, oacc, lacc, *, n_prompt):
    is_prompt = pl.program_id(1) < n_prompt
    refs = (q_ref, kh_ref, kc_ref, vh_ref, vc_ref, (b1_ref, b4_ref, b16_ref), o_ref, kbuf, vbuf, oacc, lacc)
    pl.when(is_prompt)(functools.partial(_attn_prompt_span, refs))

    @pl.when(jnp.logical_not(is_prompt))
    def _():
        o_ref[...] = jnp.zeros_like(o_ref)


def _attn_prompt_span(refs):
    q_ref, kh_ref, kc_ref, vh_ref, vc_ref, bias_refs, o_ref, kbuf, vbuf, oacc, lacc = refs
    pair = pl.program_id(0)
    first_span = pl.program_id(1) == 0
    kbuf[0:SPAN, :] = kh_ref[0]
    kbuf[SPAN:, :] = kc_ref[0]
    vbuf[0:SPAN, :] = vh_ref[0]
    vbuf[SPAN:, :] = vc_ref[0]
    lane = lax.broadcasted_iota(jnp.int32, (QB, LANES), 1)
    lo_half = lane < HEAD_DIM

    for dil, bias_ref in zip(DILATIONS, bias_refs):
        def unit(u, carry, dil=dil, bias_ref=bias_ref):
            sub = u // dil
            t0 = sub * (QB * dil) + (u - sub * dil)
            if dil == 1:
                rows_q = pl.ds(pl.multiple_of(t0, QB), QB)
                rows_k = pl.ds(pl.multiple_of(SPAN + t0 - QB, QB), 2 * QB)
            else:
                rows_q = pl.ds(t0, QB, stride=dil)
                rows_k = pl.ds(SPAN + t0 - QB * dil, 2 * QB, stride=dil)
            table = jnp.logical_and(first_span, sub == 0).astype(jnp.int32) * N_HEADS_B + pair * 2
            qp = q_ref[0, rows_q, :] * (HEAD_DIM ** -0.5 * LOG2E)
            kk = kbuf[rows_k, :].astype(BF16)
            vv = vbuf[rows_k, :].astype(BF16)
            outs, lses = [], []
            for hh in range(2):
                keep = lo_half if hh == 0 else jnp.logical_not(lo_half)
                qm = jnp.where(keep, qp, 0.0).astype(BF16)
                s = lax.dot_general(qm, kk, (((1,), (1,)), ((), ())), preferred_element_type=F32)
                s = s + bias_ref[table + hh]
                m = jnp.max(s, axis=-1, keepdims=True)
                p = jnp.exp2(s - m)
                l = jnp.sum(p, axis=-1, keepdims=True)
                pv = jnp.dot(p.astype(BF16), vv, preferred_element_type=F32)
                outs.append(pv / l)
                lses.append(m + jnp.log2(l))
            o_new = jnp.where(lo_half, outs[0], outs[1])
            l_new = jnp.where(lo_half, lses[0], lses[1])
            if dil != 1:
                l_old = lacc[rows_q, :]
                mx = jnp.maximum(l_old, l_new)
                w_old = jnp.exp2(l_old - mx)
                w_new = jnp.exp2(l_new - mx)
                tot = w_old + w_new
                o_new = (w_old * oacc[rows_q, :] + w_new * o_new) / tot
                l_new = mx + jnp.log2(tot)
            oacc[rows_q, :] = o_new
            lacc[rows_q, :] = l_new
            return carry

        lax.fori_loop(0, UNITS, unit, 0, unroll=UNIT_UNROLL)
    o_ref[0] = oacc[...]


def _attn_prompt_call(q, k, v, biases, tp):
    n = q.shape[1]
    cur = pl.BlockSpec((1, SPAN, LANES), lambda p, m: (p, m, 0))
    halo = pl.BlockSpec((1, SPAN, LANES), lambda p, m: (p, jnp.maximum(m - 1, 0), 0))
    scratch = [pltpu.VMEM((2 * SPAN, LANES), F32), pltpu.VMEM((2 * SPAN, LANES), F32),
               pltpu.VMEM((SPAN, LANES), F32), pltpu.VMEM((SPAN, LANES), F32)]
    return pl.pallas_call(
        functools.partial(_attn_prompt_kernel, n_prompt=tp // SPAN),
        grid=(N_PAIRS, pl.cdiv(n, SPAN)),
        in_specs=[cur, halo, cur, halo, cur] + [_full(b) for b in biases],
        out_specs=cur, out_shape=jax.ShapeDtypeStruct((N_PAIRS, n, LANES), F32),
        scratch_shapes=scratch, compiler_params=_params(2), name="attn_prompt")(q, k, k, v, v, *biases)


def _alibi_slopes():
    return [2.0 ** (-8.0 * (h + 1) / N_HEADS_B) for h in range(N_HEADS_B)]


def _prompt_bias(dil):
    i = np.arange(QB)[:, None]
    j = np.arange(2 * QB)[None, :]
    rel = i + QB - j
    ok = np.logical_and(rel >= 0, rel <= BAND)
    slopes = np.asarray(_alibi_slopes(), np.float32)[:, None, None]
    pen = (-slopes * (rel * dil).astype(np.float32)[None]) * np.float32(LOG2E)
    full = np.where(ok[None], pen, np.float32(NEG)).astype(np.float32)
    start = np.where((j >= QB)[None], full, np.float32(NEG)).astype(np.float32)
    return jnp.asarray(np.concatenate([full, start], axis=0))


def _branch_multiplicity(dist):
    cnt = np.zeros(dist.shape, np.int32)
    for dil in DILATIONS:
        cnt = cnt + np.logical_and(dist % dil == 0, dist <= BAND * dil).astype(np.int32)
    return cnt


def _sample_bias(n_cache):
    t = np.arange(DEC_SEQ)[:, None]
    dist_c = n_cache + t - np.arange(n_cache)[None, :]
    dist_n = t - np.arange(LANES)[None, :]
    ok_n = np.logical_and(dist_n >= 0, np.arange(LANES)[None, :] < DEC_SEQ)
    slopes = np.asarray(_alibi_slopes(), np.float32)[:, None, None]

    def bias(dist, ok):
        mult = _branch_multiplicity(np.maximum(dist, 0))
        ok = np.logical_and(ok, mult > 0)
        val = -slopes * dist.astype(np.float32)[None] + np.log(np.maximum(mult, 1).astype(np.float32))[None]
        return np.where(ok[None], val, np.float32(NEG)).astype(np.float32)

    bc = bias(dist_c, np.ones(dist_c.shape, bool)).reshape(N_HEADS_B * DEC_SEQ, n_cache)
    bn = bias(dist_n, ok_n).reshape(N_HEADS_B * DEC_SEQ, LANES)
    return jnp.asarray(bc), jnp.asarray(bn)


def _attn_sample_kernel(q_ref, kn_ref, vn_ref, kc_ref, vc_ref, bc_ref, bn_ref, big_ref, o_ref):
    del big_ref
    rows = N_HEADS_B * DEC_SEQ
    wide = lambda ref: jnp.concatenate([ref[pair] for pair in range(N_PAIRS)], axis=1)
    q = wide(q_ref) * (HEAD_DIM ** -0.5)
    qrep = jnp.concatenate([q] * N_HEADS_B, axis=0)
    rh = lax.shift_right_logical(lax.broadcasted_iota(jnp.int32, (rows, SEG), 0), 3)
    ch = lax.shift_right_logical(lax.broadcasted_iota(jnp.int32, (rows, SEG), 1), 6)
    own = rh == ch
    qm = jnp.where(own, qrep, 0.0).astype(BF16)
    pad = jnp.zeros((LANES - DEC_SEQ, SEG), F32)
    kn = jnp.concatenate([wide(kn_ref), pad], axis=0).astype(BF16)
    vn = jnp.concatenate([wide(vn_ref), pad], axis=0).astype(BF16)
    nt = (((1,), (1,)), ((), ()))
    s_c = jnp.dot(qm, kc_ref[0, 0].astype(BF16), preferred_element_type=F32) + bc_ref[...]
    s_n = lax.dot_general(qm, kn, nt, preferred_element_type=F32) + bn_ref[...]
    m = jnp.maximum(jnp.max(s_c, axis=-1, keepdims=True), jnp.max(s_n, axis=-1, keepdims=True))
    p_c = jnp.exp(s_c - m)
    p_n = jnp.exp(s_n - m)
    l = jnp.sum(p_c, axis=-1, keepdims=True) + jnp.sum(p_n, axis=-1, keepdims=True)
    r = (lax.dot_general(p_c.astype(BF16), vc_ref[0, 0].astype(BF16), nt, preferred_element_type=F32)
         + jnp.dot(p_n.astype(BF16), vn, preferred_element_type=F32)) / l
    r = jnp.where(own, r, 0.0)
    o = r[0:DEC_SEQ]
    for h in range(1, N_HEADS_B):
        o = o + r[h * DEC_SEQ:(h + 1) * DEC_SEQ]
    for pair in range(N_PAIRS):
        o_ref[pair] = o[:, pair * LANES:(pair + 1) * LANES]


def _attn_sample_call(q, k, v, cache_kt, cache_vt, layer, bc, bn, o_big, tp):
    bs, n_cache = cache_kt.shape[1], cache_kt.shape[3]
    new = pl.BlockSpec((N_PAIRS, DEC_SEQ, LANES), lambda b: (0, tp // DEC_SEQ + b, 0))
    cache = pl.BlockSpec((1, 1, C_B, n_cache), lambda b: (layer, b, 0, 0))
    return pl.pallas_call(
        _attn_sample_kernel, grid=(bs,),
        in_specs=[new, new, new, cache, cache, _full(bc), _full(bn),
                  pl.BlockSpec(memory_space=pl.ANY)],
        out_specs=new, out_shape=jax.ShapeDtypeStruct(o_big.shape, F32),
        input_output_aliases={7: 0},
        compiler_params=_params(), name="attn_sample")(q, k, v, cache_kt, cache_vt, bc, bn, o_big)


def _mix_kernel(xp_ref, xs_ref, a_ref, o_ref, c_ref, wo_ref, bo_ref, g1_ref, b1_ref, wr_ref, br_ref,
                h_ref, xs3_ref, aux_ref, cnt_ref, *, n_prompt):
    mixed = jnp.concatenate([a_ref[...]] + [o_ref[pair] for pair in range(N_PAIRS)] + [c_ref[...]], axis=1)
    mix = jnp.dot(mixed.astype(BF16), wo_ref[...], preferred_element_type=F32) + bo_ref[...]
    h = _ln_full(DN_ALPHA * _tile_rows(xp_ref, xs_ref, n_prompt) + mix, g1_ref[...], b1_ref[...])
    h_ref[...] = h

    hb = h.astype(BF16)
    h_lo = (h - hb.astype(F32)).astype(BF16)
    both = jnp.dot(hb, wr_ref[...], preferred_element_type=F32)
    logits = (both[:, :LANES] + both[:, LANES:]
              + jnp.dot(h_lo, wr_ref[:, :LANES], preferred_element_type=F32) + br_ref[...])
    lane = lax.broadcasted_iota(jnp.int32, (TT, LANES), 1)
    lane_f = lane.astype(F32)
    vals, sels = [], []
    cur = logits
    for _ in range(TOP_K):
        m = jnp.max(cur, axis=-1, keepdims=True)
        idx = jnp.min(jnp.where(cur == m, lane_f, float(LANES)), axis=-1, keepdims=True)
        sel = lane_f == idx
        vals.append(m)
        sels.append(sel)
        cur = jnp.where(sel, -jnp.inf, cur)
    exps = [jnp.exp(v - vals[0]) for v in vals]
    den = exps[0] + exps[1] + exps[2] + exps[3]

    onehot = jnp.zeros((TT, LANES), F32)
    for sel in sels:
        onehot = onehot + sel.astype(F32)
    r = lax.broadcasted_iota(jnp.int32, (TT, TT), 0)
    c = lax.broadcasted_iota(jnp.int32, (TT, TT), 1)
    below = (c < r).astype(BF16)
    earlier = jnp.dot(below, onehot.astype(BF16), preferred_element_type=F32)
    cnt = jnp.broadcast_to(jnp.sum(onehot, axis=0, keepdims=True), (SUBLANES, LANES))
    er = lax.broadcasted_iota(jnp.int32, (LANES, LANES), 0)
    ec = lax.broadcasted_iota(jnp.int32, (LANES, LANES), 1)
    off = jnp.dot(cnt.astype(BF16), (er < ec).astype(BF16), preferred_element_type=F32)[0:1]
    place = earlier + off
    aux = jnp.zeros((TT, LANES), F32)
    rows = []
    for k in range(TOP_K):
        row = jnp.sum(jnp.where(sels[k], place, 0.0), axis=-1, keepdims=True)
        rows.append(row)
        aux = aux + jnp.where(lane == k, row, 0.0) + jnp.where(lane == TOP_K + k, exps[k] / den, 0.0)
    aux_ref[...] = aux
    cnt_ref[0] = cnt.astype(jnp.int32)

    aux_t = jnp.transpose(aux)
    dest = lax.broadcasted_iota(jnp.int32, (TOP_K * TT, TT), 0).astype(F32)
    disp = jnp.zeros((TOP_K * TT, TT), F32)
    for k in range(TOP_K):
        disp = jnp.where(dest == aux_t[k:k + 1, :], 1.0, disp)
    xs = jnp.dot(disp.astype(BF16), hb, preferred_element_type=F32)
    for s in range(ROW_TILES):
        xs3_ref[pl.ds(s, TOP_K * TT, stride=SUBLANES), :] = xs[:, s * LANES:(s + 1) * LANES]


def _mix_call(x, a_out, o_b, c_out, wo, bo, g1, b1, wr, br, tp):
    n = a_out.shape[0]
    row = lambda w: pl.BlockSpec((TT, w), lambda i: (i, 0))
    xs3 = pl.BlockSpec((TOP_K * TT * SUBLANES, LANES), lambda i: (i, 0))
    cnt = pl.BlockSpec((1, SUBLANES, LANES), lambda i: (i, 0, 0))
    return pl.pallas_call(
        functools.partial(_mix_kernel, n_prompt=tp // TT), grid=(n // TT,),
        in_specs=_tile_row_specs(tp // TT)
        + [row(SEG), pl.BlockSpec((N_PAIRS, TT, LANES), lambda i: (0, i, 0)), row(SEG),
           _full(wo), _full(bo), _full(g1), _full(b1), _full(wr), _full(br)],
        out_specs=[row(D_MODEL), xs3, row(LANES), cnt],
        out_shape=[jax.ShapeDtypeStruct((n, D_MODEL), F32),
                   jax.ShapeDtypeStruct((n * TOP_K * SUBLANES, LANES), F32),
                   jax.ShapeDtypeStruct((n, LANES), F32),
                   jax.ShapeDtypeStruct((n // TT, SUBLANES, LANES), jnp.int32)],
        compiler_params=_params(), name="mix_ln_router")(
            *x, a_out, o_b, c_out, wo, bo, g1, b1, wr, br)


def _rows(start_row, n_rows):
    return pl.ds(pl.multiple_of(start_row * SUBLANES, SUBLANES), n_rows * SUBLANES)


def _expert_gather(tabs, blk, slot, xs3_hbm, buf, sem):
    blk_e, blk_s0, j_lo, j_hi, cnt, off, cum, tot = tabs[:8]
    e = blk_e[blk]
    s0 = blk_s0[blk]
    base = slot * TM

    @pl.when(tot[e] - s0 < TM)
    def _():
        buf[_rows(base, TM), :] = jnp.zeros((TM * SUBLANES, LANES), F32)

    def body(j, carry):
        run = cum[j * N_EXPERTS + e]
        lo = jnp.maximum(run, s0)
        hi = jnp.minimum(run + cnt[j * N_EXPERTS + e], s0 + TM)

        @pl.when(hi > lo)
        def _():
            src = j * (TOP_K * TT) + off[j * N_EXPERTS + e] + (lo - run)
            pltpu.make_async_copy(xs3_hbm.at[_rows(src, hi - lo)],
                                  buf.at[_rows(base + lo - s0, hi - lo)], sem.at[slot]).start()
        return carry

    lax.fori_loop(j_lo[blk], j_hi[blk], body, 0)


def _expert_gather_wait(tabs, blk, slot, xs3_hbm, buf, sem):
    blk_e, blk_s0, tot = tabs[0], tabs[1], tabs[7]
    valid = jnp.minimum(tot[blk_e[blk]] - blk_s0[blk], TM)
    pltpu.make_async_copy(xs3_hbm.at[_rows(0, valid)], buf.at[_rows(slot * TM, valid)],
                          sem.at[slot]).wait()


def _expert_weight_copies(w1_hbm, w2_hbm, expert, w1f, w2f, wslot, wsem):
    return (pltpu.make_async_copy(w1_hbm.at[expert], w1f.at[wslot], wsem.at[0, wslot]),
            pltpu.make_async_copy(w2_hbm.at[expert], w2f.at[wslot], wsem.at[1, wslot]))


def _expert_kernel(*refs, first_expert):
    tabs = refs[:10]
    blk_e, group, next_e = tabs[0], tabs[8], tabs[9]
    (n_used_ref, xs3_hbm, w1_hbm, b1_ref, w2_hbm, b2_ref, y3_ref,
     buf, xb, w1f, w2f, w1b, w2b, sem, wsem) = refs[10:]
    b = pl.program_id(0)
    n_used = n_used_ref[0]
    slot = lax.rem(b, 2)
    used = b < n_used
    wslot = lax.rem(group[b], 2)

    @pl.when(b == 0)
    def _():
        for copy in _expert_weight_copies(w1_hbm, w2_hbm, first_expert + blk_e[0], w1f, w2f, 0, wsem):
            copy.start()

    @pl.when(b == 0)
    def _():
        _expert_gather(tabs, 0, 0, xs3_hbm, buf, sem)

    @pl.when(used)
    def _():
        _expert_gather_wait(tabs, b, slot, xs3_hbm, buf, sem)

    @pl.when(b + 1 < n_used)
    def _():
        _expert_gather(tabs, b + 1, 1 - slot, xs3_hbm, buf, sem)

    new_expert = jnp.logical_or(b == 0, blk_e[b] != blk_e[jnp.maximum(b - 1, 0)])

    @pl.when(jnp.logical_and(used, new_expert))
    def _():
        for copy in _expert_weight_copies(w1_hbm, w2_hbm, first_expert + blk_e[b], w1f, w2f, wslot, wsem):
            copy.wait()

        @pl.when(next_e[b] >= 0)
        def _():
            nxt = first_expert + next_e[b]
            for copy in _expert_weight_copies(w1_hbm, w2_hbm, nxt, w1f, w2f, 1 - wslot, wsem):
                copy.start()

        w1b[...] = w1f[wslot].astype(BF16)
        w2b[...] = w2f[wslot].astype(BF16)

    @pl.when(used)
    def _():
        base = slot * (TM * SUBLANES)
        for s in range(ROW_TILES):
            xb[:, s * LANES:(s + 1) * LANES] = buf[pl.ds(base + s, TM, stride=SUBLANES), :].astype(BF16)
        x = xb[...]
        y = jnp.broadcast_to(b2_ref[0], (TM, D_MODEL))
        for c in range(0, D_FF, FF_CHUNK):
            cols = slice(c, c + FF_CHUNK)
            ucols = slice(D_FF + c, D_FF + c + FF_CHUNK)
            g = jnp.dot(x, w1b[:, cols], preferred_element_type=F32) + b1_ref[0, :, cols]
            u = jnp.dot(x, w1b[:, ucols], preferred_element_type=F32) + b1_ref[0, :, ucols]
            gate = jnp.minimum(g, SWIGLU_LIMIT)
            up = jnp.clip(u, -SWIGLU_LIMIT, SWIGLU_LIMIT)
            hh = (up + 1.0) * (gate * jax.nn.sigmoid(SWIGLU_ALPHA * gate))
            y = y + jnp.dot(hh.astype(BF16), w2b[cols, :], preferred_element_type=F32)
        for s in range(ROW_TILES):
            y3_ref[pl.ds(s, TM, stride=SUBLANES), :] = y[:, s * LANES:(s + 1) * LANES]

    @pl.when(jnp.logical_not(used))
    def _():
        y3_ref[...] = jnp.zeros_like(y3_ref)


def _expert_call(tabs, n_used, xs3, w1, b1, w2, b2, layer):
    n_blocks = tabs[0].shape[0]
    first_expert = layer * N_EXPERTS
    by_expert = lambda shape: pl.BlockSpec(shape, lambda b, e, *_: (first_expert + e[b], 0, 0))
    anywhere = pl.BlockSpec(memory_space=pl.ANY)
    grid_spec = pltpu.PrefetchScalarGridSpec(
        num_scalar_prefetch=len(tabs) + 1, grid=(n_blocks,),
        in_specs=[anywhere, anywhere, by_expert((1, 1, 2 * D_FF)), anywhere, by_expert((1, 1, D_MODEL))],
        out_specs=pl.BlockSpec((TM * SUBLANES, LANES), lambda b, *_: (b, 0)),
        scratch_shapes=[pltpu.VMEM((2 * TM * SUBLANES, LANES), F32),
                        pltpu.VMEM((TM, D_MODEL), BF16),
                        pltpu.VMEM((2, D_MODEL, 2 * D_FF), F32),
                        pltpu.VMEM((2, D_FF, D_MODEL), F32),
                        pltpu.VMEM((D_MODEL, 2 * D_FF), BF16),
                        pltpu.VMEM((D_FF, D_MODEL), BF16),
                        pltpu.SemaphoreType.DMA((2,)),
                        pltpu.SemaphoreType.DMA((2, 2))])
    return pl.pallas_call(
        functools.partial(_expert_kernel, first_expert=first_expert), grid_spec=grid_spec,
        out_shape=jax.ShapeDtypeStruct((n_blocks * TM * SUBLANES, LANES), F32),
        compiler_params=_params(vmem_limit=EXPERT_VMEM_LIMIT), name="expert_ffn")(
            *tabs, n_used, xs3, w1, b1, w2, b2)


PAIRS = TOP_K * TT


def _combine_gather(cnt, off, src, tile, slot, y3_hbm, buf, sem):
    def body(e, carry):
        n = cnt[tile * N_EXPERTS + e]

        @pl.when(n > 0)
        def _():
            pltpu.make_async_copy(y3_hbm.at[_rows(src[tile * N_EXPERTS + e], n)],
                                  buf.at[_rows(slot * PAIRS + off[tile * N_EXPERTS + e], n)],
                                  sem.at[slot]).start()
        return carry

    lax.fori_loop(0, N_EXPERTS, body, 0, unroll=4)


def _combine_kernel(cnt, off, src, y3_hbm, h_ref, aux_ref, g2_ref, b2_ref, op_ref, os_ref, buf, ys, sem,
                    *, n_prompt):
    i = pl.program_id(0)
    n_steps = pl.num_programs(0)
    slot = lax.rem(i, 2)

    @pl.when(i == 0)
    def _():
        _combine_gather(cnt, off, src, 0, 0, y3_hbm, buf, sem)

    pltpu.make_async_copy(y3_hbm.at[_rows(0, PAIRS)], buf.at[_rows(slot * PAIRS, PAIRS)],
                          sem.at[slot]).wait()

    @pl.when(i + 1 < n_steps)
    def _():
        _combine_gather(cnt, off, src, i + 1, 1 - slot, y3_hbm, buf, sem)

    base = slot * (PAIRS * SUBLANES)
    for s in range(ROW_TILES):
        ys[:, s * LANES:(s + 1) * LANES] = buf[pl.ds(base + s, PAIRS, stride=SUBLANES), :].astype(BF16)
    aux = aux_ref[...]
    dest = lax.broadcasted_iota(jnp.int32, (TT, PAIRS), 1).astype(F32)
    weights = jnp.zeros((TT, PAIRS), F32)
    for k in range(TOP_K):
        weights = jnp.where(dest == aux[:, k:k + 1], aux[:, TOP_K + k:TOP_K + k + 1], weights)
    moe = jnp.dot(weights.astype(BF16), ys[...], preferred_element_type=F32)
    y = _ln_full(DN_ALPHA * h_ref[...] + moe, g2_ref[...], b2_ref[...])

    @pl.when(i < n_prompt)
    def _():
        op_ref[...] = y

    @pl.when(i >= n_prompt)
    def _():
        os_ref[...] = y


def _combine_call(cnt, off, src, y3, h, aux, g2, b2, tp):
    n = h.shape[0]
    row = lambda w: pl.BlockSpec((TT, w), lambda i, *_: (i, 0))
    n_prompt = tp // TT
    out_specs = _tile_row_specs(n_prompt)
    out_shape = [jax.ShapeDtypeStruct((tp, D_MODEL), F32), jax.ShapeDtypeStruct((n - tp, D_MODEL), F32)]
    grid_spec = pltpu.PrefetchScalarGridSpec(
        num_scalar_prefetch=3, grid=(n // TT,),
        in_specs=[pl.BlockSpec(memory_space=pl.ANY), row(D_MODEL), row(LANES),
                  pl.BlockSpec(g2.shape, lambda i, *_: (0, 0)), pl.BlockSpec(b2.shape, lambda i, *_: (0, 0))],
        out_specs=out_specs,
        scratch_shapes=[pltpu.VMEM((2 * PAIRS * SUBLANES, LANES), F32),
                        pltpu.VMEM((PAIRS, D_MODEL), BF16),
                        pltpu.SemaphoreType.DMA((2,))])
    return pl.pallas_call(
        functools.partial(_combine_kernel, n_prompt=n_prompt), grid_spec=grid_spec, out_shape=out_shape,
        compiler_params=_params(), name="combine_ln")(cnt, off, src, y3, h, aux, g2, b2)


def _routing_tables(cnt_tiles, n):
    cnt = cnt_tiles[:, 0, :N_EXPERTS]
    off = jnp.cumsum(cnt, axis=1) - cnt
    cum = jnp.cumsum(cnt, axis=0) - cnt
    tot = jnp.sum(cnt, axis=0)
    padded = (tot + TM - 1) // TM * TM
    pends = jnp.cumsum(padded)
    pstart = pends - padded
    n_blocks = n * TOP_K // TM + N_EXPERTS
    blk_start = jnp.arange(n_blocks, dtype=jnp.int32) * TM
    blk_e = jnp.minimum(jnp.sum((pends[None, :] <= blk_start[:, None]).astype(jnp.int32), axis=1),
                        N_EXPERTS - 1)
    ids = jnp.arange(N_EXPERTS, dtype=jnp.int32)
    mine = blk_e[:, None] == ids[None, :]
    pick = lambda per_expert: jnp.sum(jnp.where(mine, per_expert[None, :], 0), axis=1)
    blk_s0 = blk_start - pick(pstart)
    lo3, hi3, mine3 = blk_s0[:, None, None], blk_s0[:, None, None] + TM, mine[:, None, :]
    j_lo = jnp.sum(jnp.logical_and(mine3, (cum + cnt)[None] <= lo3).astype(jnp.int32), axis=(1, 2))
    j_hi = jnp.sum(jnp.logical_and(mine3, cum[None] < hi3).astype(jnp.int32), axis=(1, 2))
    n_used = (pends[-1:] // TM).astype(jnp.int32)
    has = tot > 0
    group_of = jnp.cumsum(has.astype(jnp.int32)) - 1
    later = jnp.where(jnp.logical_and(has[None, :], ids[None, :] > ids[:, None]), ids[None, :], N_EXPERTS)
    next_of = jnp.min(later, axis=1)
    next_of = jnp.where(next_of == N_EXPERTS, -1, next_of)
    flat = lambda a: a.reshape(-1).astype(jnp.int32)
    expert_tabs = (flat(blk_e), flat(blk_s0), flat(j_lo), flat(j_hi), flat(cnt), flat(off), flat(cum), flat(tot),
                   flat(pick(group_of)), flat(pick(next_of)))
    combine_tabs = (flat(cnt), flat(off), flat(pstart[None, :] + cum))
    return expert_tabs, n_used, combine_tabs


def _pad_cols(a, width):
    return jnp.pad(a, [(0, 0)] * (a.ndim - 1) + [(0, width - a.shape[-1])])


def _layer_params(l, w_in, b_in, w_dw, b_dw, ln_a_g, ln_a_b, ln_c_g, ln_c_b, w_sp, b_sp,
                  w_out, b_out, ln1_g, ln1_b, w_router, b_router, ln2_g, ln2_b):
    bounds = [0, C_B, 2 * C_B, 3 * C_B, 3 * C_B + C_A, 3 * C_B + 2 * C_A, 3 * C_B + 2 * C_A + C_C,
              3 * C_B + 2 * C_A + 2 * C_C]
    w_segs = [_pad_cols(w_in[l][:, bounds[i]:bounds[i + 1]], SEG) for i in range(N_SEG)]
    b_segs = [_pad_cols(b_in[l][bounds[i]:bounds[i + 1]], SEG) for i in range(N_SEG)]
    row = lambda a: _pad_cols(a, SEG)[None, :]
    wo = w_out[l]
    pad_rows = lambda a: jnp.pad(a, ((0, SEG - a.shape[0]), (0, 0)))
    bsp = b_sp[l]
    bfull = _pad_cols(jnp.repeat(bsp.T, HEAD_DIM, axis=1), SEG)
    wr = _pad_cols(w_router[l], LANES)
    wr_hi = wr.astype(BF16)
    return dict(
        w_in=jnp.concatenate(w_segs, axis=1).astype(BF16),
        b_in=jnp.concatenate(b_segs)[None, :],
        w_dw=jnp.pad(w_dw[l], ((0, HALO - CONV_W), (0, SEG - C_A))),
        b_dw=row(b_dw[l]), ln_a_g=row(ln_a_g[l]), ln_a_b=row(ln_a_b[l]),
        ln_c_g=row(ln_c_g[l]), ln_c_b=row(ln_c_b[l]),
        w_sp=w_sp[l], b_full=bfull,
        w_out=jnp.concatenate([pad_rows(wo[:C_A]), wo[C_A:C_A + C_B], pad_rows(wo[C_A + C_B:])]).astype(BF16),
        b_out=b_out[l][None, :],
        ln1_g=ln1_g[l][None, :], ln1_b=ln1_b[l][None, :],
        w_router=jnp.concatenate([wr_hi, (wr - wr_hi.astype(F32)).astype(BF16)], axis=1),
        b_router=jnp.concatenate([b_router[l], jnp.full((LANES - N_EXPERTS,), NEG, F32)])[None, :],
        ln2_g=ln2_g[l][None, :], ln2_b=ln2_b[l][None, :])


def _layer(x, p, hist_p, caches, experts, biases, n, tp, layer):
    conv_args = (p["w_dw"], p["b_dw"], p["ln_a_g"], p["ln_a_b"])
    q, k, v, a, a_out, c_out, vn_s = _proj_call(x, p["w_in"], p["b_in"], p["ln_c_g"], p["ln_c_b"],
                                                p["w_sp"], p["b_full"], *conv_args, n, tp)
    a_out, new_a_s = _conv_sample_call(hist_p, a, *conv_args, a_out, tp)

    o_b = _attn_prompt_call(q, k, v, biases["prompt"], tp)
    o_b = _attn_sample_call(q, k, v, *caches, layer, *biases["sample"], o_b, tp)

    h, xs3, aux, cnt_tiles = _mix_call(x, a_out, o_b, c_out, p["w_out"], p["b_out"],
                                       p["ln1_g"], p["ln1_b"], p["w_router"], p["b_router"], tp)
    expert_tabs, n_used, combine_tabs = _routing_tables(cnt_tiles, n)
    y3 = _expert_call(expert_tabs, n_used, xs3, *experts, layer)
    y = _combine_call(*combine_tabs, y3, h, aux, p["ln2_g"], p["ln2_b"], tp)
    return y, a, k, v, vn_s, new_a_s


def kernel(x_prompt, x_sample, state_a_conv, cache_b_k, cache_b_v, w_in, b_in, w_dw, b_dw, ln_a_g, ln_a_b, ln_c_g, ln_c_b, w_sp, b_sp, w_out, b_out, ln1_g, ln1_b, w_router, b_router, w1, b1, w2, b2, ln2_g, ln2_b):
    bp, tp, _ = x_prompt.shape
    bs, ts, _ = x_sample.shape
    n_cache = cache_b_k.shape[2]
    assert bp == 1 and ts == DEC_SEQ and bs * ts == TT and tp % WIN_MAX == 0
    assert n_cache == WIN_MAX
    n = tp + bs * ts
    keep = min(WIN_MAX, tp)
    hist = CONV_W - 1

    weights = (w_in, b_in, w_dw, b_dw, ln_a_g, ln_a_b, ln_c_g, ln_c_b, w_sp, b_sp, w_out, b_out,
               ln1_g, ln1_b, w_router, b_router, ln2_g, ln2_b)
    biases = {"prompt": [_prompt_bias(d) for d in DILATIONS], "sample": _sample_bias(n_cache)}
    experts = (w1.reshape(DEPTH * N_EXPERTS, D_MODEL, 2 * D_FF), b1.reshape(DEPTH * N_EXPERTS, 1, 2 * D_FF),
               w2.reshape(DEPTH * N_EXPERTS, D_FF, D_MODEL), b2.reshape(DEPTH * N_EXPERTS, 1, D_MODEL))
    to_feature_major = lambda c: jnp.transpose(c, (0, 1, 3, 4, 2)).reshape(DEPTH, bs, C_B, n_cache)
    caches = (to_feature_major(cache_b_k), to_feature_major(cache_b_v))
    heads = lambda a: jnp.transpose(a, (1, 0, 2)).reshape(a.shape[1], N_HEADS_B, HEAD_DIM)

    x = (x_prompt[0], x_sample.reshape(bs * ts, D_MODEL))
    outs = {name: [] for name in ("a_p", "a_s", "k_p", "v_p", "k_s", "v_s", "c_s")}
    for l in range(DEPTH):
        p = _layer_params(l, *weights)
        hist_p = jnp.pad(state_a_conv[l], ((0, 0), (HALO - hist, 0), (0, SEG - C_A)))
        x, a, k, v, vn_s, new_a_s = _layer(x, p, hist_p, caches, experts, biases, n, tp, l)
        outs["a_p"].append(a[tp - hist:tp, :C_A][None])
        outs["a_s"].append(new_a_s[:, HALO - hist:, :C_A])
        outs["k_p"].append(heads(k[:, tp - keep:tp])[None])
        outs["v_p"].append(heads(v[:, tp - keep:tp])[None])
        outs["k_s"].append(heads(k[:, tp:]).reshape(bs, ts, N_HEADS_B, HEAD_DIM))
        outs["v_s"].append(heads(v[:, tp:]).reshape(bs, ts, N_HEADS_B, HEAD_DIM))
        outs["c_s"].append(vn_s[:, :C_C].reshape(bs, ts, C_C))
    stack = lambda name: jnp.stack(outs[name])
    y_prompt, y_sample = x
    return (y_prompt[None], y_sample.reshape(bs, ts, D_MODEL), stack("a_p"), stack("a_s"),
            stack("k_p"), stack("v_p"), stack("k_s"), stack("v_s"), stack("c_s"))
```

```python
import functools
import math

import jax
import jax.numpy as jnp
import numpy as np
from jax import lax
from jax.experimental import pallas as pl
from jax.experimental.pallas import tpu as pltpu

F32 = jnp.float32
BF16 = jnp.bfloat16

D_MODEL = 1024
HEAD_DIM = 64
N_HEADS_B = 6
C_B = N_HEADS_B * HEAD_DIM
C_A = 320
C_C = 320
N_GROUPS_C = 5
CONV_W = 31
DILATIONS = (1, 4, 16)
BAND = 128
WIN_MAX = 2048
CHUNK = 128
N_EXPERTS = 32
TOP_K = 4
D_FF = 1024
SWIGLU_LIMIT = 7.0
SWIGLU_ALPHA = 1.702
LN_EPS = 1e-5
DEPTH = 2
DN_ALPHA = (2.0 * DEPTH) ** 0.25
DEC_SEQ = 8

LANES = 128
SUBLANES = 8
SEG = 384
N_SEG = 7
N_PAIRS = C_B // LANES
ROW_TILES = D_MODEL // LANES
assert ROW_TILES == SUBLANES

TT = 256
TM = 512
FF_CHUNK = 1024
NEG = -1e30
LOG2E = math.log2(math.e)
MIB = 1024 * 1024
VMEM_LIMIT = 48 * MIB
EXPERT_VMEM_LIMIT = 56 * MIB


def _params(n_axes=1, vmem_limit=VMEM_LIMIT):
    return pltpu.CompilerParams(dimension_semantics=("arbitrary",) * n_axes,
                                vmem_limit_bytes=vmem_limit)


def _full(a):
    nd = a.ndim
    return pl.BlockSpec(a.shape, lambda *_: (0,) * nd)


def _ln_valid(x, g, b, n_valid):
    col = lax.broadcasted_iota(jnp.int32, x.shape, 1)
    ok = col < n_valid
    mu = jnp.sum(jnp.where(ok, x, 0.0), axis=-1, keepdims=True) / n_valid
    xc = jnp.where(ok, x - mu, 0.0)
    var = jnp.sum(xc * xc, axis=-1, keepdims=True) / n_valid
    return xc * lax.rsqrt(var + LN_EPS) * g + b


def _ln_full(x, g, b):
    mu = jnp.mean(x, axis=-1, keepdims=True)
    xc = x - mu
    var = jnp.mean(xc * xc, axis=-1, keepdims=True)
    return xc * lax.rsqrt(var + LN_EPS) * g + b


def _spatial_gate(vn, cu, w_ref, bfull, rows, sub):
    r = lax.broadcasted_iota(jnp.int32, (rows, rows), 0)
    c = lax.broadcasted_iota(jnp.int32, (rows, rows), 1)
    causal = c <= r
    if sub is not None:
        causal = jnp.logical_and(causal, lax.shift_right_logical(r, sub) == lax.shift_right_logical(c, sub))
        pos = lax.broadcasted_iota(jnp.int32, (rows, CHUNK), 0) & (2 ** sub - 1)
        spread = (pos == lax.broadcasted_iota(jnp.int32, (rows, CHUNK), 1)).astype(BF16)
        bfull = jnp.concatenate([bfull[0:2 ** sub]] * (rows // 2 ** sub), axis=0)
    lo_half = lax.broadcasted_iota(jnp.int32, (rows, LANES), 1) < HEAD_DIM
    cols = []
    for pair in range(SEG // LANES):
        cs = slice(pair * LANES, (pair + 1) * LANES)
        acc = jnp.zeros((rows, LANES), F32)
        for hh in range(2):
            g = pair * 2 + hh
            if g >= N_GROUPS_C:
                continue
            keep = lo_half if hh == 0 else jnp.logical_not(lo_half)
            w = w_ref[g].astype(BF16)
            if sub is not None:
                w = jnp.dot(spread, w, preferred_element_type=F32).astype(BF16)
                w = lax.dot_general(w, spread, (((1,), (1,)), ((), ())), preferred_element_type=F32).astype(BF16)
            wm = jnp.where(causal, w, 0.0)
            vm = jnp.where(keep, vn[:, cs], 0.0).astype(BF16)
            acc = acc + jnp.dot(wm, vm, preferred_element_type=F32)
        cols.append(cu[:, cs] * (acc + bfull[:, cs]))
    return jnp.concatenate(cols, axis=1)


def _tile_rows(xp_ref, xs_ref, n_prompt):
    return jnp.where(pl.program_id(0) < n_prompt, xp_ref[...], xs_ref[...])


def _tile_row_specs(n_prompt):
    return [pl.BlockSpec((TT, D_MODEL), lambda i, *_: (jnp.minimum(i, n_prompt - 1), 0)),
            pl.BlockSpec((TT, D_MODEL), lambda i, *_: (jnp.maximum(i - n_prompt, 0), 0))]


def _proj_kernel(xp_ref, xs_ref, w_ref, b_ref, lcg_ref, lcb_ref, wsp_ref, bsp_ref,
                 wdw_ref, bdw_ref, lag_ref, lab_ref,
                 q_ref, k_ref, v_ref, a_ref, ao_ref, c_ref, vns_ref, abuf, shifted, *, n_prompt):
    x = _tile_rows(xp_ref, xs_ref, n_prompt)
    proj = jnp.dot(x.astype(BF16), w_ref[...], preferred_element_type=F32) + b_ref[...]
    seg = lambda i: proj[:, i * SEG:(i + 1) * SEG]

    for i, ref in enumerate((q_ref, k_ref, v_ref)):
        qkv = seg(i)
        for pair in range(N_PAIRS):
            ref[pair] = qkv[:, pair * LANES:(pair + 1) * LANES]
    a = seg(3) * jax.nn.sigmoid(seg(4))
    a_ref[...] = a
    cu = seg(5)
    vn = _ln_valid(seg(6), lcg_ref[...], lcb_ref[...], C_C)
    i = pl.program_id(0)
    is_prompt = i < n_prompt

    @pl.when(i == 0)
    def _():
        abuf[0:HALO, :] = jnp.zeros((HALO, SEG), F32)

    @pl.when(jnp.logical_and(i > 0, is_prompt))
    def _():
        abuf[0:HALO, :] = abuf[TT:TT + HALO, :]

    @pl.when(is_prompt)
    def _():
        abuf[HALO:, :] = a
        _conv_tile(abuf, shifted, wdw_ref, bdw_ref, lag_ref, lab_ref, ao_ref)
        for ch in range(0, TT, CHUNK):
            rows = slice(ch, ch + CHUNK)
            c_ref[rows, :] = _spatial_gate(vn[rows], cu[rows], wsp_ref, bsp_ref[...], CHUNK, None)

    @pl.when(jnp.logical_not(is_prompt))
    def _():
        ao_ref[...] = jnp.zeros_like(ao_ref)
        c_ref[...] = _spatial_gate(vn, cu, wsp_ref, bsp_ref[...], TT, DEC_SEQ.bit_length() - 1)
        vns_ref[...] = vn


def _proj_call(x, w_in_p, b_in_p, lcg, lcb, w_sp, b_full, w_dw_p, bdw, lag, lab, n, tp):
    row = lambda w: pl.BlockSpec((TT, w), lambda i: (i, 0))
    heads = pl.BlockSpec((N_PAIRS, TT, LANES), lambda i: (0, i, 0))
    out = jax.ShapeDtypeStruct((n, SEG), F32)
    out3 = jax.ShapeDtypeStruct((N_PAIRS, n, LANES), F32)
    consts = (w_in_p, b_in_p, lcg, lcb, w_sp, b_full, w_dw_p, bdw, lag, lab)
    return pl.pallas_call(
        functools.partial(_proj_kernel, n_prompt=tp // TT), grid=(n // TT,),
        in_specs=_tile_row_specs(tp // TT) + [_full(c) for c in consts],
        out_specs=[heads] * 3 + [row(SEG)] * 3 + [pl.BlockSpec((TT, SEG), lambda i: (0, 0))],
        out_shape=[out3] * 3 + [out] * 3 + [jax.ShapeDtypeStruct((n - tp, SEG), F32)],
        scratch_shapes=[pltpu.VMEM((HALO + TT, SEG), F32),
                        pltpu.VMEM((SUBLANES - 1, HALO + TT, SEG), F32)],
        compiler_params=_params(), name="in_proj")(*x, *consts)


HALO = 32
CONV_CHUNK = 32


def _conv_tail(acc, bdw_ref, g_ref, b_ref):
    return jax.nn.silu(_ln_valid(acc + bdw_ref[...], g_ref[...], b_ref[...], C_A))


def _conv_tile(buf, shifted, w_ref, bdw_ref, g_ref, b_ref, o_ref):
    n_rows = HALO + TT - SUBLANES
    for sh in range(1, SUBLANES):
        shifted[sh - 1, 0:n_rows, :] = buf[sh:sh + n_rows, :]
    base = HALO - (CONV_W - 1)
    for r in range(0, TT, CONV_CHUNK):
        acc = jnp.zeros((CONV_CHUNK, SEG), F32)
        for j in range(CONV_W):
            tiles, sh = divmod(base + j, SUBLANES)
            src = buf if sh == 0 else shifted.at[sh - 1]
            lo = r + tiles * SUBLANES
            acc = acc + w_ref[j:j + 1, :] * src[lo:lo + CONV_CHUNK, :]
        o_ref[r:r + CONV_CHUNK, :] = _conv_tail(acc, bdw_ref, g_ref, b_ref)


SB = 8


def _conv_sample_kernel(hist_ref, a_ref, w_ref, bdw_ref, g_ref, b_ref, big_ref,
                        o_ref, newa_ref, buf):
    del big_ref
    bs = hist_ref.shape[0]
    buf[:, 0:HALO, :] = hist_ref[...]
    buf[:, HALO:, :] = a_ref[...].reshape(bs, DEC_SEQ, SEG)
    base = HALO - (CONV_W - 1)
    for s in range(0, bs, SB):
        acc = jnp.zeros((SB, DEC_SEQ, SEG), F32)
        for j in range(CONV_W):
            acc = acc + w_ref[j:j + 1, :][None] * buf[s:s + SB, base + j: base + j + DEC_SEQ, :]
        y = _conv_tail(acc.reshape(SB * DEC_SEQ, SEG), bdw_ref, g_ref, b_ref)
        o_ref[s * DEC_SEQ:(s + SB) * DEC_SEQ, :] = y
    newa_ref[...] = buf[:, DEC_SEQ:, :]


def _conv_sample_call(hist_p, a, w_dw_p, bdw, g, b, a_out, tp):
    bs = hist_p.shape[0]
    rows = bs * DEC_SEQ
    blk = pl.BlockSpec((rows, SEG), lambda i: (tp // rows, 0))
    return pl.pallas_call(
        _conv_sample_kernel, grid=(1,),
        in_specs=[_full(hist_p), blk, _full(w_dw_p), _full(bdw), _full(g), _full(b),
                  pl.BlockSpec(memory_space=pl.ANY)],
        out_specs=[blk, pl.BlockSpec((bs, HALO, SEG), lambda i: (0, 0, 0))],
        out_shape=[jax.ShapeDtypeStruct(a_out.shape, F32),
                   jax.ShapeDtypeStruct((bs, HALO, SEG), F32)],
        scratch_shapes=[pltpu.VMEM((bs, HALO + DEC_SEQ, SEG), F32)],
        input_output_aliases={6: 0},
        compiler_params=_params(), name="conv_sample")(hist_p, a, w_dw_p, bdw, g, b, a_out)


QB = 128
assert QB == BAND


SPAN = BAND * max(DILATIONS)
UNITS = SPAN // QB
UNIT_UNROLL = 16


def _attn_prompt_kernel(q_ref, kh_ref, kc_ref, vh_ref, vc_ref, b1_ref, b4_ref, b16_ref, o_ref,
                        kbuf, vbuf, oacc, lacc, *, n_prompt):
    is_prompt = pl.program_id(1) < n_prompt
    refs = (q_ref, kh_ref, kc_ref, vh_ref, vc_ref, (b1_ref, b4_ref, b16_ref), o_ref, kbuf, vbuf, oacc, lacc)
    pl.when(is_prompt)(functools.partial(_attn_prompt_span, refs))

    @pl.when(jnp.logical_not(is_prompt))
    def _():
        o_ref[...] = jnp.zeros_like(o_ref)


def _attn_prompt_span(refs):
    q_ref, kh_ref, kc_ref, vh_ref, vc_ref, bias_refs, o_ref, kbuf, vbuf, oacc, lacc = refs
    pair = pl.program_id(0)
    first_span = pl.program_id(1) == 0
    kbuf[0:SPAN, :] = kh_ref[0]
    kbuf[SPAN:, :] = kc_ref[0]
    vbuf[0:SPAN, :] = vh_ref[0]
    vbuf[SPAN:, :] = vc_ref[0]
    lane = lax.broadcasted_iota(jnp.int32, (QB, LANES), 1)
    lo_half = lane < HEAD_DIM

    for dil, bias_ref in zip(DILATIONS, bias_refs):
        def unit(u, carry, dil=dil, bias_ref=bias_ref):
            sub = u // dil
            t0 = sub * (QB * dil) + (u - sub * dil)
            if dil == 1:
                rows_q = pl.ds(pl.multiple_of(t0, QB), QB)
                rows_k = pl.ds(pl.multiple_of(SPAN + t0 - QB, QB), 2 * QB)
            else:
                rows_q = pl.ds(t0, QB, stride=dil)
                rows_k = pl.ds(SPAN + t0 - QB * dil, 2 * QB, stride=dil)
            table = jnp.logical_and(first_span, sub == 0).astype(jnp.int32) * N_HEADS_B + pair * 2
            qp = q_ref[0, rows_q, :] * (HEAD_DIM ** -0.5 * LOG2E)
            kk = kbuf[rows_k, :].astype(BF16)
            vv = vbuf[rows_k, :].astype(BF16)
            outs, lses = [], []
            for hh in range(2):
                keep = lo_half if hh == 0 else jnp.logical_not(lo_half)
                qm = jnp.where(keep, qp, 0.0).astype(BF16)
                s = lax.dot_general(qm, kk, (((1,), (1,)), ((), ())), preferred_element_type=F32)
                s = s + bias_ref[table + hh]
                m = jnp.max(s, axis=-1, keepdims=True)
                p = jnp.exp2(s - m)
                l = jnp.sum(p, axis=-1, keepdims=True)
                pv = jnp.dot(p.astype(BF16), vv, preferred_element_type=F32)
                outs.append(pv / l)
                lses.append(m + jnp.log2(l))
            o_new = jnp.where(lo_half, outs[0], outs[1])
            l_new = jnp.where(lo_half, lses[0], lses[1])
            if dil != 1:
                l_old = lacc[rows_q, :]
                mx = jnp.maximum(l_old, l_new)
                w_old = jnp.exp2(l_old - mx)
                w_new = jnp.exp2(l_new - mx)
                tot = w_old + w_new
                o_new = (w_old * oacc[rows_q, :] + w_new * o_new) / tot
                l_new = mx + jnp.log2(tot)
            oacc[rows_q, :] = o_new
            lacc[rows_q, :] = l_new
            return carry

        lax.fori_loop(0, UNITS, unit, 0, unroll=UNIT_UNROLL)
    o_ref[0] = oacc[...]


def _attn_prompt_call(q, k, v, biases, tp):
    n = q.shape[1]
    cur = pl.BlockSpec((1, SPAN, LANES), lambda p, m: (p, m, 0))
    halo = pl.BlockSpec((1, SPAN, LANES), lambda p, m: (p, jnp.maximum(m - 1, 0), 0))
    scratch = [pltpu.VMEM((2 * SPAN, LANES), F32), pltpu.VMEM((2 * SPAN, LANES), F32),
               pltpu.VMEM((SPAN, LANES), F32), pltpu.VMEM((SPAN, LANES), F32)]
    return pl.pallas_call(
        functools.partial(_attn_prompt_kernel, n_prompt=tp // SPAN),
        grid=(N_PAIRS, pl.cdiv(n, SPAN)),
        in_specs=[cur, halo, cur, halo, cur] + [_full(b) for b in biases],
        out_specs=cur, out_shape=jax.ShapeDtypeStruct((N_PAIRS, n, LANES), F32),
        scratch_shapes=scratch, compiler_params=_params(2), name="attn_prompt")(q, k, k, v, v, *biases)


def _alibi_slopes():
    return [2.0 ** (-8.0 * (h + 1) / N_HEADS_B) for h in range(N_HEADS_B)]


def _prompt_bias(dil):
    i = np.arange(QB)[:, None]
    j = np.arange(2 * QB)[None, :]
    rel = i + QB - j
    ok = np.logical_and(rel >= 0, rel <= BAND)
    slopes = np.asarray(_alibi_slopes(), np.float32)[:, None, None]
    pen = (-slopes * (rel * dil).astype(np.float32)[None]) * np.float32(LOG2E)
    full = np.where(ok[None], pen, np.float32(NEG)).astype(np.float32)
    start = np.where((j >= QB)[None], full, np.float32(NEG)).astype(np.float32)
    return jnp.asarray(np.concatenate([full, start], axis=0))


def _branch_multiplicity(dist):
    cnt = np.zeros(dist.shape, np.int32)
    for dil in DILATIONS:
        cnt = cnt + np.logical_and(dist % dil == 0, dist <= BAND * dil).astype(np.int32)
    return cnt


def _sample_bias(n_cache):
    t = np.arange(DEC_SEQ)[:, None]
    dist_c = n_cache + t - np.arange(n_cache)[None, :]
    dist_n = t - np.arange(LANES)[None, :]
    ok_n = np.logical_and(dist_n >= 0, np.arange(LANES)[None, :] < DEC_SEQ)
    slopes = np.asarray(_alibi_slopes(), np.float32)[:, None, None]

    def bias(dist, ok):
        mult = _branch_multiplicity(np.maximum(dist, 0))
        ok = np.logical_and(ok, mult > 0)
        val = -slopes * dist.astype(np.float32)[None] + np.log(np.maximum(mult, 1).astype(np.float32))[None]
        return np.where(ok[None], val, np.float32(NEG)).astype(np.float32)

    bc = bias(dist_c, np.ones(dist_c.shape, bool)).reshape(N_HEADS_B * DEC_SEQ, n_cache)
    bn = bias(dist_n, ok_n).reshape(N_HEADS_B * DEC_SEQ, LANES)
    return jnp.asarray(bc), jnp.asarray(bn)


def _attn_sample_kernel(q_ref, kn_ref, vn_ref, kc_ref, vc_ref, bc_ref, bn_ref, big_ref, o_ref):
    del big_ref
    rows = N_HEADS_B * DEC_SEQ
    wide = lambda ref: jnp.concatenate([ref[pair] for pair in range(N_PAIRS)], axis=1)
    q = wide(q_ref) * (HEAD_DIM ** -0.5)
    qrep = jnp.concatenate([q] * N_HEADS_B, axis=0)
    rh = lax.shift_right_logical(lax.broadcasted_iota(jnp.int32, (rows, SEG), 0), 3)
    ch = lax.shift_right_logical(lax.broadcasted_iota(jnp.int32, (rows, SEG), 1), 6)
    own = rh == ch
    qm = jnp.where(own, qrep, 0.0).astype(BF16)
    pad = jnp.zeros((LANES - DEC_SEQ, SEG), F32)
    kn = jnp.concatenate([wide(kn_ref), pad], axis=0).astype(BF16)
    vn = jnp.concatenate([wide(vn_ref), pad], axis=0).astype(BF16)
    nt = (((1,), (1,)), ((), ()))
    s_c = jnp.dot(qm, kc_ref[0, 0].astype(BF16), preferred_element_type=F32) + bc_ref[...]
    s_n = lax.dot_general(qm, kn, nt, preferred_element_type=F32) + bn_ref[...]
    m = jnp.maximum(jnp.max(s_c, axis=-1, keepdims=True), jnp.max(s_n, axis=-1, keepdims=True))
    p_c = jnp.exp(s_c - m)
    p_n = jnp.exp(s_n - m)
    l = jnp.sum(p_c, axis=-1, keepdims=True) + jnp.sum(p_n, axis=-1, keepdims=True)
    r = (lax.dot_general(p_c.astype(BF16), vc_ref[0, 0].astype(BF16), nt, preferred_element_type=F32)
         + jnp.dot(p_n.astype(BF16), vn, preferred_element_type=F32)) / l
    r = jnp.where(own, r, 0.0)
    o = r[0:DEC_SEQ]
    for h in range(1, N_HEADS_B):
        o = o + r[h * DEC_SEQ:(h + 1) * DEC_SEQ]
    for pair in range(N_PAIRS):
        o_ref[pair] = o[:, pair * LANES:(pair + 1) * LANES]


def _attn_sample_call(q, k, v, cache_kt, cache_vt, layer, bc, bn, o_big, tp):
    bs, n_cache = cache_kt.shape[1], cache_kt.shape[3]
    new = pl.BlockSpec((N_PAIRS, DEC_SEQ, LANES), lambda b: (0, tp // DEC_SEQ + b, 0))
    cache = pl.BlockSpec((1, 1, C_B, n_cache), lambda b: (layer, b, 0, 0))
    return pl.pallas_call(
        _attn_sample_kernel, grid=(bs,),
        in_specs=[new, new, new, cache, cache, _full(bc), _full(bn),
                  pl.BlockSpec(memory_space=pl.ANY)],
        out_specs=new, out_shape=jax.ShapeDtypeStruct(o_big.shape, F32),
        input_output_aliases={7: 0},
        compiler_params=_params(), name="attn_sample")(q, k, v, cache_kt, cache_vt, bc, bn, o_big)


def _mix_kernel(xp_ref, xs_ref, a_ref, o_ref, c_ref, wo_ref, bo_ref, g1_ref, b1_ref, wr_ref, br_ref,
                h_ref, xs3_ref, aux_ref, cnt_ref, *, n_prompt):
    mixed = jnp.concatenate([a_ref[...]] + [o_ref[pair] for pair in range(N_PAIRS)] + [c_ref[...]], axis=1)
    mix = jnp.dot(mixed.astype(BF16), wo_ref[...], preferred_element_type=F32) + bo_ref[...]
    h = _ln_full(DN_ALPHA * _tile_rows(xp_ref, xs_ref, n_prompt) + mix, g1_ref[...], b1_ref[...])
    h_ref[...] = h

    hb = h.astype(BF16)
    h_lo = (h - hb.astype(F32)).astype(BF16)
    both = jnp.dot(hb, wr_ref[...], preferred_element_type=F32)
    logits = (both[:, :LANES] + both[:, LANES:]
              + jnp.dot(h_lo, wr_ref[:, :LANES], preferred_element_type=F32) + br_ref[...])
    lane = lax.broadcasted_iota(jnp.int32, (TT, LANES), 1)
    lane_f = lane.astype(F32)
    vals, sels = [], []
    cur = logits
    for _ in range(TOP_K):
        m = jnp.max(cur, axis=-1, keepdims=True)
        idx = jnp.min(jnp.where(cur == m, lane_f, float(LANES)), axis=-1, keepdims=True)
        sel = lane_f == idx
        vals.append(m)
        sels.append(sel)
        cur = jnp.where(sel, -jnp.inf, cur)
    exps = [jnp.exp(v - vals[0]) for v in vals]
    den = exps[0] + exps[1] + exps[2] + exps[3]

    onehot = jnp.zeros((TT, LANES), F32)
    for sel in sels:
        onehot = onehot + sel.astype(F32)
    r = lax.broadcasted_iota(jnp.int32, (TT, TT), 0)
    c = lax.broadcasted_iota(jnp.int32, (TT, TT), 1)
    below = (c < r).astype(BF16)
    earlier = jnp.dot(below, onehot.astype(BF16), preferred_element_type=F32)
    cnt = jnp.broadcast_to(jnp.sum(onehot, axis=0, keepdims=True), (SUBLANES, LANES))
    er = lax.broadcasted_iota(jnp.int32, (LANES, LANES), 0)
    ec = lax.broadcasted_iota(jnp.int32, (LANES, LANES), 1)
    off = jnp.dot(cnt.astype(BF16), (er < ec).astype(BF16), preferred_element_type=F32)[0:1]
    place = earlier + off
    aux = jnp.zeros((TT, LANES), F32)
    rows = []
    for k in range(TOP_K):
        row = jnp.sum(jnp.where(sels[k], place, 0.0), axis=-1, keepdims=True)
        rows.append(row)
        aux = aux + jnp.where(lane == k, row, 0.0) + jnp.where(lane == TOP_K + k, exps[k] / den, 0.0)
    aux_ref[...] = aux
    cnt_ref[0] = cnt.astype(jnp.int32)

    aux_t = jnp.transpose(aux)
    dest = lax.broadcasted_iota(jnp.int32, (TOP_K * TT, TT), 0).astype(F32)
    disp = jnp.zeros((TOP_K * TT, TT), F32)
    for k in range(TOP_K):
        disp = jnp.where(dest == aux_t[k:k + 1, :], 1.0, disp)
    xs = jnp.dot(disp.astype(BF16), hb, preferred_element_type=F32)
    for s in range(ROW_TILES):
        xs3_ref[pl.ds(s, TOP_K * TT, stride=SUBLANES), :] = xs[:, s * LANES:(s + 1) * LANES]


def _mix_call(x, a_out, o_b, c_out, wo, bo, g1, b1, wr, br, tp):
    n = a_out.shape[0]
    row = lambda w: pl.BlockSpec((TT, w), lambda i: (i, 0))
    xs3 = pl.BlockSpec((TOP_K * TT * SUBLANES, LANES), lambda i: (i, 0))
    cnt = pl.BlockSpec((1, SUBLANES, LANES), lambda i: (i, 0, 0))
    return pl.pallas_call(
        functools.partial(_mix_kernel, n_prompt=tp // TT), grid=(n // TT,),
        in_specs=_tile_row_specs(tp // TT)
        + [row(SEG), pl.BlockSpec((N_PAIRS, TT, LANES), lambda i: (0, i, 0)), row(SEG),
           _full(wo), _full(bo), _full(g1), _full(b1), _full(wr), _full(br)],
        out_specs=[row(D_MODEL), xs3, row(LANES), cnt],
        out_shape=[jax.ShapeDtypeStruct((n, D_MODEL), F32),
                   jax.ShapeDtypeStruct((n * TOP_K * SUBLANES, LANES), F32),
                   jax.ShapeDtypeStruct((n, LANES), F32),
                   jax.ShapeDtypeStruct((n // TT, SUBLANES, LANES), jnp.int32)],
        compiler_params=_params(), name="mix_ln_router")(
            *x, a_out, o_b, c_out, wo, bo, g1, b1, wr, br)


def _rows(start_row, n_rows):
    return pl.ds(pl.multiple_of(start_row * SUBLANES, SUBLANES), n_rows * SUBLANES)


def _expert_gather(tabs, blk, slot, xs3_hbm, buf, sem):
    blk_e, blk_s0, j_lo, j_hi, cnt, off, cum, tot = tabs[:8]
    e = blk_e[blk]
    s0 = blk_s0[blk]
    base = slot * TM

    @pl.when(tot[e] - s0 < TM)
    def _():
        buf[_rows(base, TM), :] = jnp.zeros((TM * SUBLANES, LANES), F32)

    def body(j, carry):
        run = cum[j * N_EXPERTS + e]
        lo = jnp.maximum(run, s0)
        hi = jnp.minimum(run + cnt[j * N_EXPERTS + e], s0 + TM)

        @pl.when(hi > lo)
        def _():
            src = j * (TOP_K * TT) + off[j * N_EXPERTS + e] + (lo - run)
            pltpu.make_async_copy(xs3_hbm.at[_rows(src, hi - lo)],
                                  buf.at[_rows(base + lo - s0, hi - lo)], sem.at[slot]).start()
        return carry

    lax.fori_loop(j_lo[blk], j_hi[blk], body, 0)


def _expert_gather_wait(tabs, blk, slot, xs3_hbm, buf, sem):
    blk_e, blk_s0, tot = tabs[0], tabs[1], tabs[7]
    valid = jnp.minimum(tot[blk_e[blk]] - blk_s0[blk], TM)
    pltpu.make_async_copy(xs3_hbm.at[_rows(0, valid)], buf.at[_rows(slot * TM, valid)],
                          sem.at[slot]).wait()


def _expert_weight_copies(w1_hbm, w2_hbm, expert, w1f, w2f, wslot, wsem):
    return (pltpu.make_async_copy(w1_hbm.at[expert], w1f.at[wslot], wsem.at[0, wslot]),
            pltpu.make_async_copy(w2_hbm.at[expert], w2f.at[wslot], wsem.at[1, wslot]))


def _expert_kernel(*refs, first_expert):
    tabs = refs[:10]
    blk_e, group, next_e = tabs[0], tabs[8], tabs[9]
    (n_used_ref, xs3_hbm, w1_hbm, b1_ref, w2_hbm, b2_ref, y3_ref,
     buf, xb, w1f, w2f, w1b, w2b, sem, wsem) = refs[10:]
    b = pl.program_id(0)
    n_used = n_used_ref[0]
    slot = lax.rem(b, 2)
    used = b < n_used
    wslot = lax.rem(group[b], 2)

    @pl.when(b == 0)
    def _():
        for copy in _expert_weight_copies(w1_hbm, w2_hbm, first_expert + blk_e[0], w1f, w2f, 0, wsem):
            copy.start()

    @pl.when(b == 0)
    def _():
        _expert_gather(tabs, 0, 0, xs3_hbm, buf, sem)

    @pl.when(used)
    def _():
        _expert_gather_wait(tabs, b, slot, xs3_hbm, buf, sem)

    @pl.when(b + 1 < n_used)
    def _():
        _expert_gather(tabs, b + 1, 1 - slot, xs3_hbm, buf, sem)

    new_expert = jnp.logical_or(b == 0, blk_e[b] != blk_e[jnp.maximum(b - 1, 0)])

    @pl.when(jnp.logical_and(used, new_expert))
    def _():
        for copy in _expert_weight_copies(w1_hbm, w2_hbm, first_expert + blk_e[b], w1f, w2f, wslot, wsem):
            copy.wait()

        @pl.when(next_e[b] >= 0)
        def _():
            nxt = first_expert + next_e[b]
            for copy in _expert_weight_copies(w1_hbm, w2_hbm, nxt, w1f, w2f, 1 - wslot, wsem):
                copy.start()

        w1b[...] = w1f[wslot].astype(BF16)
        w2b[...] = w2f[wslot].astype(BF16)

    @pl.when(used)
    def _():
        base = slot * (TM * SUBLANES)
        for s in range(ROW_TILES):
            xb[:, s * LANES:(s + 1) * LANES] = buf[pl.ds(base + s, TM, stride=SUBLANES), :].astype(BF16)
        x = xb[...]
        y = jnp.broadcast_to(b2_ref[0], (TM, D_MODEL))
        for c in range(0, D_FF, FF_CHUNK):
            cols = slice(c, c + FF_CHUNK)
            ucols = slice(D_FF + c, D_FF + c + FF_CHUNK)
            g = jnp.dot(x, w1b[:, cols], preferred_element_type=F32) + b1_ref[0, :, cols]
            u = jnp.dot(x, w1b[:, ucols], preferred_element_type=F32) + b1_ref[0, :, ucols]
            gate = jnp.minimum(g, SWIGLU_LIMIT)
            up = jnp.clip(u, -SWIGLU_LIMIT, SWIGLU_LIMIT)
            hh = (up + 1.0) * (gate * jax.nn.sigmoid(SWIGLU_ALPHA * gate))
            y = y + jnp.dot(hh.astype(BF16), w2b[cols, :], preferred_element_type=F32)
        for s in range(ROW_TILES):
            y3_ref[pl.ds(s, TM, stride=SUBLANES), :] = y[:, s * LANES:(s + 1) * LANES]

    @pl.when(jnp.logical_not(used))
    def _():
        y3_ref[...] = jnp.zeros_like(y3_ref)


def _expert_call(tabs, n_used, xs3, w1, b1, w2, b2, layer):
    n_blocks = tabs[0].shape[0]
    first_expert = layer * N_EXPERTS
    by_expert = lambda shape: pl.BlockSpec(shape, lambda b, e, *_: (first_expert + e[b], 0, 0))
    anywhere = pl.BlockSpec(memory_space=pl.ANY)
    grid_spec = pltpu.PrefetchScalarGridSpec(
        num_scalar_prefetch=len(tabs) + 1, grid=(n_blocks,),
        in_specs=[anywhere, anywhere, by_expert((1, 1, 2 * D_FF)), anywhere, by_expert((1, 1, D_MODEL))],
        out_specs=pl.BlockSpec((TM * SUBLANES, LANES), lambda b, *_: (b, 0)),
        scratch_shapes=[pltpu.VMEM((2 * TM * SUBLANES, LANES), F32),
                        pltpu.VMEM((TM, D_MODEL), BF16),
                        pltpu.VMEM((2, D_MODEL, 2 * D_FF), F32),
                        pltpu.VMEM((2, D_FF, D_MODEL), F32),
                        pltpu.VMEM((D_MODEL, 2 * D_FF), BF16),
                        pltpu.VMEM((D_FF, D_MODEL), BF16),
                        pltpu.SemaphoreType.DMA((2,)),
                        pltpu.SemaphoreType.DMA((2, 2))])
    return pl.pallas_call(
        functools.partial(_expert_kernel, first_expert=first_expert), grid_spec=grid_spec,
        out_shape=jax.ShapeDtypeStruct((n_blocks * TM * SUBLANES, LANES), F32),
        compiler_params=_params(vmem_limit=EXPERT_VMEM_LIMIT), name="expert_ffn")(
            *tabs, n_used, xs3, w1, b1, w2, b2)


PAIRS = TOP_K * TT


def _combine_gather(cnt, off, src, tile, slot, y3_hbm, buf, sem):
    def body(e, carry):
        n = cnt[tile * N_EXPERTS + e]

        @pl.when(n > 0)
        def _():
            pltpu.make_async_copy(y3_hbm.at[_rows(src[tile * N_EXPERTS + e], n)],
                                  buf.at[_rows(slot * PAIRS + off[tile * N_EXPERTS + e], n)],
                                  sem.at[slot]).start()
        return carry

    lax.fori_loop(0, N_EXPERTS, body, 0, unroll=4)


def _combine_kernel(cnt, off, src, y3_hbm, h_ref, aux_ref, g2_ref, b2_ref, op_ref, os_ref, buf, ys, sem,
                    *, n_prompt):
    i = pl.program_id(0)
    n_steps = pl.num_programs(0)
    slot = lax.rem(i, 2)

    @pl.when(i == 0)
    def _():
        _combine_gather(cnt, off, src, 0, 0, y3_hbm, buf, sem)

    pltpu.make_async_copy(y3_hbm.at[_rows(0, PAIRS)], buf.at[_rows(slot * PAIRS, PAIRS)],
                          sem.at[slot]).wait()

    @pl.when(i + 1 < n_steps)
    def _():
        _combine_gather(cnt, off, src, i + 1, 1 - slot, y3_hbm, buf, sem)

    base = slot * (PAIRS * SUBLANES)
    for s in range(ROW_TILES):
        ys[:, s * LANES:(s + 1) * LANES] = buf[pl.ds(base + s, PAIRS, stride=SUBLANES), :].astype(BF16)
    aux = aux_ref[...]
    dest = lax.broadcasted_iota(jnp.int32, (TT, PAIRS), 1).astype(F32)
    weights = jnp.zeros((TT, PAIRS), F32)
    for k in range(TOP_K):
        weights = jnp.where(dest == aux[:, k:k + 1], aux[:, TOP_K + k:TOP_K + k + 1], weights)
    moe = jnp.dot(weights.astype(BF16), ys[...], preferred_element_type=F32)
    y = _ln_full(DN_ALPHA * h_ref[...] + moe, g2_ref[...], b2_ref[...])

    @pl.when(i < n_prompt)
    def _():
        op_ref[...] = y

    @pl.when(i >= n_prompt)
    def _():
        os_ref[...] = y


def _combine_call(cnt, off, src, y3, h, aux, g2, b2, tp):
    n = h.shape[0]
    row = lambda w: pl.BlockSpec((TT, w), lambda i, *_: (i, 0))
    n_prompt = tp // TT
    out_specs = _tile_row_specs(n_prompt)
    out_shape = [jax.ShapeDtypeStruct((tp, D_MODEL), F32), jax.ShapeDtypeStruct((n - tp, D_MODEL), F32)]
    grid_spec = pltpu.PrefetchScalarGridSpec(
        num_scalar_prefetch=3, grid=(n // TT,),
        in_specs=[pl.BlockSpec(memory_space=pl.ANY), row(D_MODEL), row(LANES),
                  pl.BlockSpec(g2.shape, lambda i, *_: (0, 0)), pl.BlockSpec(b2.shape, lambda i, *_: (0, 0))],
        out_specs=out_specs,
        scratch_shapes=[pltpu.VMEM((2 * PAIRS * SUBLANES, LANES), F32),
                        pltpu.VMEM((PAIRS, D_MODEL), BF16),
                        pltpu.SemaphoreType.DMA((2,))])
    return pl.pallas_call(
        functools.partial(_combine_kernel, n_prompt=n_prompt), grid_spec=grid_spec, out_shape=out_shape,
        compiler_params=_params(), name="combine_ln")(cnt, off, src, y3, h, aux, g2, b2)


def _routing_tables(cnt_tiles, n):
    cnt = cnt_tiles[:, 0, :N_EXPERTS]
    off = jnp.cumsum(cnt, axis=1) - cnt
    cum = jnp.cumsum(cnt, axis=0) - cnt
    tot = jnp.sum(cnt, axis=0)
    padded = (tot + TM - 1) // TM * TM
    pends = jnp.cumsum(padded)
    pstart = pends - padded
    n_blocks = n * TOP_K // TM + N_EXPERTS
    blk_start = jnp.arange(n_blocks, dtype=jnp.int32) * TM
    blk_e = jnp.minimum(jnp.sum((pends[None, :] <= blk_start[:, None]).astype(jnp.int32), axis=1),
                        N_EXPERTS - 1)
    ids = jnp.arange(N_EXPERTS, dtype=jnp.int32)
    mine = blk_e[:, None] == ids[None, :]
    pick = lambda per_expert: jnp.sum(jnp.where(mine, per_expert[None, :], 0), axis=1)
    blk_s0 = blk_start - pick(pstart)
    lo3, hi3, mine3 = blk_s0[:, None, None], blk_s0[:, None, None] + TM, mine[:, None, :]
    j_lo = jnp.sum(jnp.logical_and(mine3, (cum + cnt)[None] <= lo3).astype(jnp.int32), axis=(1, 2))
    j_hi = jnp.sum(jnp.logical_and(mine3, cum[None] < hi3).astype(jnp.int32), axis=(1, 2))
    n_used = (pends[-1:] // TM).astype(jnp.int32)
    has = tot > 0
    group_of = jnp.cumsum(has.astype(jnp.int32)) - 1
    later = jnp.where(jnp.logical_and(has[None, :], ids[None, :] > ids[:, None]), ids[None, :], N_EXPERTS)
    next_of = jnp.min(later, axis=1)
    next_of = jnp.where(next_of == N_EXPERTS, -1, next_of)
    flat = lambda a: a.reshape(-1).astype(jnp.int32)
    expert_tabs = (flat(blk_e), flat(blk_s0), flat(j_lo), flat(j_hi), flat(cnt), flat(off), flat(cum), flat(tot),
                   flat(pick(group_of)), flat(pick(next_of)))
    combine_tabs = (flat(cnt), flat(off), flat(pstart[None, :] + cum))
    return expert_tabs, n_used, combine_tabs


def _pad_cols(a, width):
    return jnp.pad(a, [(0, 0)] * (a.ndim - 1) + [(0, width - a.shape[-1])])


def _layer_params(l, w_in, b_in, w_dw, b_dw, ln_a_g, ln_a_b, ln_c_g, ln_c_b, w_sp, b_sp,
                  w_out, b_out, ln1_g, ln1_b, w_router, b_router, ln2_g, ln2_b):
    bounds = [0, C_B, 2 * C_B, 3 * C_B, 3 * C_B + C_A, 3 * C_B + 2 * C_A, 3 * C_B + 2 * C_A + C_C,
              3 * C_B + 2 * C_A + 2 * C_C]
    w_segs = [_pad_cols(w_in[l][:, bounds[i]:bounds[i + 1]], SEG) for i in range(N_SEG)]
    b_segs = [_pad_cols(b_in[l][bounds[i]:bounds[i + 1]], SEG) for i in range(N_SEG)]
    row = lambda a: _pad_cols(a, SEG)[None, :]
    wo = w_out[l]
    pad_rows = lambda a: jnp.pad(a, ((0, SEG - a.shape[0]), (0, 0)))
    bsp = b_sp[l]
    bfull = _pad_cols(jnp.repeat(bsp.T, HEAD_DIM, axis=1), SEG)
    wr = _pad_cols(w_router[l], LANES)
    wr_hi = wr.astype(BF16)
    return dict(
        w_in=jnp.concatenate(w_segs, axis=1).astype(BF16),
        b_in=jnp.concatenate(b_segs)[None, :],
        w_dw=jnp.pad(w_dw[l], ((0, HALO - CONV_W), (0, SEG - C_A))),
        b_dw=row(b_dw[l]), ln_a_g=row(ln_a_g[l]), ln_a_b=row(ln_a_b[l]),
        ln_c_g=row(ln_c_g[l]), ln_c_b=row(ln_c_b[l]),
        w_sp=w_sp[l], b_full=bfull,
        w_out=jnp.concatenate([pad_rows(wo[:C_A]), wo[C_A:C_A + C_B], pad_rows(wo[C_A + C_B:])]).astype(BF16),
        b_out=b_out[l][None, :],
        ln1_g=ln1_g[l][None, :], ln1_b=ln1_b[l][None, :],
        w_router=jnp.concatenate([wr_hi, (wr - wr_hi.astype(F32)).astype(BF16)], axis=1),
        b_router=jnp.concatenate([b_router[l], jnp.full((LANES - N_EXPERTS,), NEG, F32)])[None, :],
        ln2_g=ln2_g[l][None, :], ln2_b=ln2_b[l][None, :])


def _layer(x, p, hist_p, caches, experts, biases, n, tp, layer):
    conv_args = (p["w_dw"], p["b_dw"], p["ln_a_g"], p["ln_a_b"])
    q, k, v, a, a_out, c_out, vn_s = _proj_call(x, p["w_in"], p["b_in"], p["ln_c_g"], p["ln_c_b"],
                                                p["w_sp"], p["b_full"], *conv_args, n, tp)
    a_out, new_a_s = _conv_sample_call(hist_p, a, *conv_args, a_out, tp)

    o_b = _attn_prompt_call(q, k, v, biases["prompt"], tp)
    o_b = _attn_sample_call(q, k, v, *caches, layer, *biases["sample"], o_b, tp)

    h, xs3, aux, cnt_tiles = _mix_call(x, a_out, o_b, c_out, p["w_out"], p["b_out"],
                                       p["ln1_g"], p["ln1_b"], p["w_router"], p["b_router"], tp)
    expert_tabs, n_used, combine_tabs = _routing_tables(cnt_tiles, n)
    y3 = _expert_call(expert_tabs, n_used, xs3, *experts, layer)
    y = _combine_call(*combine_tabs, y3, h, aux, p["ln2_g"], p["ln2_b"], tp)
    return y, a, k, v, vn_s, new_a_s


def kernel(x_prompt, x_sample, state_a_conv, cache_b_k, cache_b_v, w_in, b_in, w_dw, b_dw, ln_a_g, ln_a_b, ln_c_g, ln_c_b, w_sp, b_sp, w_out, b_out, ln1_g, ln1_b, w_router, b_router, w1, b1, w2, b2, ln2_g, ln2_b):
    bp, tp, _ = x_prompt.shape
    bs, ts, _ = x_sample.shape
    n_cache = cache_b_k.shape[2]
    assert bp == 1 and ts == DEC_SEQ and bs * ts == TT and tp % WIN_MAX == 0
    assert n_cache == WIN_MAX
    n = tp + bs * ts
    keep = min(WIN_MAX, tp)
    hist = CONV_W - 1

    weights = (w_in, b_in, w_dw, b_dw, ln_a_g, ln_a_b, ln_c_g, ln_c_b, w_sp, b_sp, w_out, b_out,
               ln1_g, ln1_b, w_router, b_router, ln2_g, ln2_b)
    biases = {"prompt": [_prompt_bias(d) for d in DILATIONS], "sample": _sample_bias(n_cache)}
    experts = (w1.reshape(DEPTH * N_EXPERTS, D_MODEL, 2 * D_FF), b1.reshape(DEPTH * N_EXPERTS, 1, 2 * D_FF),
               w2.reshape(DEPTH * N_EXPERTS, D_FF, D_MODEL), b2.reshape(DEPTH * N_EXPERTS, 1, D_MODEL))
    to_feature_major = lambda c: jnp.transpose(c, (0, 1, 3, 4, 2)).reshape(DEPTH, bs, C_B, n_cache)
    caches = (to_feature_major(cache_b_k), to_feature_major(cache_b_v))
    heads = lambda a: jnp.transpose(a, (1, 0, 2)).reshape(a.shape[1], N_HEADS_B, HEAD_DIM)

    x = (x_prompt[0], x_sample.reshape(bs * ts, D_MODEL))
    outs = {name: [] for name in ("a_p", "a_s", "k_p", "v_p", "k_s", "v_s", "c_s")}
    for l in range(DEPTH):
        p = _layer_params(l, *weights)
        hist_p = jnp.pad(state_a_conv[l], ((0, 0), (HALO - hist, 0), (0, SEG - C_A)))
        x, a, k, v, vn_s, new_a_s = _layer(x, p, hist_p, caches, experts, biases, n, tp, l)
        outs["a_p"].append(a[tp - hist:tp, :C_A][None])
        outs["a_s"].append(new_a_s[:, HALO - hist:, :C_A])
        outs["k_p"].append(heads(k[:, tp - keep:tp])[None])
        outs["v_p"].append(heads(v[:, tp - keep:tp])[None])
        outs["k_s"].append(heads(k[:, tp:]).reshape(bs, ts, N_HEADS_B, HEAD_DIM))
        outs["v_s"].append(heads(v[:, tp:]).reshape(bs, ts, N_HEADS_B, HEAD_DIM))
        outs["c_s"].append(vn_s[:, :C_C].reshape(bs, ts, C_C))
    stack = lambda name: jnp.stack(outs[name])
    y_prompt, y_sample = x
    return (y_prompt[None], y_sample.reshape(bs, ts, D_MODEL), stack("a_p"), stack("a_s"),
            stack("k_p"), stack("v_p"), stack("k_s"), stack("v_s"), stack("c_s"))
```

```python
import functools
import math

import jax
import jax.numpy as jnp
import numpy as np
from jax import lax
from jax.experimental import pallas as pl
from jax.experimental.pallas import tpu as pltpu

F32 = jnp.float32
BF16 = jnp.bfloat16

D_MODEL = 1024
HEAD_DIM = 64
N_HEADS_B = 6
C_B = N_HEADS_B * HEAD_DIM
C_A = 320
C_C = 320
N_GROUPS_C = 5
CONV_W = 31
DILATIONS = (1, 4, 16)
BAND = 128
WIN_MAX = 2048
CHUNK = 128
N_EXPERTS = 32
TOP_K = 4
D_FF = 1024
SWIGLU_LIMIT = 7.0
SWIGLU_ALPHA = 1.702
LN_EPS = 1e-5
DEPTH = 2
DN_ALPHA = (2.0 * DEPTH) ** 0.25
DEC_SEQ = 8

LANES = 128
SUBLANES = 8
SEG = 384
N_SEG = 7
N_PAIRS = C_B // LANES
ROW_TILES = D_MODEL // LANES
assert ROW_TILES == SUBLANES

TT = 256
TM = 512
FF_CHUNK = 1024
NEG = -1e30
LOG2E = math.log2(math.e)
MIB = 1024 * 1024
VMEM_LIMIT = 48 * MIB
EXPERT_VMEM_LIMIT = 56 * MIB


def _params(n_axes=1, vmem_limit=VMEM_LIMIT):
    return pltpu.CompilerParams(dimension_semantics=("arbitrary",) * n_axes,
                                vmem_limit_bytes=vmem_limit)


def _full(a):
    nd = a.ndim
    return pl.BlockSpec(a.shape, lambda *_: (0,) * nd)


def _ln_valid(x, g, b, n_valid):
    col = lax.broadcasted_iota(jnp.int32, x.shape, 1)
    ok = col < n_valid
    mu = jnp.sum(jnp.where(ok, x, 0.0), axis=-1, keepdims=True) / n_valid
    xc = jnp.where(ok, x - mu, 0.0)
    var = jnp.sum(xc * xc, axis=-1, keepdims=True) / n_valid
    return xc * lax.rsqrt(var + LN_EPS) * g + b


def _ln_full(x, g, b):
    mu = jnp.mean(x, axis=-1, keepdims=True)
    xc = x - mu
    var = jnp.mean(xc * xc, axis=-1, keepdims=True)
    return xc * lax.rsqrt(var + LN_EPS) * g + b


def _spatial_gate(vn, cu, w_ref, bfull, rows, sub):
    r = lax.broadcasted_iota(jnp.int32, (rows, rows), 0)
    c = lax.broadcasted_iota(jnp.int32, (rows, rows), 1)
    causal = c <= r
    if sub is not None:
        causal = jnp.logical_and(causal, lax.shift_right_logical(r, sub) == lax.shift_right_logical(c, sub))
        pos = lax.broadcasted_iota(jnp.int32, (rows, CHUNK), 0) & (2 ** sub - 1)
        spread = (pos == lax.broadcasted_iota(jnp.int32, (rows, CHUNK), 1)).astype(BF16)
        bfull = jnp.concatenate([bfull[0:2 ** sub]] * (rows // 2 ** sub), axis=0)
    lo_half = lax.broadcasted_iota(jnp.int32, (rows, LANES), 1) < HEAD_DIM
    cols = []
    for pair in range(SEG // LANES):
        cs = slice(pair * LANES, (pair + 1) * LANES)
        acc = jnp.zeros((rows, LANES), F32)
        for hh in range(2):
            g = pair * 2 + hh
            if g >= N_GROUPS_C:
                continue
            keep = lo_half if hh == 0 else jnp.logical_not(lo_half)
            w = w_ref[g].astype(BF16)
            if sub is not None:
                w = jnp.dot(spread, w, preferred_element_type=F32).astype(BF16)
                w = lax.dot_general(w, spread, (((1,), (1,)), ((), ())), preferred_element_type=F32).astype(BF16)
            wm = jnp.where(causal, w, 0.0)
            vm = jnp.where(keep, vn[:, cs], 0.0).astype(BF16)
            acc = acc + jnp.dot(wm, vm, preferred_element_type=F32)
        cols.append(cu[:, cs] * (acc + bfull[:, cs]))
    return jnp.concatenate(cols, axis=1)


def _tile_rows(xp_ref, xs_ref, n_prompt):
    return jnp.where(pl.program_id(0) < n_prompt, xp_ref[...], xs_ref[...])


def _tile_row_specs(n_prompt):
    return [pl.BlockSpec((TT, D_MODEL), lambda i, *_: (jnp.minimum(i, n_prompt - 1), 0)),
            pl.BlockSpec((TT, D_MODEL), lambda i, *_: (jnp.maximum(i - n_prompt, 0), 0))]


def _proj_kernel(xp_ref, xs_ref, w_ref, b_ref, lcg_ref, lcb_ref, wsp_ref, bsp_ref,
                 wdw_ref, bdw_ref, lag_ref, lab_ref,
                 q_ref, k_ref, v_ref, a_ref, ao_ref, c_ref, vns_ref, abuf, shifted, *, n_prompt):
    x = _tile_rows(xp_ref, xs_ref, n_prompt)
    proj = jnp.dot(x.astype(BF16), w_ref[...], preferred_element_type=F32) + b_ref[...]
    seg = lambda i: proj[:, i * SEG:(i + 1) * SEG]

    for i, ref in enumerate((q_ref, k_ref, v_ref)):
        qkv = seg(i)
        for pair in range(N_PAIRS):
            ref[pair] = qkv[:, pair * LANES:(pair + 1) * LANES]
    a = seg(3) * jax.nn.sigmoid(seg(4))
    a_ref[...] = a
    cu = seg(5)
    vn = _ln_valid(seg(6), lcg_ref[...], lcb_ref[...], C_C)
    i = pl.program_id(0)
    is_prompt = i < n_prompt

    @pl.when(i == 0)
    def _():
        abuf[0:HALO, :] = jnp.zeros((HALO, SEG), F32)

    @pl.when(jnp.logical_and(i > 0, is_prompt))
    def _():
        abuf[0:HALO, :] = abuf[TT:TT + HALO, :]

    @pl.when(is_prompt)
    def _():
        abuf[HALO:, :] = a
        _conv_tile(abuf, shifted, wdw_ref, bdw_ref, lag_ref, lab_ref, ao_ref)
        for ch in range(0, TT, CHUNK):
            rows = slice(ch, ch + CHUNK)
            c_ref[rows, :] = _spatial_gate(vn[rows], cu[rows], wsp_ref, bsp_ref[...], CHUNK, None)

    @pl.when(jnp.logical_not(is_prompt))
    def _():
        ao_ref[...] = jnp.zeros_like(ao_ref)
        c_ref[...] = _spatial_gate(vn, cu, wsp_ref, bsp_ref[...], TT, DEC_SEQ.bit_length() - 1)
        vns_ref[...] = vn


def _proj_call(x, w_in_p, b_in_p, lcg, lcb, w_sp, b_full, w_dw_p, bdw, lag, lab, n, tp):
    row = lambda w: pl.BlockSpec((TT, w), lambda i: (i, 0))
    heads = pl.BlockSpec((N_PAIRS, TT, LANES), lambda i: (0, i, 0))
    out = jax.ShapeDtypeStruct((n, SEG), F32)
    out3 = jax.ShapeDtypeStruct((N_PAIRS, n, LANES), F32)
    consts = (w_in_p, b_in_p, lcg, lcb, w_sp, b_full, w_dw_p, bdw, lag, lab)
    return pl.pallas_call(
        functools.partial(_proj_kernel, n_prompt=tp // TT), grid=(n // TT,),
        in_specs=_tile_row_specs(tp // TT) + [_full(c) for c in consts],
        out_specs=[heads] * 3 + [row(SEG)] * 3 + [pl.BlockSpec((TT, SEG), lambda i: (0, 0))],
        out_shape=[out3] * 3 + [out] * 3 + [jax.ShapeDtypeStruct((n - tp, SEG), F32)],
        scratch_shapes=[pltpu.VMEM((HALO + TT, SEG), F32),
                        pltpu.VMEM((SUBLANES - 1, HALO + TT, SEG), F32)],
        compiler_params=_params(), name="in_proj")(*x, *consts)


HALO = 32
CONV_CHUNK = 32


def _conv_tail(acc, bdw_ref, g_ref, b_ref):
    return jax.nn.silu(_ln_valid(acc + bdw_ref[...], g_ref[...], b_ref[...], C_A))


def _conv_tile(buf, shifted, w_ref, bdw_ref, g_ref, b_ref, o_ref):
    n_rows = HALO + TT - SUBLANES
    for sh in range(1, SUBLANES):
        shifted[sh - 1, 0:n_rows, :] = buf[sh:sh + n_rows, :]
    base = HALO - (CONV_W - 1)
    for r in range(0, TT, CONV_CHUNK):
        acc = jnp.zeros((CONV_CHUNK, SEG), F32)
        for j in range(CONV_W):
            tiles, sh = divmod(base + j, SUBLANES)
            src = buf if sh == 0 else shifted.at[sh - 1]
            lo = r + tiles * SUBLANES
            acc = acc + w_ref[j:j + 1, :] * src[lo:lo + CONV_CHUNK, :]
        o_ref[r:r + CONV_CHUNK, :] = _conv_tail(acc, bdw_ref, g_ref, b_ref)


SB = 8


def _conv_sample_kernel(hist_ref, a_ref, w_ref, bdw_ref, g_ref, b_ref, big_ref,
                        o_ref, newa_ref, buf):
    del big_ref
    bs = hist_ref.shape[0]
    buf[:, 0:HALO, :] = hist_ref[...]
    buf[:, HALO:, :] = a_ref[...].reshape(bs, DEC_SEQ, SEG)
    base = HALO - (CONV_W - 1)
    for s in range(0, bs, SB):
        acc = jnp.zeros((SB, DEC_SEQ, SEG), F32)
        for j in range(CONV_W):
            acc = acc + w_ref[j:j + 1, :][None] * buf[s:s + SB, base + j: base + j + DEC_SEQ, :]
        y = _conv_tail(acc.reshape(SB * DEC_SEQ, SEG), bdw_ref, g_ref, b_ref)
        o_ref[s * DEC_SEQ:(s + SB) * DEC_SEQ, :] = y
    newa_ref[...] = buf[:, DEC_SEQ:, :]


def _conv_sample_call(hist_p, a, w_dw_p, bdw, g, b, a_out, tp):
    bs = hist_p.shape[0]
    rows = bs * DEC_SEQ
    blk = pl.BlockSpec((rows, SEG), lambda i: (tp // rows, 0))
    return pl.pallas_call(
        _conv_sample_kernel, grid=(1,),
        in_specs=[_full(hist_p), blk, _full(w_dw_p), _full(bdw), _full(g), _full(b),
                  pl.BlockSpec(memory_space=pl.ANY)],
        out_specs=[blk, pl.BlockSpec((bs, HALO, SEG), lambda i: (0, 0, 0))],
        out_shape=[jax.ShapeDtypeStruct(a_out.shape, F32),
                   jax.ShapeDtypeStruct((bs, HALO, SEG), F32)],
        scratch_shapes=[pltpu.VMEM((bs, HALO + DEC_SEQ, SEG), F32)],
        input_output_aliases={6: 0},
        compiler_params=_params(), name="conv_sample")(hist_p, a, w_dw_p, bdw, g, b, a_out)


QB = 128
assert QB == BAND


SPAN = BAND * max(DILATIONS)
UNITS = SPAN // QB
UNIT_UNROLL = 16


def _attn_prompt_kernel(q_ref, kh_ref, kc_ref, vh_ref, vc_ref, b1_ref, b4_ref, b16_ref, o_ref,
                        kbuf, vbuf, oacc, lacc, *, n_prompt):
    is_prompt = pl.program_id(1) < n_prompt
    refs = (q_ref, kh_ref, kc_ref, vh_ref, vc_ref, (b1_ref, b4_ref, b16_ref), o_ref, kbuf, vbuf, oacc, lacc)
    pl.when(is_prompt)(functools.partial(_attn_prompt_span, refs))

    @pl.when(jnp.logical_not(is_prompt))
    def _():
        o_ref[...] = jnp.zeros_like(o_ref)


def _attn_prompt_span(refs):
    q_ref, kh_ref, kc_ref, vh_ref, vc_ref, bias_refs, o_ref, kbuf, vbuf, oacc, lacc = refs
    pair = pl.program_id(0)
    first_span = pl.program_id(1) == 0
    kbuf[0:SPAN, :] = kh_ref[0]
    kbuf[SPAN:, :] = kc_ref[0]
    vbuf[0:SPAN, :] = vh_ref[0]
    vbuf[SPAN:, :] = vc_ref[0]
    lane = lax.broadcasted_iota(jnp.int32, (QB, LANES), 1)
    lo_half = lane < HEAD_DIM

    for dil, bias_ref in zip(DILATIONS, bias_refs):
        def unit(u, carry, dil=dil, bias_ref=bias_ref):
            sub = u // dil
            t0 = sub * (QB * dil) + (u - sub * dil)
            if dil == 1:
                rows_q = pl.ds(pl.multiple_of(t0, QB), QB)
                rows_k = pl.ds(pl.multiple_of(SPAN + t0 - QB, QB), 2 * QB)
            else:
                rows_q = pl.ds(t0, QB, stride=dil)
                rows_k = pl.ds(SPAN + t0 - QB * dil, 2 * QB, stride=dil)
            table = jnp.logical_and(first_span, sub == 0).astype(jnp.int32) * N_HEADS_B + pair * 2
            qp = q_ref[0, rows_q, :] * (HEAD_DIM ** -0.5 * LOG2E)
            kk = kbuf[rows_k, :].astype(BF16)
            vv = vbuf[rows_k, :].astype(BF16)
            outs, lses = [], []
            for hh in range(2):
                keep = lo_half if hh == 0 else jnp.logical_not(lo_half)
                qm = jnp.where(keep, qp, 0.0).astype(BF16)
                s = lax.dot_general(qm, kk, (((1,), (1,)), ((), ())), preferred_element_type=F32)
                s = s + bias_ref[table + hh]
                m = jnp.max(s, axis=-1, keepdims=True)
                p = jnp.exp2(s - m)
                l = jnp.sum(p, axis=-1, keepdims=True)
                pv = jnp.dot(p.astype(BF16), vv, preferred_element_type=F32)
                outs.append(pv / l)
                lses.append(m + jnp.log2(l))
            o_new = jnp.where(lo_half, outs[0], outs[1])
            l_new = jnp.where(lo_half, lses[0], lses[1])
            if dil != 1:
                l_old = lacc[rows_q, :]
                mx = jnp.maximum(l_old, l_new)
                w_old = jnp.exp2(l_old - mx)
                w_new = jnp.exp2(l_new - mx)
                tot = w_old + w_new
                o_new = (w_old * oacc[rows_q, :] + w_new * o_new) / tot
                l_new = mx + jnp.log2(tot)
            oacc[rows_q, :] = o_new
            lacc[rows_q, :] = l_new
            return carry

        lax.fori_loop(0, UNITS, unit, 0, unroll=UNIT_UNROLL)
    o_ref[0] = oacc[...]


def _attn_prompt_call(q, k, v, biases, tp):
    n = q.shape[1]
    cur = pl.BlockSpec((1, SPAN, LANES), lambda p, m: (p, m, 0))
    halo = pl.BlockSpec((1, SPAN, LANES), lambda p, m: (p, jnp.maximum(m - 1, 0), 0))
    scratch = [pltpu.VMEM((2 * SPAN, LANES), F32), pltpu.VMEM((2 * SPAN, LANES), F32),
               pltpu.VMEM((SPAN, LANES), F32), pltpu.VMEM((SPAN, LANES), F32)]
    return pl.pallas_call(
        functools.partial(_attn_prompt_kernel, n_prompt=tp // SPAN),
        grid=(N_PAIRS, pl.cdiv(n, SPAN)),
        in_specs=[cur, halo, cur, halo, cur] + [_full(b) for b in biases],
        out_specs=cur, out_shape=jax.ShapeDtypeStruct((N_PAIRS, n, LANES), F32),
        scratch_shapes=scratch, compiler_params=_params(2), name="attn_prompt")(q, k, k, v, v, *biases)


def _alibi_slopes():
    return [2.0 ** (-8.0 * (h + 1) / N_HEADS_B) for h in range(N_HEADS_B)]


def _prompt_bias(dil):
    i = np.arange(QB)[:, None]
    j = np.arange(2 * QB)[None, :]
    rel = i + QB - j
    ok = np.logical_and(rel >= 0, rel <= BAND)
    slopes = np.asarray(_alibi_slopes(), np.float32)[:, None, None]
    pen = (-slopes * (rel * dil).astype(np.float32)[None]) * np.float32(LOG2E)
    full = np.where(ok[None], pen, np.float32(NEG)).astype(np.float32)
    start = np.where((j >= QB)[None], full, np.float32(NEG)).astype(np.float32)
    return jnp.asarray(np.concatenate([full, start], axis=0))


def _branch_multiplicity(dist):
    cnt = np.zeros(dist.shape, np.int32)
    for dil in DILATIONS:
        cnt = cnt + np.logical_and(dist % dil == 0, dist <= BAND * dil).astype(np.int32)
    return cnt


def _sample_bias(n_cache):
    t = np.arange(DEC_SEQ)[:, None]
    dist_c = n_cache + t - np.arange(n_cache)[None, :]
    dist_n = t - np.arange(LANES)[None, :]
    ok_n = np.logical_and(dist_n >= 0, np.arange(LANES)[None, :] < DEC_SEQ)
    slopes = np.asarray(_alibi_slopes(), np.float32)[:, None, None]

    def bias(dist, ok):
        mult = _branch_multiplicity(np.maximum(dist, 0))
        ok = np.logical_and(ok, mult > 0)
        val = -slopes * dist.astype(np.float32)[None] + np.log(np.maximum(mult, 1).astype(np.float32))[None]
        return np.where(ok[None], val, np.float32(NEG)).astype(np.float32)

    bc = bias(dist_c, np.ones(dist_c.shape, bool)).reshape(N_HEADS_B * DEC_SEQ, n_cache)
    bn = bias(dist_n, ok_n).reshape(N_HEADS_B * DEC_SEQ, LANES)
    return jnp.asarray(bc), jnp.asarray(bn)


def _attn_sample_kernel(q_ref, kn_ref, vn_ref, kc_ref, vc_ref, bc_ref, bn_ref, big_ref, o_ref):
    del big_ref
    rows = N_HEADS_B * DEC_SEQ
    wide = lambda ref: jnp.concatenate([ref[pair] for pair in range(N_PAIRS)], axis=1)
    q = wide(q_ref) * (HEAD_DIM ** -0.5)
    qrep = jnp.concatenate([q] * N_HEADS_B, axis=0)
    rh = lax.shift_right_logical(lax.broadcasted_iota(jnp.int32, (rows, SEG), 0), 3)
    ch = lax.shift_right_logical(lax.broadcasted_iota(jnp.int32, (rows, SEG), 1), 6)
    own = rh == ch
    qm = jnp.where(own, qrep, 0.0).astype(BF16)
    pad = jnp.zeros((LANES - DEC_SEQ, SEG), F32)
    kn = jnp.concatenate([wide(kn_ref), pad], axis=0).astype(BF16)
    vn = jnp.concatenate([wide(vn_ref), pad], axis=0).astype(BF16)
    nt = (((1,), (1,)), ((), ()))
    s_c = jnp.dot(qm, kc_ref[0, 0].astype(BF16), preferred_element_type=F32) + bc_ref[...]
    s_n = lax.dot_general(qm, kn, nt, preferred_element_type=F32) + bn_ref[...]
    m = jnp.maximum(jnp.max(s_c, axis=-1, keepdims=True), jnp.max(s_n, axis=-1, keepdims=True))
    p_c = jnp.exp(s_c - m)
    p_n = jnp.exp(s_n - m)
    l = jnp.sum(p_c, axis=-1, keepdims=True) + jnp.sum(p_n, axis=-1, keepdims=True)
    r = (lax.dot_general(p_c.astype(BF16), vc_ref[0, 0].astype(BF16), nt, preferred_element_type=F32)
         + jnp.dot(p_n.astype(BF16), vn, preferred_element_type=F32)) / l
    r = jnp.where(own, r, 0.0)
    o = r[0:DEC_SEQ]
    for h in range(1, N_HEADS_B):
        o = o + r[h * DEC_SEQ:(h + 1) * DEC_SEQ]
    for pair in range(N_PAIRS):
        o_ref[pair] = o[:, pair * LANES:(pair + 1) * LANES]


def _attn_sample_call(q, k, v, cache_kt, cache_vt, layer, bc, bn, o_big, tp):
    bs, n_cache = cache_kt.shape[1], cache_kt.shape[3]
    new = pl.BlockSpec((N_PAIRS, DEC_SEQ, LANES), lambda b: (0, tp // DEC_SEQ + b, 0))
    cache = pl.BlockSpec((1, 1, C_B, n_cache), lambda b: (layer, b, 0, 0))
    return pl.pallas_call(
        _attn_sample_kernel, grid=(bs,),
        in_specs=[new, new, new, cache, cache, _full(bc), _full(bn),
                  pl.BlockSpec(memory_space=pl.ANY)],
        out_specs=new, out_shape=jax.ShapeDtypeStruct(o_big.shape, F32),
        input_output_aliases={7: 0},
        compiler_params=_params(), name="attn_sample")(q, k, v, cache_kt, cache_vt, bc, bn, o_big)


def _mix_kernel(xp_ref, xs_ref, a_ref, o_ref, c_ref, wo_ref, bo_ref, g1_ref, b1_ref, wr_ref, br_ref,
                h_ref, xs3_ref, aux_ref, cnt_ref, *, n_prompt):
    mixed = jnp.concatenate([a_ref[...]] + [o_ref[pair] for pair in range(N_PAIRS)] + [c_ref[...]], axis=1)
    mix = jnp.dot(mixed.astype(BF16), wo_ref[...], preferred_element_type=F32) + bo_ref[...]
    h = _ln_full(DN_ALPHA * _tile_rows(xp_ref, xs_ref, n_prompt) + mix, g1_ref[...], b1_ref[...])
    h_ref[...] = h

    hb = h.astype(BF16)
    h_lo = (h - hb.astype(F32)).astype(BF16)
    both = jnp.dot(hb, wr_ref[...], preferred_element_type=F32)
    logits = (both[:, :LANES] + both[:, LANES:]
              + jnp.dot(h_lo, wr_ref[:, :LANES], preferred_element_type=F32) + br_ref[...])
    lane = lax.broadcasted_iota(jnp.int32, (TT, LANES), 1)
    lane_f = lane.astype(F32)
    vals, sels = [], []
    cur = logits
    for _ in range(TOP_K):
        m = jnp.max(cur, axis=-1, keepdims=True)
        idx = jnp.min(jnp.where(cur == m, lane_f, float(LANES)), axis=-1, keepdims=True)
        sel = lane_f == idx
        vals.append(m)
        sels.append(sel)
        cur = jnp.where(sel, -jnp.inf, cur)
    exps = [jnp.exp(v - vals[0]) for v in vals]
    den = exps[0] + exps[1] + exps[2] + exps[3]

    onehot = jnp.zeros((TT, LANES), F32)
    for sel in sels:
        onehot = onehot + sel.astype(F32)
    r = lax.broadcasted_iota(jnp.int32, (TT, TT), 0)
    c = lax.broadcasted_iota(jnp.int32, (TT, TT), 1)
    below = (c < r).astype(BF16)
    earlier = jnp.dot(below, onehot.astype(BF16), preferred_element_type=F32)
    cnt = jnp.broadcast_to(jnp.sum(onehot, axis=0, keepdims=True), (SUBLANES, LANES))
    er = lax.broadcasted_iota(jnp.int32, (LANES, LANES), 0)
    ec = lax.broadcasted_iota(jnp.int32, (LANES, LANES), 1)
    off = jnp.dot(cnt.astype(BF16), (er < ec).astype(BF16), preferred_element_type=F32)[0:1]
    place = earlier + off
    aux = jnp.zeros((TT, LANES), F32)
    rows = []
    for k in range(TOP_K):
        row = jnp.sum(jnp.where(sels[k], place, 0.0), axis=-1, keepdims=True)
        rows.append(row)
        aux = aux + jnp.where(lane == k, row, 0.0) + jnp.where(lane == TOP_K + k, exps[k] / den, 0.0)
    aux_ref[...] = aux
    cnt_ref[0] = cnt.astype(jnp.int32)

    aux_t = jnp.transpose(aux)
    dest = lax.broadcasted_iota(jnp.int32, (TOP_K * TT, TT), 0).astype(F32)
    disp = jnp.zeros((TOP_K * TT, TT), F32)
    for k in range(TOP_K):
        disp = jnp.where(dest == aux_t[k:k + 1, :], 1.0, disp)
    xs = jnp.dot(disp.astype(BF16), hb, preferred_element_type=F32)
    for s in range(ROW_TILES):
        xs3_ref[pl.ds(s, TOP_K * TT, stride=SUBLANES), :] = xs[:, s * LANES:(s + 1) * LANES]


def _mix_call(x, a_out, o_b, c_out, wo, bo, g1, b1, wr, br, tp):
    n = a_out.shape[0]
    row = lambda w: pl.BlockSpec((TT, w), lambda i: (i, 0))
    xs3 = pl.BlockSpec((TOP_K * TT * SUBLANES, LANES), lambda i: (i, 0))
    cnt = pl.BlockSpec((1, SUBLANES, LANES), lambda i: (i, 0, 0))
    return pl.pallas_call(
        functools.partial(_mix_kernel, n_prompt=tp // TT), grid=(n // TT,),
        in_specs=_tile_row_specs(tp // TT)
        + [row(SEG), pl.BlockSpec((N_PAIRS, TT, LANES), lambda i: (0, i, 0)), row(SEG),
           _full(wo), _full(bo), _full(g1), _full(b1), _full(wr), _full(br)],
        out_specs=[row(D_MODEL), xs3, row(LANES), cnt],
        out_shape=[jax.ShapeDtypeStruct((n, D_MODEL), F32),
                   jax.ShapeDtypeStruct((n * TOP_K * SUBLANES, LANES), F32),
                   jax.ShapeDtypeStruct((n, LANES), F32),
                   jax.ShapeDtypeStruct((n // TT, SUBLANES, LANES), jnp.int32)],
        compiler_params=_params(), name="mix_ln_router")(
            *x, a_out, o_b, c_out, wo, bo, g1, b1, wr, br)


def _rows(start_row, n_rows):
    return pl.ds(pl.multiple_of(start_row * SUBLANES, SUBLANES), n_rows * SUBLANES)


def _expert_gather(tabs, blk, slot, xs3_hbm, buf, sem):
    blk_e, blk_s0, j_lo, j_hi, cnt, off, cum, tot = tabs[:8]
    e = blk_e[blk]
    s0 = blk_s0[blk]
    base = slot * TM

    @pl.when(tot[e] - s0 < TM)
    def _():
        buf[_rows(base, TM), :] = jnp.zeros((TM * SUBLANES, LANES), F32)

    def body(j, carry):
        run = cum[j * N_EXPERTS + e]
        lo = jnp.maximum(run, s0)
        hi = jnp.minimum(run + cnt[j * N_EXPERTS + e], s0 + TM)

        @pl.when(hi > lo)
        def _():
            src = j * (TOP_K * TT) + off[j * N_EXPERTS + e] + (lo - run)
            pltpu.make_async_copy(xs3_hbm.at[_rows(src, hi - lo)],
                                  buf.at[_rows(base + lo - s0, hi - lo)], sem.at[slot]).start()
        return carry

    lax.fori_loop(j_lo[blk], j_hi[blk], body, 0)


def _expert_gather_wait(tabs, blk, slot, xs3_hbm, buf, sem):
    blk_e, blk_s0, tot = tabs[0], tabs[1], tabs[7]
    valid = jnp.minimum(tot[blk_e[blk]] - blk_s0[blk], TM)
    pltpu.make_async_copy(xs3_hbm.at[_rows(0, valid)], buf.at[_rows(slot * TM, valid)],
                          sem.at[slot]).wait()


def _expert_weight_copies(w1_hbm, w2_hbm, expert, w1f, w2f, wslot, wsem):
    return (pltpu.make_async_copy(w1_hbm.at[expert], w1f.at[wslot], wsem.at[0, wslot]),
            pltpu.make_async_copy(w2_hbm.at[expert], w2f.at[wslot], wsem.at[1, wslot]))


def _expert_kernel(*refs, first_expert):
    tabs = refs[:10]
    blk_e, group, next_e = tabs[0], tabs[8], tabs[9]
    (n_used_ref, xs3_hbm, w1_hbm, b1_ref, w2_hbm, b2_ref, y3_ref,
     buf, xb, w1f, w2f, w1b, w2b, sem, wsem) = refs[10:]
    b = pl.program_id(0)
    n_used = n_used_ref[0]
    slot = lax.rem(b, 2)
    used = b < n_used
    wslot = lax.rem(group[b], 2)

    @pl.when(b == 0)
    def _():
        for copy in _expert_weight_copies(w1_hbm, w2_hbm, first_expert + blk_e[0], w1f, w2f, 0, wsem):
            copy.start()

    @pl.when(b == 0)
    def _():
        _expert_gather(tabs, 0, 0, xs3_hbm, buf, sem)

    @pl.when(used)
    def _():
        _expert_gather_wait(tabs, b, slot, xs3_hbm, buf, sem)

    @pl.when(b + 1 < n_used)
    def _():
        _expert_gather(tabs, b + 1, 1 - slot, xs3_hbm, buf, sem)

    new_expert = jnp.logical_or(b == 0, blk_e[b] != blk_e[jnp.maximum(b - 1, 0)])

    @pl.when(jnp.logical_and(used, new_expert))
    def _():
        for copy in _expert_weight_copies(w1_hbm, w2_hbm, first_expert + blk_e[b], w1f, w2f, wslot, wsem):
            copy.wait()

        @pl.when(next_e[b] >= 0)
        def _():
            nxt = first_expert + next_e[b]
            for copy in _expert_weight_copies(w1_hbm, w2_hbm, nxt, w1f, w2f, 1 - wslot, wsem):
                copy.start()

        w1b[...] = w1f[wslot].astype(BF16)
        w2b[...] = w2f[wslot].astype(BF16)

    @pl.when(used)
    def _():
        base = slot * (TM * SUBLANES)
        for s in range(ROW_TILES):
            xb[:, s * LANES:(s + 1) * LANES] = buf[pl.ds(base + s, TM, stride=SUBLANES), :].astype(BF16)
        x = xb[...]
        y = jnp.broadcast_to(b2_ref[0], (TM, D_MODEL))
        for c in range(0, D_FF, FF_CHUNK):
            cols = slice(c, c + FF_CHUNK)
            ucols = slice(D_FF + c, D_FF + c + FF_CHUNK)
            g = jnp.dot(x, w1b[:, cols], preferred_element_type=F32) + b1_ref[0, :, cols]
            u = jnp.dot(x, w1b[:, ucols], preferred_element_type=F32) + b1_ref[0, :, ucols]
            gate = jnp.minimum(g, SWIGLU_LIMIT)
            up = jnp.clip(u, -SWIGLU_LIMIT, SWIGLU_LIMIT)
            hh = (up + 1.0) * (gate * jax.nn.sigmoid(SWIGLU_ALPHA * gate))
            y = y + jnp.dot(hh.astype(BF16), w2b[cols, :], preferred_element_type=F32)
        for s in range(ROW_TILES):
            y3_ref[pl.ds(s, TM, stride=SUBLANES), :] = y[:, s * LANES:(s + 1) * LANES]

    @pl.when(jnp.logical_not(used))
    def _():
        y3_ref[...] = jnp.zeros_like(y3_ref)


def _expert_call(tabs, n_used, xs3, w1, b1, w2, b2, layer):
    n_blocks = tabs[0].shape[0]
    first_expert = layer * N_EXPERTS
    by_expert = lambda shape: pl.BlockSpec(shape, lambda b, e, *_: (first_expert + e[b], 0, 0))
    anywhere = pl.BlockSpec(memory_space=pl.ANY)
    grid_spec = pltpu.PrefetchScalarGridSpec(
        num_scalar_prefetch=len(tabs) + 1, grid=(n_blocks,),
        in_specs=[anywhere, anywhere, by_expert((1, 1, 2 * D_FF)), anywhere, by_expert((1, 1, D_MODEL))],
        out_specs=pl.BlockSpec((TM * SUBLANES, LANES), lambda b, *_: (b, 0)),
        scratch_shapes=[pltpu.VMEM((2 * TM * SUBLANES, LANES), F32),
                        pltpu.VMEM((TM, D_MODEL), BF16),
                        pltpu.VMEM((2, D_MODEL, 2 * D_FF), F32),
                        pltpu.VMEM((2, D_FF, D_MODEL), F32),
                        pltpu.VMEM((D_MODEL, 2 * D_FF), BF16),
                        pltpu.VMEM((D_FF, D_MODEL), BF16),
                        pltpu.SemaphoreType.DMA((2,)),
                        pltpu.SemaphoreType.DMA((2, 2))])
    return pl.pallas_call(
        functools.partial(_expert_kernel, first_expert=first_expert), grid_spec=grid_spec,
        out_shape=jax.ShapeDtypeStruct((n_blocks * TM * SUBLANES, LANES), F32),
        compiler_params=_params(vmem_limit=EXPERT_VMEM_LIMIT), name="expert_ffn")(
            *tabs, n_used, xs3, w1, b1, w2, b2)


PAIRS = TOP_K * TT


def _combine_gather(cnt, off, src, tile, slot, y3_hbm, buf, sem):
    for e in range(N_EXPERTS):
        n = cnt[tile * N_EXPERTS + e]

        @pl.when(n > 0)
        def _(e=e, n=n):
            pltpu.make_async_copy(y3_hbm.at[_rows(src[tile * N_EXPERTS + e], n)],
                                  buf.at[_rows(slot * PAIRS + off[tile * N_EXPERTS + e], n)],
                                  sem.at[slot]).start(priority=e % 2)


def _combine_kernel(cnt, off, src, y3_hbm, h_ref, aux_ref, g2_ref, b2_ref, op_ref, os_ref, buf, ys, sem,
                    *, n_prompt):
    i = pl.program_id(0)
    n_steps = pl.num_programs(0)
    slot = lax.rem(i, 2)

    @pl.when(i == 0)
    def _():
        _combine_gather(cnt, off, src, 0, 0, y3_hbm, buf, sem)

    pltpu.make_async_copy(y3_hbm.at[_rows(0, PAIRS)], buf.at[_rows(slot * PAIRS, PAIRS)],
                          sem.at[slot]).wait()

    @pl.when(i + 1 < n_steps)
    def _():
        _combine_gather(cnt, off, src, i + 1, 1 - slot, y3_hbm, buf, sem)

    base = slot * (PAIRS * SUBLANES)
    for s in range(ROW_TILES):
        ys[:, s * LANES:(s + 1) * LANES] = buf[pl.ds(base + s, PAIRS, stride=SUBLANES), :].astype(BF16)
    aux = aux_ref[...]
    dest = lax.broadcasted_iota(jnp.int32, (TT, PAIRS), 1).astype(F32)
    weights = jnp.zeros((TT, PAIRS), F32)
    for k in range(TOP_K):
        weights = jnp.where(dest == aux[:, k:k + 1], aux[:, TOP_K + k:TOP_K + k + 1], weights)
    moe = jnp.dot(weights.astype(BF16), ys[...], preferred_element_type=F32)
    y = _ln_full(DN_ALPHA * h_ref[...] + moe, g2_ref[...], b2_ref[...])

    @pl.when(i < n_prompt)
    def _():
        op_ref[...] = y

    @pl.when(i >= n_prompt)
    def _():
        os_ref[...] = y


def _combine_call(cnt, off, src, y3, h, aux, g2, b2, tp):
    n = h.shape[0]
    row = lambda w: pl.BlockSpec((TT, w), lambda i, *_: (i, 0))
    n_prompt = tp // TT
    out_specs = _tile_row_specs(n_prompt)
    out_shape = [jax.ShapeDtypeStruct((tp, D_MODEL), F32), jax.ShapeDtypeStruct((n - tp, D_MODEL), F32)]
    grid_spec = pltpu.PrefetchScalarGridSpec(
        num_scalar_prefetch=3, grid=(n // TT,),
        in_specs=[pl.BlockSpec(memory_space=pl.ANY), row(D_MODEL), row(LANES),
                  pl.BlockSpec(g2.shape, lambda i, *_: (0, 0)), pl.BlockSpec(b2.shape, lambda i, *_: (0, 0))],
        out_specs=out_specs,
        scratch_shapes=[pltpu.VMEM((2 * PAIRS * SUBLANES, LANES), F32),
                        pltpu.VMEM((PAIRS, D_MODEL), BF16),
                        pltpu.SemaphoreType.DMA((2,))])
    return pl.pallas_call(
        functools.partial(_combine_kernel, n_prompt=n_prompt), grid_spec=grid_spec, out_shape=out_shape,
        compiler_params=_params(), name="combine_ln")(cnt, off, src, y3, h, aux, g2, b2)


def _routing_tables(cnt_tiles, n):
    cnt = cnt_tiles[:, 0, :N_EXPERTS]
    off = jnp.cumsum(cnt, axis=1) - cnt
    cum = jnp.cumsum(cnt, axis=0) - cnt
    tot = jnp.sum(cnt, axis=0)
    padded = (tot + TM - 1) // TM * TM
    pends = jnp.cumsum(padded)
    pstart = pends - padded
    n_blocks = n * TOP_K // TM + N_EXPERTS
    blk_start = jnp.arange(n_blocks, dtype=jnp.int32) * TM
    blk_e = jnp.minimum(jnp.sum((pends[None, :] <= blk_start[:, None]).astype(jnp.int32), axis=1),
                        N_EXPERTS - 1)
    ids = jnp.arange(N_EXPERTS, dtype=jnp.int32)
    mine = blk_e[:, None] == ids[None, :]
    pick = lambda per_expert: jnp.sum(jnp.where(mine, per_expert[None, :], 0), axis=1)
    blk_s0 = blk_start - pick(pstart)
    lo3, hi3, mine3 = blk_s0[:, None, None], blk_s0[:, None, None] + TM, mine[:, None, :]
    j_lo = jnp.sum(jnp.logical_and(mine3, (cum + cnt)[None] <= lo3).astype(jnp.int32), axis=(1, 2))
    j_hi = jnp.sum(jnp.logical_and(mine3, cum[None] < hi3).astype(jnp.int32), axis=(1, 2))
    n_used = (pends[-1:] // TM).astype(jnp.int32)
    has = tot > 0
    group_of = jnp.cumsum(has.astype(jnp.int32)) - 1
    later = jnp.where(jnp.logical_and(has[None, :], ids[None, :] > ids[:, None]), ids[None, :], N_EXPERTS)
    next_of = jnp.min(later, axis=1)
    next_of = jnp.where(next_of == N_EXPERTS, -1, next_of)
    flat = lambda a: a.reshape(-1).astype(jnp.int32)
    expert_tabs = (flat(blk_e), flat(blk_s0), flat(j_lo), flat(j_hi), flat(cnt), flat(off), flat(cum), flat(tot),
                   flat(pick(group_of)), flat(pick(next_of)))
    combine_tabs = (flat(cnt), flat(off), flat(pstart[None, :] + cum))
    return expert_tabs, n_used, combine_tabs


def _pad_cols(a, width):
    return jnp.pad(a, [(0, 0)] * (a.ndim - 1) + [(0, width - a.shape[-1])])


def _layer_params(l, w_in, b_in, w_dw, b_dw, ln_a_g, ln_a_b, ln_c_g, ln_c_b, w_sp, b_sp,
                  w_out, b_out, ln1_g, ln1_b, w_router, b_router, ln2_g, ln2_b):
    bounds = [0, C_B, 2 * C_B, 3 * C_B, 3 * C_B + C_A, 3 * C_B + 2 * C_A, 3 * C_B + 2 * C_A + C_C,
              3 * C_B + 2 * C_A + 2 * C_C]
    w_segs = [_pad_cols(w_in[l][:, bounds[i]:bounds[i + 1]], SEG) for i in range(N_SEG)]
    b_segs = [_pad_cols(b_in[l][bounds[i]:bounds[i + 1]], SEG) for i in range(N_SEG)]
    row = lambda a: _pad_cols(a, SEG)[None, :]
    wo = w_out[l]
    pad_rows = lambda a: jnp.pad(a, ((0, SEG - a.shape[0]), (0, 0)))
    bsp = b_sp[l]
    bfull = _pad_cols(jnp.repeat(bsp.T, HEAD_DIM, axis=1), SEG)
    wr = _pad_cols(w_router[l], LANES)
    wr_hi = wr.astype(BF16)
    return dict(
        w_in=jnp.concatenate(w_segs, axis=1).astype(BF16),
        b_in=jnp.concatenate(b_segs)[None, :],
        w_dw=jnp.pad(w_dw[l], ((0, HALO - CONV_W), (0, SEG - C_A))),
        b_dw=row(b_dw[l]), ln_a_g=row(ln_a_g[l]), ln_a_b=row(ln_a_b[l]),
        ln_c_g=row(ln_c_g[l]), ln_c_b=row(ln_c_b[l]),
        w_sp=w_sp[l], b_full=bfull,
        w_out=jnp.concatenate([pad_rows(wo[:C_A]), wo[C_A:C_A + C_B], pad_rows(wo[C_A + C_B:])]).astype(BF16),
        b_out=b_out[l][None, :],
        ln1_g=ln1_g[l][None, :], ln1_b=ln1_b[l][None, :],
        w_router=jnp.concatenate([wr_hi, (wr - wr_hi.astype(F32)).astype(BF16)], axis=1),
        b_router=jnp.concatenate([b_router[l], jnp.full((LANES - N_EXPERTS,), NEG, F32)])[None, :],
        ln2_g=ln2_g[l][None, :], ln2_b=ln2_b[l][None, :])


def _layer(x, p, hist_p, caches, experts, biases, n, tp, layer):
    conv_args = (p["w_dw"], p["b_dw"], p["ln_a_g"], p["ln_a_b"])
    q, k, v, a, a_out, c_out, vn_s = _proj_call(x, p["w_in"], p["b_in"], p["ln_c_g"], p["ln_c_b"],
                                                p["w_sp"], p["b_full"], *conv_args, n, tp)
    a_out, new_a_s = _conv_sample_call(hist_p, a, *conv_args, a_out, tp)

    o_b = _attn_prompt_call(q, k, v, biases["prompt"], tp)
    o_b = _attn_sample_call(q, k, v, *caches, layer, *biases["sample"], o_b, tp)

    h, xs3, aux, cnt_tiles = _mix_call(x, a_out, o_b, c_out, p["w_out"], p["b_out"],
                                       p["ln1_g"], p["ln1_b"], p["w_router"], p["b_router"], tp)
    expert_tabs, n_used, combine_tabs = _routing_tables(cnt_tiles, n)
    y3 = _expert_call(expert_tabs, n_used, xs3, *experts, layer)
    y = _combine_call(*combine_tabs, y3, h, aux, p["ln2_g"], p["ln2_b"], tp)
    return y, a, k, v, vn_s, new_a_s


def kernel(x_prompt, x_sample, state_a_conv, cache_b_k, cache_b_v, w_in, b_in, w_dw, b_dw, ln_a_g, ln_a_b, ln_c_g, ln_c_b, w_sp, b_sp, w_out, b_out, ln1_g, ln1_b, w_router, b_router, w1, b1, w2, b2, ln2_g, ln2_b):
    bp, tp, _ = x_prompt.shape
    bs, ts, _ = x_sample.shape
    n_cache = cache_b_k.shape[2]
    assert bp == 1 and ts == DEC_SEQ and bs * ts == TT and tp % WIN_MAX == 0
    assert n_cache == WIN_MAX
    n = tp + bs * ts
    keep = min(WIN_MAX, tp)
    hist = CONV_W - 1

    weights = (w_in, b_in, w_dw, b_dw, ln_a_g, ln_a_b, ln_c_g, ln_c_b, w_sp, b_sp, w_out, b_out,
               ln1_g, ln1_b, w_router, b_router, ln2_g, ln2_b)
    biases = {"prompt": [_prompt_bias(d) for d in DILATIONS], "sample": _sample_bias(n_cache)}
    experts = (w1.reshape(DEPTH * N_EXPERTS, D_MODEL, 2 * D_FF), b1.reshape(DEPTH * N_EXPERTS, 1, 2 * D_FF),
               w2.reshape(DEPTH * N_EXPERTS, D_FF, D_MODEL), b2.reshape(DEPTH * N_EXPERTS, 1, D_MODEL))
    to_feature_major = lambda c: jnp.transpose(c, (0, 1, 3, 4, 2)).reshape(DEPTH, bs, C_B, n_cache)
    caches = (to_feature_major(cache_b_k), to_feature_major(cache_b_v))
    heads = lambda a: jnp.transpose(a, (1, 0, 2)).reshape(a.shape[1], N_HEADS_B, HEAD_DIM)

    x = (x_prompt[0], x_sample.reshape(bs * ts, D_MODEL))
    outs = {name: [] for name in ("a_p", "a_s", "k_p", "v_p", "k_s", "v_s", "c_s")}
    for l in range(DEPTH):
        p = _layer_params(l, *weights)
        hist_p = jnp.pad(state_a_conv[l], ((0, 0), (HALO - hist, 0), (0, SEG - C_A)))
        x, a, k, v, vn_s, new_a_s = _layer(x, p, hist_p, caches, experts, biases, n, tp, l)
        outs["a_p"].append(a[tp - hist:tp, :C_A][None])
        outs["a_s"].append(new_a_s[:, HALO - hist:, :C_A])
        outs["k_p"].append(heads(k[:, tp - keep:tp])[None])
        outs["v_p"].append(heads(v[:, tp - keep:tp])[None])
        outs["k_s"].append(heads(k[:, tp:]).reshape(bs, ts, N_HEADS_B, HEAD_DIM))
        outs["v_s"].append(heads(v[:, tp:]).reshape(bs, ts, N_HEADS_B, HEAD_DIM))
        outs["c_s"].append(vn_s[:, :C_C].reshape(bs, ts, C_C))
    stack = lambda name: jnp.stack(outs[name])
    y_prompt, y_sample = x
    return (y_prompt[None], y_sample.reshape(bs, ts, D_MODEL), stack("a_p"), stack("a_s"),
            stack("k_p"), stack("v_p"), stack("k_s"), stack("v_s"), stack("c_s"))
```
